```python
import jax, jax.numpy as jnp
from jax import lax
import numpy as np

D_MODEL = 1024
BATCH = 1
SEQ = 16384
DEPTH = 2
DEC_BATCH = 32
DEC_SEQ = 1
PAST_LEN = 16384
PAGE_SIZE = 128

D_MIX = D_MODEL
D_RNN = D_MIX // 2
RG_BLOCKS = 8
RG_BW = D_RNN // RG_BLOCKS
RG_CONV = 4
RG_C = 8.0
D_SSM = D_MIX - D_RNN
S5_GROUP = 16
S5_GROUPS = D_SSM // S5_GROUP
S5_STATE = 64
S5_DT_MIN = 1e-3
S5_DT_MAX = 1e-1
N_HEADS = 16
HEAD_DIM = D_MODEL // N_HEADS
Q_BLOCK = 128
D_FF = ((8 * D_MODEL // 3) + 127) // 128 * 128
FFN_CONV = 3
ALPHA = (2.0 * DEPTH) ** 0.25
BETA = (8.0 * DEPTH) ** -0.25
LN_EPS = 1e-5

kernel_name = "hawk_s5_fox_convffn_deepnorm_step"


def layer_norm(x, g, b):
    xf = x.astype(jnp.float32)
    xc = xf - jnp.mean(xf, -1, keepdims=True)
    var = jnp.mean(xc * xc, -1, keepdims=True)
    y = xc * lax.rsqrt(var + LN_EPS) * g.astype(jnp.float32) + b.astype(jnp.float32)
    return y.astype(x.dtype)


def causal_dwconv(u, buf, w, b):
    width = w.shape[0]
    t = u.shape[1]
    ext = jnp.concatenate([buf.astype(u.dtype), u], axis=1)
    y = b
    for j in range(width):
        y = y + ext[:, j:j + t] * w[j]
    return y, ext[:, t:]


def linear_scan(a, b, h0):
    b = b.at[:, 0].add(a[:, 0] * h0)
    def combine(l, r):
        return r[0] * l[0], r[0] * l[1] + r[1]
    return lax.associative_scan(combine, (a, b), axis=1)[1]


def block_diag(x, w, b):
    xs = x.reshape(x.shape[:-1] + (RG_BLOCKS, RG_BW))
    return jnp.einsum('nthi,hij->nthj', xs, w).reshape(x.shape) + b


def rglru(xc, h0, w_a, b_a, w_x, b_x, lam):
    f32 = jnp.float32
    xf = xc.astype(f32)
    r = jax.nn.sigmoid(block_diag(xf, w_a.astype(f32), b_a.astype(f32)))
    i = jax.nn.sigmoid(block_diag(xf, w_x.astype(f32), b_x.astype(f32)))
    log_a = -RG_C * r * jax.nn.softplus(-lam.astype(f32))
    a = jnp.exp(log_a)
    inp = jnp.sqrt(-jnp.expm1(2.0 * log_a)) * (i * xf)
    h = linear_scan(a, inp, h0.astype(f32))
    return h, h[:, -1]


def s5_mixer(u, s_re, s_im, lam_re, lam_im, log_dt, b_re, b_im, c_re, c_im, d, w_glu, b_glu):
    f32 = jnp.float32
    n, t, _ = u.shape
    uf = u.astype(f32)
    ug = uf.reshape(n, t, S5_GROUPS, S5_GROUP)
    lr = jnp.minimum(lam_re.astype(f32), -1e-4)
    li = lam_im.astype(f32)
    dt = jnp.exp(log_dt.astype(f32))[:, None]
    mag = jnp.exp(lr * dt)
    ab_re = mag * jnp.cos(li * dt)
    ab_im = mag * jnp.sin(li * dt)
    den = lr * lr + li * li
    nr = ab_re - 1.0
    zr = (nr * lr + ab_im * li) / den
    zi = (ab_im * lr - nr * li) / den
    br = b_re.astype(f32)
    bi = b_im.astype(f32)
    bb_re = zr[..., None] * br - zi[..., None] * bi
    bb_im = zr[..., None] * bi + zi[..., None] * br
    bu_re = jnp.einsum('ntgm,gpm->ntgp', ug, bb_re)
    bu_im = jnp.einsum('ntgm,gpm->ntgp', ug, bb_im)
    sr = s_re.astype(f32)
    si = s_im.astype(f32)
    bu_re = bu_re.at[:, 0].add(ab_re * sr - ab_im * si)
    bu_im = bu_im.at[:, 0].add(ab_re * si + ab_im * sr)
    a_re = jnp.broadcast_to(ab_re, bu_re.shape)
    a_im = jnp.broadcast_to(ab_im, bu_im.shape)
    def combine(l, r):
        l_ar, l_ai, l_br, l_bi = l
        r_ar, r_ai, r_br, r_bi = r
        return (r_ar * l_ar - r_ai * l_ai, r_ar * l_ai + r_ai * l_ar,
                r_ar * l_br - r_ai * l_bi + r_br, r_ar * l_bi + r_ai * l_br + r_bi)
    _, _, x_re, x_im = lax.associative_scan(combine, (a_re, a_im, bu_re, bu_im), axis=1)
    y = (jnp.einsum('ntgp,gmp->ntgm', x_re, c_re.astype(f32))
         - jnp.einsum('ntgp,gmp->ntgm', x_im, c_im.astype(f32)))
    y = y.reshape(n, t, D_SSM) + d.astype(f32) * uf
    g = jax.nn.gelu(y)
    out = g * jax.nn.sigmoid(g @ w_glu.astype(f32) + b_glu.astype(f32))
    return out.astype(u.dtype), x_re[:, -1], x_im[:, -1]


def fox_prompt(q, k, v, logf):
    n, t, h, dh = q.shape
    scale = dh ** -0.5
    ft = jnp.cumsum(logf, axis=1).transpose(0, 2, 1)
    nb = t // Q_BLOCK
    qb = q.reshape(n, nb, Q_BLOCK, h, dh).transpose(1, 0, 2, 3, 4)
    fb = ft.reshape(n, h, nb, Q_BLOCK).transpose(2, 0, 1, 3)
    kpos = jnp.arange(t)
    def one_block(args):
        qi, fi, bi = args
        s = jnp.einsum('nqhd,nkhd->nhqk', qi, k).astype(jnp.float32) * scale
        s = s + fi[..., None] - ft[:, :, None, :]
        qpos = bi * Q_BLOCK + jnp.arange(Q_BLOCK)
        s = jnp.where(kpos[None, :] <= qpos[:, None], s, -jnp.inf)
        p = jax.nn.softmax(s, axis=-1)
        return jnp.einsum('nhqk,nkhd->nqhd', p.astype(v.dtype), v)
    o = lax.map(one_block, (qb, fb, jnp.arange(nb)))
    return o.transpose(1, 0, 2, 3, 4).reshape(n, t, h, dh)


def fox_sample(q, k, v, logf, k_past, v_past, logf_past):
    tn = q.shape[1]
    past = k_past.shape[1]
    scale = q.shape[-1] ** -0.5
    lp = logf_past.astype(jnp.float32)
    suffix = (lax.cumsum(lp, axis=1, reverse=True) - lp).transpose(0, 2, 1)
    gt = jnp.cumsum(logf, axis=1).transpose(0, 2, 1)
    s_past = jnp.einsum('nqhd,nkhd->nhqk', q, k_past).astype(jnp.float32) * scale
    s_past = s_past + gt[..., None] + suffix[:, :, None, :]
    s_new = jnp.einsum('nqhd,nkhd->nhqk', q, k).astype(jnp.float32) * scale
    s_new = s_new + gt[..., None] - gt[:, :, None, :]
    causal = jnp.arange(tn)[None, :] <= jnp.arange(tn)[:, None]
    s_new = jnp.where(causal, s_new, -jnp.inf)
    p = jax.nn.softmax(jnp.concatenate([s_past, s_new], axis=-1), axis=-1)
    o = (jnp.einsum('nhqk,nkhd->nqhd', p[..., :past].astype(v.dtype), v_past)
         + jnp.einsum('nhqk,nkhd->nqhd', p[..., past:].astype(v.dtype), v))
    return o


def fox_mixer(x, attend, w_in1, b_fgt, w_out1):
    n, t, _ = x.shape
    proj = x @ w_in1
    q, k, v, fl = jnp.split(proj, [D_MODEL, 2 * D_MODEL, 3 * D_MODEL], axis=-1)
    shp = (n, t, N_HEADS, HEAD_DIM)
    q = q.reshape(shp)
    k = k.reshape(shp)
    v = v.reshape(shp)
    logf = jax.nn.log_sigmoid(fl.astype(jnp.float32) + b_fgt.astype(jnp.float32))
    o = attend(q, k, v, logf)
    return o.reshape(n, t, D_MODEL) @ w_out1, k, v, logf


def conv_ffn(x, buf, w_up, w_gate, cw, cb, w_down):
    hc, new_buf = causal_dwconv(x @ w_up, buf, cw, cb)
    return (jax.nn.gelu(hc) * (x @ w_gate)) @ w_down, new_buf


def trunk(x, rg_conv, rg_h, s5_re, s5_im, ffn_conv, attend, p):
    ffn_bufs = []
    for layer in range(DEPTH):
        if layer % 2 == 0:
            proj = x @ p['w_in0']
            xa, ga, xb = jnp.split(proj, [D_RNN, 2 * D_RNN], axis=-1)
            xc, rg_conv = causal_dwconv(xa, rg_conv, p['rg_conv_w'], p['rg_conv_b'])
            h, rg_h = rglru(xc, rg_h, p['rg_wa'], p['rg_ba'], p['rg_wx'], p['rg_bx'], p['rg_lambda'])
            ya = h.astype(x.dtype) * jax.nn.gelu(ga)
            yb, s5_re, s5_im = s5_mixer(xb, s5_re, s5_im, p['s5_lambda_re'], p['s5_lambda_im'],
                                        p['s5_log_dt'], p['s5_b_re'], p['s5_b_im'], p['s5_c_re'],
                                        p['s5_c_im'], p['s5_d'], p['s5_w_glu'], p['s5_b_glu'])
            y = jnp.concatenate([ya, yb], axis=-1) @ p['w_out0']
        else:
            y, k, v, logf = fox_mixer(x, attend, p['w_in1'], p['b_fgt'], p['w_out1'])
        x = layer_norm(ALPHA * x + y, p['ln_g'][layer, 0], p['ln_b'][layer, 0])
        f, buf = conv_ffn(x, ffn_conv[layer], p['ffn_w_up'][layer], p['ffn_w_gate'][layer],
                          p['ffn_conv_w'][layer], p['ffn_conv_b'][layer], p['ffn_w_down'][layer])
        ffn_bufs.append(buf)
        x = layer_norm(ALPHA * x + f, p['ln_g'][layer, 1], p['ln_b'][layer, 1])
    return x, rg_conv, rg_h, s5_re, s5_im, k, v, logf, jnp.stack(ffn_bufs, axis=0)


def setup_inputs(seed: int = 0) -> dict:
    key = jax.random.key(seed)
    ks = iter(jax.random.split(key, 48))
    def nrm(shape, scale=1.0):
        return jax.random.normal(next(ks), shape, jnp.float32) * scale
    def uni(shape, lo, hi):
        return jax.random.uniform(next(ks), shape, jnp.float32, lo, hi)
    n_pages = PAST_LEN // PAGE_SIZE
    n_used = DEC_BATCH * n_pages
    n_pool = n_used + n_used // 4
    page_table = jax.random.permutation(next(ks), n_pool)[:n_used].reshape(DEC_BATCH, n_pages).astype(jnp.int32)
    a_c = uni((D_RNN,), 0.9, 0.999)
    sig = a_c ** (1.0 / RG_C)
    rg_lambda = jnp.log(sig) - jnp.log1p(-sig)
    lam_im = jnp.pi * jnp.broadcast_to(jnp.arange(S5_STATE, dtype=jnp.float32), (S5_GROUPS, S5_STATE))
    return {
        'x_prompt': nrm((BATCH, SEQ, D_MODEL)),
        'x_sample': nrm((DEC_BATCH, DEC_SEQ, D_MODEL)),
        'state_rglru_conv': nrm((DEC_BATCH, RG_CONV - 1, D_RNN)),
        'state_rglru_h': nrm((DEC_BATCH, D_RNN), 0.5),
        'state_s5_re': nrm((DEC_BATCH, S5_GROUPS, S5_STATE), 0.5),
        'state_s5_im': nrm((DEC_BATCH, S5_GROUPS, S5_STATE), 0.5),
        'cache_k': nrm((n_pool, PAGE_SIZE, N_HEADS, HEAD_DIM)),
        'cache_v': nrm((n_pool, PAGE_SIZE, N_HEADS, HEAD_DIM)),
        'cache_logf': jax.nn.log_sigmoid(uni((n_pool, PAGE_SIZE, N_HEADS), 2.0, 6.0)
                                         + nrm((n_pool, PAGE_SIZE, N_HEADS))),
        'state_ffn_conv': nrm((DEPTH, DEC_BATCH, FFN_CONV - 1, D_FF)),
        'page_table': page_table,
        'w_in0': nrm((D_MODEL, 2 * D_RNN + D_SSM), D_MODEL ** -0.5),
        'rg_conv_w': nrm((RG_CONV, D_RNN), RG_CONV ** -0.5),
        'rg_conv_b': nrm((D_RNN,), 0.02),
        'rg_wa': nrm((RG_BLOCKS, RG_BW, RG_BW), RG_BW ** -0.5),
        'rg_ba': nrm((D_RNN,), 0.1),
        'rg_wx': nrm((RG_BLOCKS, RG_BW, RG_BW), RG_BW ** -0.5),
        'rg_bx': nrm((D_RNN,), 0.1),
        'rg_lambda': rg_lambda,
        's5_lambda_re': -0.5 + nrm((S5_GROUPS, S5_STATE), 0.01),
        's5_lambda_im': lam_im + nrm((S5_GROUPS, S5_STATE), 0.01),
        's5_log_dt': uni((S5_GROUPS,), float(np.log(S5_DT_MIN)), float(np.log(S5_DT_MAX))),
        's5_b_re': nrm((S5_GROUPS, S5_STATE, S5_GROUP), (2.0 * S5_GROUP) ** -0.5),
        's5_b_im': nrm((S5_GROUPS, S5_STATE, S5_GROUP), (2.0 * S5_GROUP) ** -0.5),
        's5_c_re': nrm((S5_GROUPS, S5_GROUP, S5_STATE), (2.0 * S5_STATE) ** -0.5),
        's5_c_im': nrm((S5_GROUPS, S5_GROUP, S5_STATE), (2.0 * S5_STATE) ** -0.5),
        's5_d': nrm((D_SSM,)),
        's5_w_glu': nrm((D_SSM, D_SSM), D_SSM ** -0.5),
        's5_b_glu': nrm((D_SSM,), 0.02),
        'w_out0': nrm((D_MIX, D_MODEL), BETA * D_MIX ** -0.5),
        'w_in1': nrm((D_MODEL, 3 * D_MODEL + N_HEADS), D_MODEL ** -0.5),
        'b_fgt': uni((N_HEADS,), 2.0, 6.0),
        'w_out1': nrm((D_MODEL, D_MODEL), BETA * D_MODEL ** -0.5),
        'ffn_w_up': nrm((DEPTH, D_MODEL, D_FF), D_MODEL ** -0.5),
        'ffn_w_gate': nrm((DEPTH, D_MODEL, D_FF), D_MODEL ** -0.5),
        'ffn_conv_w': nrm((DEPTH, FFN_CONV, D_FF), FFN_CONV ** -0.5),
        'ffn_conv_b': nrm((DEPTH, D_FF), 0.02),
        'ffn_w_down': nrm((DEPTH, D_FF, D_MODEL), BETA * D_FF ** -0.5),
        'ln_g': 1.0 + nrm((DEPTH, 2, D_MODEL), 0.05),
        'ln_b': nrm((DEPTH, 2, D_MODEL), 0.05),
    }


def reference(x_prompt, x_sample, state_rglru_conv, state_rglru_h, state_s5_re, state_s5_im,
              cache_k, cache_v, cache_logf, state_ffn_conv, page_table,
              w_in0, rg_conv_w, rg_conv_b, rg_wa, rg_ba, rg_wx, rg_bx, rg_lambda,
              s5_lambda_re, s5_lambda_im, s5_log_dt, s5_b_re, s5_b_im, s5_c_re, s5_c_im,
              s5_d, s5_w_glu, s5_b_glu, w_out0, w_in1, b_fgt, w_out1,
              ffn_w_up, ffn_w_gate, ffn_conv_w, ffn_conv_b, ffn_w_down, ln_g, ln_b):
    p = dict(w_in0=w_in0, rg_conv_w=rg_conv_w, rg_conv_b=rg_conv_b, rg_wa=rg_wa, rg_ba=rg_ba,
             rg_wx=rg_wx, rg_bx=rg_bx, rg_lambda=rg_lambda, s5_lambda_re=s5_lambda_re,
             s5_lambda_im=s5_lambda_im, s5_log_dt=s5_log_dt, s5_b_re=s5_b_re, s5_b_im=s5_b_im,
             s5_c_re=s5_c_re, s5_c_im=s5_c_im, s5_d=s5_d, s5_w_glu=s5_w_glu, s5_b_glu=s5_b_glu,
             w_out0=w_out0, w_in1=w_in1, b_fgt=b_fgt, w_out1=w_out1, ffn_w_up=ffn_w_up,
             ffn_w_gate=ffn_w_gate, ffn_conv_w=ffn_conv_w, ffn_conv_b=ffn_conv_b,
             ffn_w_down=ffn_w_down, ln_g=ln_g, ln_b=ln_b)

    nb_ = x_prompt.shape[0]
    dt_ = x_prompt.dtype
    (y_prompt, p_rg_conv, p_rg_h, p_s5_re, p_s5_im, p_k, p_v, p_logf, p_ffn_conv) = trunk(
        x_prompt,
        jnp.zeros((nb_, RG_CONV - 1, D_RNN), dt_),
        jnp.zeros((nb_, D_RNN), jnp.float32),
        jnp.zeros((nb_, S5_GROUPS, S5_STATE), jnp.float32),
        jnp.zeros((nb_, S5_GROUPS, S5_STATE), jnp.float32),
        jnp.zeros((DEPTH, nb_, FFN_CONV - 1, D_FF), dt_),
        fox_prompt, p)

    n_seq, n_pages = page_table.shape
    past = n_pages * cache_k.shape[1]
    k_past = cache_k[page_table].reshape(n_seq, past, N_HEADS, HEAD_DIM)
    v_past = cache_v[page_table].reshape(n_seq, past, N_HEADS, HEAD_DIM)
    lf_past = cache_logf[page_table].reshape(n_seq, past, N_HEADS)

    def attend_sample(q, k, v, logf):
        return fox_sample(q, k, v, logf, k_past, v_past, lf_past)

    (y_sample, s_rg_conv, s_rg_h, s_s5_re, s_s5_im, s_k, s_v, s_logf, s_ffn_conv) = trunk(
        x_sample, state_rglru_conv, state_rglru_h, state_s5_re, state_s5_im, state_ffn_conv,
        attend_sample, p)

    return (y_prompt, y_sample,
            p_rg_conv, p_rg_h, p_s5_re, p_s5_im, p_k, p_v, p_logf, p_ffn_conv,
            s_rg_conv, s_rg_h, s_s5_re, s_s5_im, s_k, s_v, s_logf, s_ffn_conv)
```

```python
import functools

import jax
import jax.numpy as jnp
from jax import lax
from jax.experimental import pallas as pl
from jax.experimental.pallas import tpu as pltpu

F32 = jnp.float32
BF16 = jnp.bfloat16

D_MODEL = 1024
DEPTH = 2
PAGE_SIZE = 128
D_RNN = 512
RG_BLOCKS = 8
RG_BW = D_RNN // RG_BLOCKS
RG_C = 8.0
D_SSM = 512
S5_GROUP = 16
S5_GROUPS = D_SSM // S5_GROUP
S5_STATE = 64
N_STATE = S5_GROUPS * S5_STATE
N_HEADS = 16
HEAD_DIM = D_MODEL // N_HEADS
D_FF = 2816
ALPHA = (2.0 * DEPTH) ** 0.25
LN_EPS = 1e-5

LANES = 128
GROUPS_PER_BLOCK = LANES // S5_GROUP
N_SSM_BLOCKS = D_SSM // LANES
STATE_BLOCK = GROUPS_PER_BLOCK * S5_STATE
FF_CHUNK = 256
N_FF_CHUNKS = D_FF // FF_CHUNK
NEG_BIG = -1e30
AUG_COLS = 6
VMEM_LIMIT = 56 * 1024 * 1024


def _cparams(n_axes=1, vmem=VMEM_LIMIT):
    return pltpu.CompilerParams(dimension_semantics=("arbitrary",) * n_axes,
                                vmem_limit_bytes=vmem)


def _split3(x):
    h1 = x.astype(BF16)
    r1 = x - h1.astype(F32)
    h2 = r1.astype(BF16)
    h3 = (r1 - h2.astype(F32)).astype(BF16)
    return h1, h2, h3


def _dotw(x, w):
    if w.dtype == BF16:
        return jnp.dot(x.astype(BF16), w, preferred_element_type=F32)
    xh = x.astype(BF16)
    xl = (x - xh.astype(F32)).astype(BF16)
    wh = w.astype(BF16)
    wl = (w - wh.astype(F32)).astype(BF16)
    return (jnp.dot(xh, wh, preferred_element_type=F32)
            + jnp.dot(xl, wh, preferred_element_type=F32)
            + jnp.dot(xh, wl, preferred_element_type=F32))


def _dot_exact01(m01, x):
    h1, h2, h3 = _split3(x)
    return (jnp.dot(m01, h1, preferred_element_type=F32)
            + jnp.dot(m01, h2, preferred_element_type=F32)
            + jnp.dot(m01, h3, preferred_element_type=F32))


def _gelu(x):
    return jax.nn.gelu(x)


def _sigmoid(x):
    return jax.nn.sigmoid(x)


def _log_sigmoid(x):
    return jnp.minimum(x, 0.0) - jnp.log1p(jnp.exp(-jnp.abs(x)))


def _expm1(x):
    u = jnp.exp(x)
    um1 = u - 1.0
    return jnp.where(um1 == 0.0, x, jnp.where(um1 == -1.0, -1.0, um1 * x / jnp.log(u)))


def _layer_norm(xf, g, b):
    mean = jnp.mean(xf, axis=-1, keepdims=True)
    xc = xf - mean
    var = jnp.mean(xc * xc, axis=-1, keepdims=True)
    return xc * lax.rsqrt(var + LN_EPS) * g + b


def _rglru_gates(xc, wa, ba, wx, bx, c_lam):
    r = _sigmoid(_dotw(xc, wa) + ba)
    i = _sigmoid(_dotw(xc, wx) + bx)
    log_a = c_lam * r
    a = jnp.exp(log_a)
    inp = jnp.sqrt(-_expm1(2.0 * log_a)) * (i * xc)
    return a, inp


def _scan_rows(a, b):
    n = a.shape[0]
    row = lax.broadcasted_iota(jnp.int32, a.shape, 0)
    d = 1
    while d < n:
        valid = row >= d
        b = b + jnp.where(valid, a * pltpu.roll(b, d, 0), 0.0)
        if 2 * d < n:
            a = jnp.where(valid, a * pltpu.roll(a, d, 0), a)
        d *= 2
    return b


def _scan_rows_const(ar, ai, br, bi):
    n = br.shape[0]
    row = lax.broadcasted_iota(jnp.int32, br.shape, 0)
    d = 1
    while d < n:
        valid = row >= d
        sr = pltpu.roll(br, d, 0)
        si = pltpu.roll(bi, d, 0)
        br, bi = (br + jnp.where(valid, ar * sr - ai * si, 0.0),
                  bi + jnp.where(valid, ar * si + ai * sr, 0.0))
        ar, ai = ar * ar - ai * ai, 2.0 * ar * ai
        d *= 2
    return br, bi


def _rglru_prompt_kernel(x_ref, w_ref, cw_ref, cb_ref, wa_ref, ba_ref, wx_ref, bx_ref, cl_ref,
                         ya_ref, tail_ref, h_ref, ext_scr, h_scr, *, tb):
    step = pl.program_id(0)

    @pl.when(step == 0)
    def _():
        ext_scr[0:8, :] = jnp.zeros((8, D_RNN), F32)
        h_scr[...] = jnp.zeros_like(h_scr)

    proj = _dotw(x_ref[...], w_ref[...])
    xa = proj[:, :D_RNN]
    ga = proj[:, D_RNN:]
    ext_scr[8:8 + tb, :] = xa
    xc = cb_ref[...] + xa * cw_ref[3:4, :]
    for j in range(3):
        xc = xc + ext_scr[pl.ds(5 + j, tb), :] * cw_ref[j:j + 1, :]
    tail = ext_scr[tb:tb + 8, :]
    ext_scr[0:8, :] = tail
    tail_ref[...] = tail

    a, inp = _rglru_gates(xc, wa_ref[...], ba_ref[...], wx_ref[...], bx_ref[...], cl_ref[...])
    row = lax.broadcasted_iota(jnp.int32, a.shape, 0)
    inp = inp + jnp.where(row == 0, a * h_scr[0:1, :], 0.0)
    h = _scan_rows(a, inp)
    h_last = jnp.broadcast_to(h[tb - 1:tb, :], (8, D_RNN))
    h_scr[...] = h_last
    h_ref[...] = h_last
    ya_ref[...] = h * _gelu(ga)


def _rglru_prompt(x, w_ag, cw, cb, wa, ba, wx, bx, c_lam, tb):
    t = x.shape[0]
    full = lambda shape: pl.BlockSpec(shape, lambda i: (0,) * len(shape))
    return pl.pallas_call(
        functools.partial(_rglru_prompt_kernel, tb=tb),
        grid=(t // tb,),
        in_specs=[pl.BlockSpec((tb, D_MODEL), lambda i: (i, 0)),
                  full(w_ag.shape), full(cw.shape), full(cb.shape), full(wa.shape), full(ba.shape),
                  full(wx.shape), full(bx.shape), full(c_lam.shape)],
        out_specs=[pl.BlockSpec((tb, D_RNN), lambda i: (i, 0)),
                   pl.BlockSpec((8, D_RNN), lambda i: (0, 0)),
                   pl.BlockSpec((8, D_RNN), lambda i: (0, 0))],
        out_shape=[jax.ShapeDtypeStruct((t, D_RNN), F32),
                   jax.ShapeDtypeStruct((8, D_RNN), F32),
                   jax.ShapeDtypeStruct((8, D_RNN), F32)],
        scratch_shapes=[pltpu.VMEM((tb + 8, D_RNN), F32), pltpu.VMEM((8, D_RNN), F32)],
        compiler_params=_cparams(),
        name="rglru_prompt",
    )(x, w_ag, cw, cb, wa, ba, wx, bx, c_lam)


def _s5_readout(u, xr_blocks, xi_blocks, cre_ref, cim_ref, d, wglu, bglu):
    ys = []
    for j in range(N_SSM_BLOCKS):
        ys.append(_dotw(xr_blocks[j], cre_ref[j]) - _dotw(xi_blocks[j], cim_ref[j]))
    y = jnp.concatenate(ys, axis=1) + d * u
    g = _gelu(y)
    return g * _sigmoid(_dotw(g, wglu) + bglu)


def _s5_prompt_kernel(x_ref, w_ref, bbr_ref, bbi_ref, ar_ref, ai_ref, cre_ref, cim_ref, d_ref,
                      wglu_ref, bglu_ref, yb_ref, sre_ref, sim_ref, sr_scr, si_scr, *, tb):
    step = pl.program_id(0)

    @pl.when(step == 0)
    def _():
        sr_scr[...] = jnp.zeros_like(sr_scr)
        si_scr[...] = jnp.zeros_like(si_scr)

    u = _dotw(x_ref[...], w_ref[...])
    row = lax.broadcasted_iota(jnp.int32, (tb, STATE_BLOCK), 0)
    xr_blocks, xi_blocks = [], []
    for j in range(N_SSM_BLOCKS):
        cols = slice(j * STATE_BLOCK, (j + 1) * STATE_BLOCK)
        uj = u[:, j * LANES:(j + 1) * LANES]
        br = _dotw(uj, bbr_ref[j])
        bi = _dotw(uj, bbi_ref[j])
        ar = ar_ref[:, cols]
        ai = ai_ref[:, cols]
        pr = sr_scr[0:1, cols]
        pi = si_scr[0:1, cols]
        br = br + jnp.where(row == 0, ar * pr - ai * pi, 0.0)
        bi = bi + jnp.where(row == 0, ar * pi + ai * pr, 0.0)
        xr, xi = _scan_rows_const(ar, ai, br, bi)
        sr_scr[:, cols] = jnp.broadcast_to(xr[tb - 1:tb, :], (8, STATE_BLOCK))
        si_scr[:, cols] = jnp.broadcast_to(xi[tb - 1:tb, :], (8, STATE_BLOCK))
        xr_blocks.append(xr)
        xi_blocks.append(xi)
    sre_ref[...] = sr_scr[...]
    sim_ref[...] = si_scr[...]
    yb_ref[...] = _s5_readout(u, xr_blocks, xi_blocks, cre_ref, cim_ref, d_ref[...],
                              wglu_ref[...], bglu_ref[...])


def _s5_prompt(x, w_b, bbr, bbi, a_re, a_im, cre, cim, d, wglu, bglu, tb):
    t = x.shape[0]
    full = lambda shape: pl.BlockSpec(shape, lambda i: (0,) * len(shape))
    return pl.pallas_call(
        functools.partial(_s5_prompt_kernel, tb=tb),
        grid=(t // tb,),
        in_specs=[pl.BlockSpec((tb, D_MODEL), lambda i: (i, 0)),
                  full(w_b.shape), full(bbr.shape), full(bbi.shape), full(a_re.shape), full(a_im.shape),
                  full(cre.shape), full(cim.shape), full(d.shape), full(wglu.shape), full(bglu.shape)],
        out_specs=[pl.BlockSpec((tb, D_SSM), lambda i: (i, 0)),
                   pl.BlockSpec((8, N_STATE), lambda i: (0, 0)),
                   pl.BlockSpec((8, N_STATE), lambda i: (0, 0))],
        out_shape=[jax.ShapeDtypeStruct((t, D_SSM), F32),
                   jax.ShapeDtypeStruct((8, N_STATE), F32),
                   jax.ShapeDtypeStruct((8, N_STATE), F32)],
        scratch_shapes=[pltpu.VMEM((8, N_STATE), F32), pltpu.VMEM((8, N_STATE), F32)],
        compiler_params=_cparams(),
        name="s5_prompt",
    )(x, w_b, bbr, bbi, a_re, a_im, cre, cim, d, wglu, bglu)


def _out_ln_kernel(*refs, n_in):
    a_refs = refs[:n_in]
    w_refs = refs[n_in:2 * n_in]
    x_ref, g_ref, b_ref, o_ref = refs[2 * n_in:]
    y = ALPHA * x_ref[...]
    for a_ref, w_ref in zip(a_refs, w_refs):
        y = y + _dotw(a_ref[...], w_ref[...])
    o_ref[...] = _layer_norm(y, g_ref[...], b_ref[...])


def _out_ln(acts, weights, x, g, b, tm, name):
    m = x.shape[0]
    n_in = len(acts)
    full = lambda shape: pl.BlockSpec(shape, lambda i: (0,) * len(shape))
    in_specs = ([pl.BlockSpec((tm, a.shape[1]), lambda i: (i, 0)) for a in acts]
                + [full(w.shape) for w in weights]
                + [pl.BlockSpec((tm, D_MODEL), lambda i: (i, 0)), full(g.shape), full(b.shape)])
    return pl.pallas_call(
        functools.partial(_out_ln_kernel, n_in=n_in),
        grid=(m // tm,),
        in_specs=in_specs,
        out_specs=pl.BlockSpec((tm, D_MODEL), lambda i: (i, 0)),
        out_shape=jax.ShapeDtypeStruct((m, D_MODEL), F32),
        compiler_params=_cparams(),
        name=name,
    )(*acts, *weights, x, g, b)


def _ffn_prompt_kernel(x_ref, wup_ref, wgate_ref, cw_ref, cb_ref, wdown_ref, g_ref, b_ref,
                       o_ref, tail_ref, h_scr, carry_scr, *, tm):
    step = pl.program_id(0)

    @pl.when(step == 0)
    def _():
        carry_scr[...] = jnp.zeros_like(carry_scr)

    x = x_ref[...]
    xb = x.astype(BF16)
    row = lax.broadcasted_iota(jnp.int32, (tm, FF_CHUNK), 0)
    for c in range(N_FF_CHUNKS):
        cols = slice(c * FF_CHUNK, (c + 1) * FF_CHUNK)
        up = jnp.dot(xb, wup_ref[:, cols], preferred_element_type=F32)
        gate = jnp.dot(xb, wgate_ref[:, cols], preferred_element_type=F32)
        prev1 = carry_scr[7:8, cols]
        prev2 = carry_scr[6:7, cols]
        up1 = jnp.where(row >= 1, pltpu.roll(up, 1, 0), prev1)
        up2 = jnp.where(row >= 2, pltpu.roll(up, 2, 0), jnp.where(row == 1, prev1, prev2))
        hc = (cb_ref[:, cols] + up2 * cw_ref[0:1, cols] + up1 * cw_ref[1:2, cols]
              + up * cw_ref[2:3, cols])
        h_scr[:, cols] = (_gelu(hc) * gate).astype(BF16)
        carry_scr[:, cols] = up[tm - 8:tm, :]
    tail_ref[...] = carry_scr[...]
    f = jnp.dot(h_scr[...], wdown_ref[...], preferred_element_type=F32)
    o_ref[...] = _layer_norm(ALPHA * x + f, g_ref[...], b_ref[...])


def _ffn_prompt(x, wup, wgate, cw, cb, wdown, g, b, tm):
    t = x.shape[0]
    const = lambda shape: pl.BlockSpec(shape, lambda i: (0,) * len(shape),
                                       pipeline_mode=pl.Buffered(1))
    return pl.pallas_call(
        functools.partial(_ffn_prompt_kernel, tm=tm),
        grid=(t // tm,),
        in_specs=[pl.BlockSpec((tm, D_MODEL), lambda i: (i, 0)),
                  const(wup.shape), const(wgate.shape), const(cw.shape), const(cb.shape),
                  const(wdown.shape), const(g.shape), const(b.shape)],
        out_specs=[pl.BlockSpec((tm, D_MODEL), lambda i: (i, 0)),
                   pl.BlockSpec((8, D_FF), lambda i: (0, 0))],
        out_shape=[jax.ShapeDtypeStruct((t, D_MODEL), F32),
                   jax.ShapeDtypeStruct((8, D_FF), F32)],
        scratch_shapes=[pltpu.VMEM((tm, D_FF), BF16), pltpu.VMEM((8, D_FF), F32)],
        compiler_params=_cparams(),
        name="ffn_prompt",
    )(x, wup, wgate, cw, cb, wdown, g, b)


def _aug_lhs(f3):
    h1, h2, h3 = (h.astype(F32) for h in _split3(f3))
    lane = lax.broadcasted_iota(jnp.int32, f3.shape, 1)
    pieces = jnp.where(lane < N_HEADS, h1, jnp.where(lane < 2 * N_HEADS, h2, jnp.where(
        lane < 3 * N_HEADS, h3, jnp.where(lane < 4 * N_HEADS, 1.0, 0.0))))
    return pieces.astype(BF16)


def _qkv_prompt_kernel(x_ref, w_ref, wf_ref, bf_ref, selq_ref, selk_ref,
                       k_ref, v_ref, lf_ref, qa_ref, ka_ref, va_ref, f_scr, *, tm):
    step = pl.program_id(0)

    @pl.when(step == 0)
    def _():
        f_scr[...] = jnp.zeros_like(f_scr)

    x = x_ref[...]
    qkv = _dotw(x, w_ref[...])
    k = qkv[:, D_MODEL:2 * D_MODEL]
    v = qkv[:, 2 * D_MODEL:]
    k_ref[...] = k
    v_ref[...] = v
    logf3 = _log_sigmoid(_dotw(x, wf_ref[...]) + bf_ref[...])
    lf_ref[...] = logf3

    r_i = lax.broadcasted_iota(jnp.int32, (tm, tm), 0)
    c_i = lax.broadcasted_iota(jnp.int32, (tm, tm), 1)
    tri = jnp.where(r_i >= c_i, 1.0, 0.0).astype(BF16)
    f3 = _dot_exact01(tri, logf3) + f_scr[0:1, :]
    f_scr[...] = jnp.broadcast_to(f3[tm - 1:tm, :], f_scr.shape)

    lhs = _aug_lhs(f3)
    augq = jnp.dot(lhs, selq_ref[...], preferred_element_type=F32)
    augk = jnp.dot(lhs, selk_ref[...], preferred_element_type=F32)
    lane = lax.broadcasted_iota(jnp.int32, (tm, LANES), 1)
    low = lane < HEAD_DIM
    vtail = jnp.where(lane == HEAD_DIM, 1.0, 0.0)
    scale = HEAD_DIM ** -0.5
    for hp in range(N_HEADS // 2):
        cols = slice(hp * LANES, (hp + 1) * LANES)
        qb = qkv[:, cols] * scale
        kb = k[:, cols]
        vb = v[:, cols]
        for odd in range(2):
            h = 2 * hp + odd
            acols = slice(h * LANES, (h + 1) * LANES)
            if odd:
                qb, kb, vb = (pltpu.roll(qb, HEAD_DIM, 1), pltpu.roll(kb, HEAD_DIM, 1),
                              pltpu.roll(vb, HEAD_DIM, 1))
            qa_ref[h] = jnp.where(low, qb, augq[:, acols]).astype(BF16)
            ka_ref[h] = jnp.where(low, kb, augk[:, acols]).astype(BF16)
            va_ref[h] = jnp.where(low, vb, vtail).astype(BF16)


def _qkv_prompt(x, w_qkv, wf3, bf3, selq, selk, tm):
    t = x.shape[0]
    full = lambda shape: pl.BlockSpec(shape, lambda i: (0,) * len(shape))
    head_major = jax.ShapeDtypeStruct((N_HEADS, t, LANES), BF16)
    head_spec = pl.BlockSpec((N_HEADS, tm, LANES), lambda i: (0, i, 0))
    return pl.pallas_call(
        functools.partial(_qkv_prompt_kernel, tm=tm),
        grid=(t // tm,),
        in_specs=[pl.BlockSpec((tm, D_MODEL), lambda i: (i, 0)),
                  full(w_qkv.shape), full(wf3.shape), full(bf3.shape), full(selq.shape),
                  full(selk.shape)],
        out_specs=[pl.BlockSpec((tm, D_MODEL), lambda i: (i, 0)),
                   pl.BlockSpec((tm, D_MODEL), lambda i: (i, 0)),
                   pl.BlockSpec((tm, LANES), lambda i: (i, 0)),
                   head_spec, head_spec, head_spec],
        out_shape=[jax.ShapeDtypeStruct((t, D_MODEL), F32),
                   jax.ShapeDtypeStruct((t, D_MODEL), F32),
                   jax.ShapeDtypeStruct((t, LANES), F32),
                   head_major, head_major, head_major],
        scratch_shapes=[pltpu.VMEM((8, LANES), F32)],
        compiler_params=_cparams(),
        name="qkv_prompt",
    )(x, w_qkv, wf3, bf3, selq, selk)


def _flash_kernel(it_ref, jt_ref, q_ref, k_ref, v_ref, o_ref, m_scr, acc_scr, *, tq, tk):
    s_idx = pl.program_id(1)
    i = it_ref[s_idx]
    j = jt_ref[s_idx]

    @pl.when(j == 0)
    def _():
        m_scr[...] = jnp.full(m_scr.shape, NEG_BIG, F32)
        acc_scr[...] = jnp.zeros_like(acc_scr)

    qpos = i * tq + lax.broadcasted_iota(jnp.int32, (tq, tk), 0)
    kpos = j * tk + lax.broadcasted_iota(jnp.int32, (tq, tk), 1)
    causal = kpos <= qpos
    for hh in range(2):
        s = lax.dot_general(q_ref[hh], k_ref[hh], (((1,), (1,)), ((), ())),
                            preferred_element_type=F32)
        s = jnp.where(causal, s, NEG_BIG)
        m_prev = m_scr[hh]
        m_next = jnp.maximum(m_prev, jnp.max(s, axis=1, keepdims=True))
        alpha = jnp.exp(m_prev - m_next)
        p = jnp.exp(s - m_next[:, 0:1])
        acc_scr[hh] = alpha * acc_scr[hh] + jnp.dot(p.astype(BF16), v_ref[hh],
                                                    preferred_element_type=F32)
        m_scr[hh] = m_next

    @pl.when(j == ((i + 1) * tq - 1) // tk)
    def _():
        lane = lax.broadcasted_iota(jnp.int32, (tq, LANES), 1)
        outs = []
        for hh in range(2):
            acc = acc_scr[hh]
            outs.append(acc / acc[:, HEAD_DIM:HEAD_DIM + 1])
        o_ref[...] = jnp.where(lane < HEAD_DIM, outs[0], pltpu.roll(outs[1], HEAD_DIM, 1))


def _flash(qa, ka, va, tq, tk):
    t = qa.shape[1]
    nq = t // tq
    i_list, j_list = [], []
    for i in range(nq):
        for j in range(((i + 1) * tq - 1) // tk + 1):
            i_list.append(i)
            j_list.append(j)
    it = jnp.asarray(i_list, jnp.int32)
    jt = jnp.asarray(j_list, jnp.int32)
    grid_spec = pltpu.PrefetchScalarGridSpec(
        num_scalar_prefetch=2,
        grid=(N_HEADS // 2, len(i_list)),
        in_specs=[pl.BlockSpec((2, tq, LANES), lambda hp, s, it, jt: (hp, it[s], 0)),
                  pl.BlockSpec((2, tk, LANES), lambda hp, s, it, jt: (hp, jt[s], 0)),
                  pl.BlockSpec((2, tk, LANES), lambda hp, s, it, jt: (hp, jt[s], 0))],
        out_specs=pl.BlockSpec((tq, LANES), lambda hp, s, it, jt: (it[s], hp)),
        scratch_shapes=[pltpu.VMEM((2, tq, LANES), F32), pltpu.VMEM((2, tq, LANES), F32)],
    )
    return pl.pallas_call(
        functools.partial(_flash_kernel, tq=tq, tk=tk),
        grid_spec=grid_spec,
        out_shape=jax.ShapeDtypeStruct((t, D_MODEL), F32),
        compiler_params=_cparams(2),
        name="flash_prompt",
    )(it, jt, qa, ka, va)


def _mix0_sample_kernel(x_ref, w_ref, c0_ref, c1_ref, c2_ref, cw_ref, cb_ref, wa_ref, ba_ref,
                        wx_ref, bx_ref, cl_ref, h0_ref, sr_ref, si_ref, bbr_ref, bbi_ref,
                        ar_ref, ai_ref, cre_ref, cim_ref, d_ref, wglu_ref, bglu_ref,
                        ya_ref, yb_ref, xa_ref, h_ref, xr_ref, xi_ref):
    proj = _dotw(x_ref[...], w_ref[...])
    xa = proj[:, :D_RNN]
    ga = proj[:, D_RNN:2 * D_RNN]
    u = proj[:, 2 * D_RNN:]
    xa_ref[...] = xa
    xc = (cb_ref[...] + c0_ref[...] * cw_ref[0:1, :] + c1_ref[...] * cw_ref[1:2, :]
          + c2_ref[...] * cw_ref[2:3, :] + xa * cw_ref[3:4, :])
    a, inp = _rglru_gates(xc, wa_ref[...], ba_ref[...], wx_ref[...], bx_ref[...], cl_ref[...])
    h = a * h0_ref[...] + inp
    h_ref[...] = h
    ya_ref[...] = h * _gelu(ga)

    xr_blocks, xi_blocks = [], []
    for j in range(N_SSM_BLOCKS):
        cols = slice(j * STATE_BLOCK, (j + 1) * STATE_BLOCK)
        uj = u[:, j * LANES:(j + 1) * LANES]
        ar = ar_ref[:, cols]
        ai = ai_ref[:, cols]
        pr = sr_ref[:, cols]
        pi = si_ref[:, cols]
        xr = _dotw(uj, bbr_ref[j]) + (ar * pr - ai * pi)
        xi = _dotw(uj, bbi_ref[j]) + (ar * pi + ai * pr)
        xr_ref[:, cols] = xr
        xi_ref[:, cols] = xi
        xr_blocks.append(xr)
        xi_blocks.append(xi)
    yb_ref[...] = _s5_readout(u, xr_blocks, xi_blocks, cre_ref, cim_ref, d_ref[...],
                              wglu_ref[...], bglu_ref[...])


def _mix0_sample(*args):
    n = args[0].shape[0]
    shapes = [(n, D_RNN), (n, D_SSM), (n, D_RNN), (n, D_RNN), (n, N_STATE), (n, N_STATE)]
    return pl.pallas_call(
        _mix0_sample_kernel,
        out_shape=[jax.ShapeDtypeStruct(s, F32) for s in shapes],
        compiler_params=pltpu.CompilerParams(vmem_limit_bytes=VMEM_LIMIT),
        name="mix0_sample",
    )(*args)


def _ffn_sample_kernel(x_ref, wup_ref, wgate_ref, b0_ref, b1_ref, cw_ref, cb_ref, wdown_ref,
                       g_ref, b_ref, o_ref, up_ref, acc_scr):
    c = pl.program_id(0)

    @pl.when(c == 0)
    def _():
        acc_scr[...] = jnp.zeros_like(acc_scr)

    x = x_ref[...]
    up = _dotw(x, wup_ref[...])
    gate = _dotw(x, wgate_ref[...])
    up_ref[...] = up
    hc = (cb_ref[...] + b0_ref[...] * cw_ref[0:1, :] + b1_ref[...] * cw_ref[1:2, :]
          + up * cw_ref[2:3, :])
    acc_scr[...] += _dotw(_gelu(hc) * gate, wdown_ref[...])

    @pl.when(c == pl.num_programs(0) - 1)
    def _():
        o_ref[...] = _layer_norm(ALPHA * x + acc_scr[...], g_ref[...], b_ref[...])


def _ffn_sample(x, wup, wgate, buf0, buf1, cw, cb, wdown, g, b):
    n = x.shape[0]
    fc = FF_CHUNK
    col = lambda rows: pl.BlockSpec((rows, fc), lambda c: (0, c))
    full = lambda shape: pl.BlockSpec(shape, lambda c: (0,) * len(shape))
    return pl.pallas_call(
        _ffn_sample_kernel,
        grid=(D_FF // fc,),
        in_specs=[full(x.shape), col(D_MODEL), col(D_MODEL), col(n), col(n), col(cw.shape[0]),
                  col(1), pl.BlockSpec((fc, D_MODEL), lambda c: (c, 0)), full(g.shape),
                  full(b.shape)],
        out_specs=[full((n, D_MODEL)), col(n)],
        out_shape=[jax.ShapeDtypeStruct((n, D_MODEL), F32), jax.ShapeDtypeStruct((n, D_FF), F32)],
        scratch_shapes=[pltpu.VMEM((n, D_MODEL), F32)],
        compiler_params=_cparams(),
        name="ffn_sample",
    )(x, wup, wgate, buf0, buf1, cw, cb, wdown, g, b)


def _qkv_sample_kernel(x_ref, w_ref, wf_ref, bf_ref, o_ref, lf_ref):
    x = x_ref[...]
    o_ref[...] = _dotw(x, w_ref[...])

    @pl.when(pl.program_id(0) == 0)
    def _():
        lf_ref[...] = _log_sigmoid(_dotw(x, wf_ref[...]) + bf_ref[...])


def _qkv_sample(x, w_in1, wf, bf):
    n = x.shape[0]
    full = lambda shape: pl.BlockSpec(shape, lambda c: (0,) * len(shape))
    return pl.pallas_call(
        _qkv_sample_kernel,
        grid=(3,),
        in_specs=[full(x.shape), pl.BlockSpec((D_MODEL, D_MODEL), lambda c: (0, c)),
                  full(wf.shape), full(bf.shape)],
        out_specs=[pl.BlockSpec((n, D_MODEL), lambda c: (0, c)), full((n, LANES))],
        out_shape=[jax.ShapeDtypeStruct((n, 3 * D_MODEL), F32),
                   jax.ShapeDtypeStruct((n, LANES), F32)],
        compiler_params=_cparams(),
        name="qkv_sample",
    )(x, w_in1, wf, bf)


def _attn_sample_kernel(pt_ref, q_ref, kn_ref, vn_ref, gt_ref, *refs, pages_per_step):
    pp = pages_per_step
    k_refs = refs[:pp]
    v_refs = refs[pp:2 * pp]
    lf_refs = refs[2 * pp:3 * pp]
    o_ref, m_scr, l_scr, acc_scr, suf_scr, pad_scr = refs[3 * pp:]
    s_idx = pl.program_id(1)

    @pl.when(s_idx == 0)
    def _():
        m_scr[...] = jnp.full(m_scr.shape, NEG_BIG, F32)
        l_scr[...] = jnp.zeros_like(l_scr)
        acc_scr[...] = jnp.zeros_like(acc_scr)
        suf_scr[...] = jnp.zeros_like(suf_scr)
        pad_scr[...] = jnp.zeros_like(pad_scr)

    hrow = lax.broadcasted_iota(jnp.int32, (N_HEADS, D_MODEL), 0)
    hcol = lax.broadcasted_iota(jnp.int32, (N_HEADS, D_MODEL), 1) // HEAD_DIM
    diag = hrow == hcol
    scale = HEAD_DIM ** -0.5
    qbd = jnp.where(diag, q_ref[0] * scale, 0.0)
    qbd_b = qbd.astype(BF16)
    r_i = lax.broadcasted_iota(jnp.int32, (PAGE_SIZE, PAGE_SIZE), 0)
    c_i = lax.broadcasted_iota(jnp.int32, (PAGE_SIZE, PAGE_SIZE), 1)
    later = jnp.where(c_i > r_i, 1.0, 0.0).astype(BF16)
    gt = gt_ref[0]

    for r in range(pp):
        kb = k_refs[r][0].astype(BF16)
        vb = v_refs[r][0].astype(BF16)
        st = lax.dot_general(qbd_b, kb, (((1,), (1,)), ((), ())),
                             preferred_element_type=F32)
        pad_scr[:, 0:N_HEADS] = lf_refs[r][0]
        lf = pad_scr[...]
        suf_tok = _dot_exact01(later, lf)
        suf_t = suf_tok.T[0:N_HEADS, :]
        lf_t = lf.T[0:N_HEADS, :]
        carry = suf_scr[:, 0:1]
        logit = st + suf_t + (carry + gt)
        m_prev = m_scr[:, 0:1]
        m_next = jnp.maximum(m_prev, jnp.max(logit, axis=1, keepdims=True))
        alpha = jnp.exp(m_prev - m_next)
        p = jnp.exp(logit - m_next)
        l_scr[...] = jnp.broadcast_to(alpha * l_scr[:, 0:1] + jnp.sum(p, axis=1, keepdims=True),
                                      l_scr.shape)
        acc_scr[...] = alpha * acc_scr[...] + jnp.dot(p.astype(BF16), vb,
                                                      preferred_element_type=F32)
        m_scr[...] = jnp.broadcast_to(m_next, m_scr.shape)
        suf_scr[...] = jnp.broadcast_to(carry + jnp.sum(lf_t, axis=1, keepdims=True),
                                        suf_scr.shape)

    @pl.when(s_idx == pl.num_programs(1) - 1)
    def _():
        s_self = jnp.sum(qbd * kn_ref[0], axis=1, keepdims=True)
        m_prev = m_scr[:, 0:1]
        m_fin = jnp.maximum(m_prev, s_self)
        a_past = jnp.exp(m_prev - m_fin)
        a_self = jnp.exp(s_self - m_fin)
        l_fin = a_past * l_scr[:, 0:1] + a_self
        acc = (a_past * acc_scr[...] + a_self * vn_ref[0]) / l_fin
        o_ref[0] = jnp.sum(jnp.where(diag, acc, 0.0), axis=0, keepdims=True)


def _attn_sample(page_table, q, k_new, v_new, gt, cache_k, cache_v, cache_logf, pages_per_step):
    n_seq, n_pages = page_table.shape
    pp = pages_per_step
    n_pool = cache_k.shape[0]
    ck = cache_k.reshape(n_pool, PAGE_SIZE, D_MODEL)
    cv = cache_v.reshape(n_pool, PAGE_SIZE, D_MODEL)
    pt = page_table.reshape(-1)

    def page_map(r):
        return lambda b, s, pt: (pt[b * n_pages + (n_pages - 1 - (s * pp + r))], 0, 0)

    row = lambda: pl.BlockSpec((1, 1, D_MODEL), lambda b, s, pt: (b, 0, 0))
    in_specs = [row(), row(), row(), pl.BlockSpec((1, N_HEADS, 1), lambda b, s, pt: (b, 0, 0))]
    in_specs += [pl.BlockSpec((1, PAGE_SIZE, D_MODEL), page_map(r)) for r in range(pp)]
    in_specs += [pl.BlockSpec((1, PAGE_SIZE, D_MODEL), page_map(r)) for r in range(pp)]
    in_specs += [pl.BlockSpec((1, PAGE_SIZE, N_HEADS), page_map(r)) for r in range(pp)]
    grid_spec = pltpu.PrefetchScalarGridSpec(
        num_scalar_prefetch=1,
        grid=(n_seq, n_pages // pp),
        in_specs=in_specs,
        out_specs=pl.BlockSpec((1, 1, D_MODEL), lambda b, s, pt: (b, 0, 0)),
        scratch_shapes=[pltpu.VMEM((N_HEADS, LANES), F32), pltpu.VMEM((N_HEADS, LANES), F32),
                        pltpu.VMEM((N_HEADS, D_MODEL), F32), pltpu.VMEM((N_HEADS, LANES), F32),
                        pltpu.VMEM((PAGE_SIZE, LANES), F32)],
    )
    out = pl.pallas_call(
        functools.partial(_attn_sample_kernel, pages_per_step=pp),
        grid_spec=grid_spec,
        out_shape=jax.ShapeDtypeStruct((n_seq, 1, D_MODEL), F32),
        compiler_params=_cparams(2),
        name="attn_sample",
    )(pt, q.reshape(n_seq, 1, D_MODEL), k_new.reshape(n_seq, 1, D_MODEL),
      v_new.reshape(n_seq, 1, D_MODEL), gt.reshape(n_seq, N_HEADS, 1),
      *([ck] * pp), *([cv] * pp), *([cache_logf] * pp))
    return out.reshape(n_seq, D_MODEL)


def _block_diag_dense(w):
    nb, n, _ = w.shape
    eye = jnp.eye(nb, dtype=w.dtype)
    return jnp.einsum('hij,hk->hikj', w, eye).reshape(nb * n, nb * n)


def _s5_discretize(lam_re, lam_im, log_dt, b_re, b_im):
    lr = jnp.minimum(lam_re, -1e-4)
    li = lam_im
    dt = jnp.exp(log_dt)[:, None]
    mag = jnp.exp(lr * dt)
    ab_re = mag * jnp.cos(li * dt)
    ab_im = mag * jnp.sin(li * dt)
    den = lr * lr + li * li
    nr = ab_re - 1.0
    zr = (nr * lr + ab_im * li) / den
    zi = (ab_im * lr - nr * li) / den
    bb_re = zr[..., None] * b_re - zi[..., None] * b_im
    bb_im = zr[..., None] * b_im + zi[..., None] * b_re
    return ab_re, ab_im, bb_re, bb_im


def _s5_in_blocks(bb):
    eye = jnp.eye(GROUPS_PER_BLOCK, dtype=bb.dtype)
    b4 = bb.transpose(0, 2, 1).reshape(N_SSM_BLOCKS, GROUPS_PER_BLOCK, S5_GROUP, S5_STATE)
    return jnp.einsum('jgmp,gh->jgmhp', b4, eye).reshape(N_SSM_BLOCKS, LANES, STATE_BLOCK)


def _s5_out_blocks(c):
    eye = jnp.eye(GROUPS_PER_BLOCK, dtype=c.dtype)
    c4 = c.reshape(N_SSM_BLOCKS, GROUPS_PER_BLOCK, S5_GROUP, S5_STATE)
    return jnp.einsum('jgmp,gh->jgphm', c4, eye).reshape(N_SSM_BLOCKS, STATE_BLOCK, LANES)


def _aug_selectors():
    rows = jnp.arange(LANES)[:, None]
    cols = jnp.arange(N_HEADS * LANES)[None, :]
    head = cols // LANES
    c = cols % LANES - HEAD_DIM
    piece = rows // N_HEADS
    rhead = rows % N_HEADS
    same = (rhead == head) & (rows < 4 * N_HEADS)
    f_piece = same & (piece < 3)
    ones_piece = same & (piece == 3)
    selq = jnp.where(f_piece & (c == piece), 1.0, 0.0) + jnp.where(
        ones_piece & (c >= 3) & (c < AUG_COLS), 1.0, 0.0)
    selk = jnp.where(ones_piece & (c >= 0) & (c < 3), 1.0, 0.0) - jnp.where(
        f_piece & (c == piece + 3), 1.0, 0.0)
    return selq.astype(BF16), selk.astype(BF16)


def _forget_cols(w_in1, b_fgt, dtype):
    wf = w_in1[:, 3 * D_MODEL:]
    pad = jnp.zeros((D_MODEL, LANES - 3 * N_HEADS), w_in1.dtype)
    wf3 = jnp.concatenate([wf, wf, wf, pad], axis=1).astype(dtype)
    bf3 = jnp.concatenate([b_fgt, b_fgt, b_fgt, jnp.zeros((LANES - 3 * N_HEADS,), F32)])[None, :]
    return wf3, bf3


def _row(v):
    return v.reshape(1, -1)


def _pick(t, pref):
    return pref if t % pref == 0 else t


def _prompt_path(x, p, s5p, tiles):
    t = x.shape[0]
    bf = lambda w: w.astype(BF16)
    tb = _pick(t, tiles['scan'])
    ya, tail, h_last = _rglru_prompt(
        x, bf(p['w_in0'][:, :2 * D_RNN]), p['rg_conv_w'], _row(p['rg_conv_b']),
        bf(s5p['wa_bd']), _row(p['rg_ba']), bf(s5p['wx_bd']), _row(p['rg_bx']), s5p['c_lam'], tb)
    yb, s_re, s_im = _s5_prompt(
        x, bf(p['w_in0'][:, 2 * D_RNN:]), bf(s5p['bbr']), bf(s5p['bbi']), s5p['a_re'], s5p['a_im'],
        bf(s5p['cre']), bf(s5p['cim']), _row(p['s5_d']), bf(p['s5_w_glu']), _row(p['s5_b_glu']), tb)
    tm = _pick(t, tiles['rows'])
    x1 = _out_ln([ya, yb], [bf(p['w_out0'][:D_RNN]), bf(p['w_out0'][D_RNN:])], x,
                 p['ln_g'][0, 0][None], p['ln_b'][0, 0][None], tm, "out0_prompt")
    x2, ffn_tail0 = _ffn_prompt(x1, bf(p['ffn_w_up'][0]), bf(p['ffn_w_gate'][0]), p['ffn_conv_w'][0],
                                p['ffn_conv_b'][0][None], bf(p['ffn_w_down'][0]),
                                p['ln_g'][0, 1][None], p['ln_b'][0, 1][None], tm)
    wf3, bf3 = _forget_cols(p['w_in1'], p['b_fgt'], BF16)
    selq, selk = _aug_selectors()
    k, v, lf, qa, ka, va = _qkv_prompt(x2, bf(p['w_in1'][:, :3 * D_MODEL]), wf3, bf3, selq, selk,
                                       _pick(t, tiles['qkv']))
    o = _flash(qa, ka, va, _pick(t, tiles['tq']), _pick(t, tiles['tk']))
    x3 = _out_ln([o], [bf(p['w_out1'])], x2, p['ln_g'][1, 0][None], p['ln_b'][1, 0][None], tm,
                 "out1_prompt")
    x4, ffn_tail1 = _ffn_prompt(x3, bf(p['ffn_w_up'][1]), bf(p['ffn_w_gate'][1]), p['ffn_conv_w'][1],
                                p['ffn_conv_b'][1][None], bf(p['ffn_w_down'][1]),
                                p['ln_g'][1, 1][None], p['ln_b'][1, 1][None], tm)
    return (x4[None], tail[None, 5:8], h_last[0:1], s_re[0:1].reshape(1, S5_GROUPS, S5_STATE),
            s_im[0:1].reshape(1, S5_GROUPS, S5_STATE), k.reshape(1, t, N_HEADS, HEAD_DIM),
            v.reshape(1, t, N_HEADS, HEAD_DIM), lf[None, :, :N_HEADS],
            jnp.stack([ffn_tail0[None, 6:8], ffn_tail1[None, 6:8]], axis=0))


def _sample_path(x, rg_conv, rg_h, s5_re, s5_im, cache_k, cache_v, cache_logf, ffn_conv,
                 page_table, p, s5p, pages_per_step):
    n = x.shape[0]
    ya, yb, xa, h_new, xr, xi = _mix0_sample(
        x, p['w_in0'], rg_conv[:, 0], rg_conv[:, 1], rg_conv[:, 2], p['rg_conv_w'],
        _row(p['rg_conv_b']), s5p['wa_bd'], _row(p['rg_ba']), s5p['wx_bd'], _row(p['rg_bx']),
        s5p['c_lam'], rg_h, s5_re.reshape(n, N_STATE), s5_im.reshape(n, N_STATE),
        s5p['bbr'], s5p['bbi'], s5p['a_re'], s5p['a_im'], s5p['cre'], s5p['cim'],
        _row(p['s5_d']), p['s5_w_glu'], _row(p['s5_b_glu']))
    x1 = _out_ln([ya, yb], [p['w_out0'][:D_RNN], p['w_out0'][D_RNN:]], x,
                 p['ln_g'][0, 0][None], p['ln_b'][0, 0][None], n, "out0_sample")
    x2, up0 = _ffn_sample(x1, p['ffn_w_up'][0], p['ffn_w_gate'][0], ffn_conv[0, :, 0],
                          ffn_conv[0, :, 1], p['ffn_conv_w'][0], p['ffn_conv_b'][0][None],
                          p['ffn_w_down'][0], p['ln_g'][0, 1][None], p['ln_b'][0, 1][None])
    wf3, bf3 = _forget_cols(p['w_in1'], p['b_fgt'], F32)
    qkv, lf3 = _qkv_sample(x2, p['w_in1'], wf3, bf3)
    q = qkv[:, :D_MODEL]
    k = qkv[:, D_MODEL:2 * D_MODEL]
    v = qkv[:, 2 * D_MODEL:]
    logf = lf3[:, :N_HEADS]
    o = _attn_sample(page_table, q, k, v, logf, cache_k, cache_v, cache_logf, pages_per_step)
    x3 = _out_ln([o], [p['w_out1']], x2, p['ln_g'][1, 0][None], p['ln_b'][1, 0][None], n,
                 "out1_sample")
    x4, up1 = _ffn_sample(x3, p['ffn_w_up'][1], p['ffn_w_gate'][1], ffn_conv[1, :, 0],
                          ffn_conv[1, :, 1], p['ffn_conv_w'][1], p['ffn_conv_b'][1][None],
                          p['ffn_w_down'][1], p['ln_g'][1, 1][None], p['ln_b'][1, 1][None])
    new_rg_conv = jnp.concatenate([rg_conv[:, 1:], xa[:, None]], axis=1)
    new_ffn = jnp.stack([jnp.stack([ffn_conv[0, :, 1], up0], axis=1),
                         jnp.stack([ffn_conv[1, :, 1], up1], axis=1)], axis=0)
    return (x4[:, None], new_rg_conv, h_new, xr.reshape(n, S5_GROUPS, S5_STATE),
            xi.reshape(n, S5_GROUPS, S5_STATE), k.reshape(n, 1, N_HEADS, HEAD_DIM),
            v.reshape(n, 1, N_HEADS, HEAD_DIM), logf[:, None], new_ffn)


def _derived_params(p):
    a_re, a_im, bb_re, bb_im = _s5_discretize(p['s5_lambda_re'], p['s5_lambda_im'], p['s5_log_dt'],
                                              p['s5_b_re'], p['s5_b_im'])
    return dict(
        wa_bd=_block_diag_dense(p['rg_wa']), wx_bd=_block_diag_dense(p['rg_wx']),
        c_lam=_row(-RG_C * jax.nn.softplus(-p['rg_lambda'])),
        a_re=a_re.reshape(1, N_STATE), a_im=a_im.reshape(1, N_STATE),
        bbr=_s5_in_blocks(bb_re), bbi=_s5_in_blocks(bb_im),
        cre=_s5_out_blocks(p['s5_c_re']), cim=_s5_out_blocks(p['s5_c_im']))


PROMPT_TILES = dict(scan=256, rows=512, qkv=256, tq=1024, tk=1024)
PAGES_PER_STEP = 4


def kernel(x_prompt, x_sample, state_rglru_conv, state_rglru_h, state_s5_re, state_s5_im, cache_k, cache_v, cache_logf, state_ffn_conv, page_table, w_in0, rg_conv_w, rg_conv_b, rg_wa, rg_ba, rg_wx, rg_bx, rg_lambda, s5_lambda_re, s5_lambda_im, s5_log_dt, s5_b_re, s5_b_im, s5_c_re, s5_c_im, s5_d, s5_w_glu, s5_b_glu, w_out0, w_in1, b_fgt, w_out1, ffn_w_up, ffn_w_gate, ffn_conv_w, ffn_conv_b, ffn_w_down, ln_g, ln_b):
    p = dict(w_in0=w_in0, rg_conv_w=rg_conv_w, rg_conv_b=rg_conv_b, rg_wa=rg_wa, rg_ba=rg_ba,
             rg_wx=rg_wx, rg_bx=rg_bx, rg_lambda=rg_lambda, s5_lambda_re=s5_lambda_re,
             s5_lambda_im=s5_lambda_im, s5_log_dt=s5_log_dt, s5_b_re=s5_b_re, s5_b_im=s5_b_im,
             s5_c_re=s5_c_re, s5_c_im=s5_c_im, s5_d=s5_d, s5_w_glu=s5_w_glu, s5_b_glu=s5_b_glu,
             w_out0=w_out0, w_in1=w_in1, b_fgt=b_fgt, w_out1=w_out1, ffn_w_up=ffn_w_up,
             ffn_w_gate=ffn_w_gate, ffn_conv_w=ffn_conv_w, ffn_conv_b=ffn_conv_b,
             ffn_w_down=ffn_w_down, ln_g=ln_g, ln_b=ln_b)
    s5p = _derived_params(p)
    prompt = _prompt_path(x_prompt[0], p, s5p, PROMPT_TILES)
    sample = _sample_path(x_sample[:, 0], state_rglru_conv, state_rglru_h, state_s5_re, state_s5_im,
                          cache_k, cache_v, cache_logf, state_ffn_conv, page_table, p, s5p,
                          PAGES_PER_STEP)
    return (prompt[0], sample[0]) + tuple(prompt[1:]) + tuple(sample[1:])
```

```python
import functools

import jax
import jax.numpy as jnp
from jax import lax
from jax.experimental import pallas as pl
from jax.experimental.pallas import tpu as pltpu

F32 = jnp.float32
BF16 = jnp.bfloat16

D_MODEL = 1024
DEPTH = 2
PAGE_SIZE = 128
D_RNN = 512
RG_BLOCKS = 8
RG_BW = D_RNN // RG_BLOCKS
RG_C = 8.0
D_SSM = 512
S5_GROUP = 16
S5_GROUPS = D_SSM // S5_GROUP
S5_STATE = 64
N_STATE = S5_GROUPS * S5_STATE
N_HEADS = 16
HEAD_DIM = D_MODEL // N_HEADS
D_FF = 2816
ALPHA = (2.0 * DEPTH) ** 0.25
LN_EPS = 1e-5

LANES = 128
GROUPS_PER_BLOCK = LANES // S5_GROUP
N_SSM_BLOCKS = D_SSM // LANES
STATE_BLOCK = GROUPS_PER_BLOCK * S5_STATE
FF_CHUNK = 256
N_FF_CHUNKS = D_FF // FF_CHUNK
NEG_BIG = -1e30
LOG2E = 1.4426950408889634
AUG_COLS = 6
VMEM_LIMIT = 56 * 1024 * 1024


def _cparams(n_axes=1, vmem=VMEM_LIMIT):
    return pltpu.CompilerParams(dimension_semantics=("arbitrary",) * n_axes,
                                vmem_limit_bytes=vmem)


def _split3(x):
    h1 = x.astype(BF16)
    r1 = x - h1.astype(F32)
    h2 = r1.astype(BF16)
    h3 = (r1 - h2.astype(F32)).astype(BF16)
    return h1, h2, h3


def _dotw(x, w):
    if w.dtype == BF16:
        return jnp.dot(x.astype(BF16), w, preferred_element_type=F32)
    xh = x.astype(BF16)
    xl = (x - xh.astype(F32)).astype(BF16)
    wh = w.astype(BF16)
    wl = (w - wh.astype(F32)).astype(BF16)
    return (jnp.dot(xh, wh, preferred_element_type=F32)
            + jnp.dot(xl, wh, preferred_element_type=F32)
            + jnp.dot(xh, wl, preferred_element_type=F32))


def _dot_exact01(m01, x):
    h1, h2, h3 = _split3(x)
    return (jnp.dot(m01, h1, preferred_element_type=F32)
            + jnp.dot(m01, h2, preferred_element_type=F32)
            + jnp.dot(m01, h3, preferred_element_type=F32))


def _gelu(x):
    return jax.nn.gelu(x)


def _sigmoid(x):
    return jax.nn.sigmoid(x)


def _log_sigmoid(x):
    return jnp.minimum(x, 0.0) - jnp.log1p(jnp.exp(-jnp.abs(x)))


def _expm1(x):
    u = jnp.exp(x)
    um1 = u - 1.0
    return jnp.where(um1 == 0.0, x, jnp.where(um1 == -1.0, -1.0, um1 * x / jnp.log(u)))


def _layer_norm(xf, g, b):
    mean = jnp.mean(xf, axis=-1, keepdims=True)
    xc = xf - mean
    var = jnp.mean(xc * xc, axis=-1, keepdims=True)
    return xc * lax.rsqrt(var + LN_EPS) * g + b


def _rglru_gates(xc, wa, ba, wx, bx, c_lam):
    r = _sigmoid(_dotw(xc, wa) + ba)
    i = _sigmoid(_dotw(xc, wx) + bx)
    log_a = c_lam * r
    a = jnp.exp(log_a)
    inp = jnp.sqrt(-_expm1(2.0 * log_a)) * (i * xc)
    return a, inp


def _scan_rows(a, b):
    n = a.shape[0]
    row = lax.broadcasted_iota(jnp.int32, a.shape, 0)
    d = 1
    while d < n:
        valid = row >= d
        b = b + jnp.where(valid, a * pltpu.roll(b, d, 0), 0.0)
        if 2 * d < n:
            a = jnp.where(valid, a * pltpu.roll(a, d, 0), a)
        d *= 2
    return b


def _scan_rows_const(ar, ai, br, bi):
    n = br.shape[0]
    row = lax.broadcasted_iota(jnp.int32, br.shape, 0)
    d = 1
    while d < n:
        valid = row >= d
        sr = pltpu.roll(br, d, 0)
        si = pltpu.roll(bi, d, 0)
        br, bi = (br + jnp.where(valid, ar * sr - ai * si, 0.0),
                  bi + jnp.where(valid, ar * si + ai * sr, 0.0))
        ar, ai = ar * ar - ai * ai, 2.0 * ar * ai
        d *= 2
    return br, bi


def _rglru_prompt_kernel(x_ref, w_ref, cw_ref, cb_ref, wa_ref, ba_ref, wx_ref, bx_ref, cl_ref,
                         ya_ref, tail_ref, h_ref, ext_scr, h_scr, *, tb):
    step = pl.program_id(0)

    @pl.when(step == 0)
    def _():
        ext_scr[0:8, :] = jnp.zeros((8, D_RNN), F32)
        h_scr[...] = jnp.zeros_like(h_scr)

    proj = _dotw(x_ref[...], w_ref[...])
    xa = proj[:, :D_RNN]
    ga = proj[:, D_RNN:]
    ext_scr[8:8 + tb, :] = xa
    xc = cb_ref[...] + xa * cw_ref[3:4, :]
    for j in range(3):
        xc = xc + ext_scr[pl.ds(5 + j, tb), :] * cw_ref[j:j + 1, :]
    tail = ext_scr[tb:tb + 8, :]
    ext_scr[0:8, :] = tail
    tail_ref[...] = tail

    a, inp = _rglru_gates(xc, wa_ref[...], ba_ref[...], wx_ref[...], bx_ref[...], cl_ref[...])
    row = lax.broadcasted_iota(jnp.int32, a.shape, 0)
    inp = inp + jnp.where(row == 0, a * h_scr[0:1, :], 0.0)
    h = _scan_rows(a, inp)
    h_last = jnp.broadcast_to(h[tb - 1:tb, :], (8, D_RNN))
    h_scr[...] = h_last
    h_ref[...] = h_last
    ya_ref[...] = h * _gelu(ga)


def _rglru_prompt(x, w_ag, cw, cb, wa, ba, wx, bx, c_lam, tb):
    t = x.shape[0]
    full = lambda shape: pl.BlockSpec(shape, lambda i: (0,) * len(shape))
    return pl.pallas_call(
        functools.partial(_rglru_prompt_kernel, tb=tb),
        grid=(t // tb,),
        in_specs=[pl.BlockSpec((tb, D_MODEL), lambda i: (i, 0)),
                  full(w_ag.shape), full(cw.shape), full(cb.shape), full(wa.shape), full(ba.shape),
                  full(wx.shape), full(bx.shape), full(c_lam.shape)],
        out_specs=[pl.BlockSpec((tb, D_RNN), lambda i: (i, 0)),
                   pl.BlockSpec((8, D_RNN), lambda i: (0, 0)),
                   pl.BlockSpec((8, D_RNN), lambda i: (0, 0))],
        out_shape=[jax.ShapeDtypeStruct((t, D_RNN), F32),
                   jax.ShapeDtypeStruct((8, D_RNN), F32),
                   jax.ShapeDtypeStruct((8, D_RNN), F32)],
        scratch_shapes=[pltpu.VMEM((tb + 8, D_RNN), F32), pltpu.VMEM((8, D_RNN), F32)],
        compiler_params=_cparams(),
        name="rglru_prompt",
    )(x, w_ag, cw, cb, wa, ba, wx, bx, c_lam)


def _s5_readout(u, xr_blocks, xi_blocks, cre_ref, cim_ref, d, wglu, bglu):
    ys = []
    for j in range(N_SSM_BLOCKS):
        ys.append(_dotw(xr_blocks[j], cre_ref[j]) - _dotw(xi_blocks[j], cim_ref[j]))
    y = jnp.concatenate(ys, axis=1) + d * u
    g = _gelu(y)
    return g * _sigmoid(_dotw(g, wglu) + bglu)


def _s5_prompt_kernel(x_ref, w_ref, bbr_ref, bbi_ref, ar_ref, ai_ref, cre_ref, cim_ref, d_ref,
                      wglu_ref, bglu_ref, yb_ref, sre_ref, sim_ref, sr_scr, si_scr, *, tb):
    step = pl.program_id(0)

    @pl.when(step == 0)
    def _():
        sr_scr[...] = jnp.zeros_like(sr_scr)
        si_scr[...] = jnp.zeros_like(si_scr)

    u = _dotw(x_ref[...], w_ref[...])
    row = lax.broadcasted_iota(jnp.int32, (tb, STATE_BLOCK), 0)
    xr_blocks, xi_blocks = [], []
    for j in range(N_SSM_BLOCKS):
        cols = slice(j * STATE_BLOCK, (j + 1) * STATE_BLOCK)
        uj = u[:, j * LANES:(j + 1) * LANES]
        br = _dotw(uj, bbr_ref[j])
        bi = _dotw(uj, bbi_ref[j])
        ar = ar_ref[:, cols]
        ai = ai_ref[:, cols]
        pr = sr_scr[0:1, cols]
        pi = si_scr[0:1, cols]
        br = br + jnp.where(row == 0, ar * pr - ai * pi, 0.0)
        bi = bi + jnp.where(row == 0, ar * pi + ai * pr, 0.0)
        xr, xi = _scan_rows_const(ar, ai, br, bi)
        sr_scr[:, cols] = jnp.broadcast_to(xr[tb - 1:tb, :], (8, STATE_BLOCK))
        si_scr[:, cols] = jnp.broadcast_to(xi[tb - 1:tb, :], (8, STATE_BLOCK))
        xr_blocks.append(xr)
        xi_blocks.append(xi)
    sre_ref[...] = sr_scr[...]
    sim_ref[...] = si_scr[...]
    yb_ref[...] = _s5_readout(u, xr_blocks, xi_blocks, cre_ref, cim_ref, d_ref[...],
                              wglu_ref[...], bglu_ref[...])


def _s5_prompt(x, w_b, bbr, bbi, a_re, a_im, cre, cim, d, wglu, bglu, tb):
    t = x.shape[0]
    full = lambda shape: pl.BlockSpec(shape, lambda i: (0,) * len(shape))
    return pl.pallas_call(
        functools.partial(_s5_prompt_kernel, tb=tb),
        grid=(t // tb,),
        in_specs=[pl.BlockSpec((tb, D_MODEL), lambda i: (i, 0)),
                  full(w_b.shape), full(bbr.shape), full(bbi.shape), full(a_re.shape), full(a_im.shape),
                  full(cre.shape), full(cim.shape), full(d.shape), full(wglu.shape), full(bglu.shape)],
        out_specs=[pl.BlockSpec((tb, D_SSM), lambda i: (i, 0)),
                   pl.BlockSpec((8, N_STATE), lambda i: (0, 0)),
                   pl.BlockSpec((8, N_STATE), lambda i: (0, 0))],
        out_shape=[jax.ShapeDtypeStruct((t, D_SSM), F32),
                   jax.ShapeDtypeStruct((8, N_STATE), F32),
                   jax.ShapeDtypeStruct((8, N_STATE), F32)],
        scratch_shapes=[pltpu.VMEM((8, N_STATE), F32), pltpu.VMEM((8, N_STATE), F32)],
        compiler_params=_cparams(),
        name="s5_prompt",
    )(x, w_b, bbr, bbi, a_re, a_im, cre, cim, d, wglu, bglu)


def _out_ln_kernel(*refs, n_in):
    a_refs = refs[:n_in]
    w_refs = refs[n_in:2 * n_in]
    x_ref, g_ref, b_ref, o_ref = refs[2 * n_in:]
    y = ALPHA * x_ref[...]
    for a_ref, w_ref in zip(a_refs, w_refs):
        y = y + _dotw(a_ref[...], w_ref[...])
    o_ref[...] = _layer_norm(y, g_ref[...], b_ref[...])


def _out_ln(acts, weights, x, g, b, tm, name):
    m = x.shape[0]
    n_in = len(acts)
    full = lambda shape: pl.BlockSpec(shape, lambda i: (0,) * len(shape))
    in_specs = ([pl.BlockSpec((tm, a.shape[1]), lambda i: (i, 0)) for a in acts]
                + [full(w.shape) for w in weights]
                + [pl.BlockSpec((tm, D_MODEL), lambda i: (i, 0)), full(g.shape), full(b.shape)])
    return pl.pallas_call(
        functools.partial(_out_ln_kernel, n_in=n_in),
        grid=(m // tm,),
        in_specs=in_specs,
        out_specs=pl.BlockSpec((tm, D_MODEL), lambda i: (i, 0)),
        out_shape=jax.ShapeDtypeStruct((m, D_MODEL), F32),
        compiler_params=_cparams(),
        name=name,
    )(*acts, *weights, x, g, b)


def _ffn_prompt_kernel(x_ref, wup_ref, wgate_ref, cw_ref, cb_ref, wdown_ref, g_ref, b_ref,
                       o_ref, tail_ref, h_scr, carry_scr, *, tm):
    step = pl.program_id(0)

    @pl.when(step == 0)
    def _():
        carry_scr[...] = jnp.zeros_like(carry_scr)

    x = x_ref[...]
    xb = x.astype(BF16)
    row = lax.broadcasted_iota(jnp.int32, (tm, FF_CHUNK), 0)
    for c in range(N_FF_CHUNKS):
        cols = slice(c * FF_CHUNK, (c + 1) * FF_CHUNK)
        up = jnp.dot(xb, wup_ref[:, cols], preferred_element_type=F32)
        gate = jnp.dot(xb, wgate_ref[:, cols], preferred_element_type=F32)
        prev1 = carry_scr[7:8, cols]
        prev2 = carry_scr[6:7, cols]
        up1 = jnp.where(row >= 1, pltpu.roll(up, 1, 0), prev1)
        up2 = jnp.where(row >= 2, pltpu.roll(up, 2, 0), jnp.where(row == 1, prev1, prev2))
        hc = (cb_ref[:, cols] + up2 * cw_ref[0:1, cols] + up1 * cw_ref[1:2, cols]
              + up * cw_ref[2:3, cols])
        h_scr[:, cols] = (_gelu(hc) * gate).astype(BF16)
        carry_scr[:, cols] = up[tm - 8:tm, :]
    tail_ref[...] = carry_scr[...]
    f = jnp.dot(h_scr[...], wdown_ref[...], preferred_element_type=F32)
    o_ref[...] = _layer_norm(ALPHA * x + f, g_ref[...], b_ref[...])


def _ffn_prompt(x, wup, wgate, cw, cb, wdown, g, b, tm):
    t = x.shape[0]
    const = lambda shape: pl.BlockSpec(shape, lambda i: (0,) * len(shape),
                                       pipeline_mode=pl.Buffered(1))
    return pl.pallas_call(
        functools.partial(_ffn_prompt_kernel, tm=tm),
        grid=(t // tm,),
        in_specs=[pl.BlockSpec((tm, D_MODEL), lambda i: (i, 0)),
                  const(wup.shape), const(wgate.shape), const(cw.shape), const(cb.shape),
                  const(wdown.shape), const(g.shape), const(b.shape)],
        out_specs=[pl.BlockSpec((tm, D_MODEL), lambda i: (i, 0)),
                   pl.BlockSpec((8, D_FF), lambda i: (0, 0))],
        out_shape=[jax.ShapeDtypeStruct((t, D_MODEL), F32),
                   jax.ShapeDtypeStruct((8, D_FF), F32)],
        scratch_shapes=[pltpu.VMEM((tm, D_FF), BF16), pltpu.VMEM((8, D_FF), F32)],
        compiler_params=_cparams(),
        name="ffn_prompt",
    )(x, wup, wgate, cw, cb, wdown, g, b)


def _aug_lhs(f3):
    h1, h2, h3 = (h.astype(F32) for h in _split3(f3))
    lane = lax.broadcasted_iota(jnp.int32, f3.shape, 1)
    pieces = jnp.where(lane < N_HEADS, h1, jnp.where(lane < 2 * N_HEADS, h2, jnp.where(
        lane < 3 * N_HEADS, h3, jnp.where(lane < 4 * N_HEADS, 1.0, 0.0))))
    return pieces.astype(BF16)


def _qkv_prompt_kernel(x_ref, w_ref, wf_ref, bf_ref, selq_ref, selk_ref,
                       k_ref, v_ref, lf_ref, qa_ref, ka_ref, va_ref, f_scr, *, tm):
    step = pl.program_id(0)

    @pl.when(step == 0)
    def _():
        f_scr[...] = jnp.zeros_like(f_scr)

    x = x_ref[...]
    qkv = _dotw(x, w_ref[...])
    k = qkv[:, D_MODEL:2 * D_MODEL]
    v = qkv[:, 2 * D_MODEL:]
    k_ref[...] = k
    v_ref[...] = v
    logf3 = _log_sigmoid(_dotw(x, wf_ref[...]) + bf_ref[...])
    lf_ref[...] = logf3

    r_i = lax.broadcasted_iota(jnp.int32, (tm, tm), 0)
    c_i = lax.broadcasted_iota(jnp.int32, (tm, tm), 1)
    tri = jnp.where(r_i >= c_i, 1.0, 0.0).astype(BF16)
    f3 = _dot_exact01(tri, logf3) + f_scr[0:1, :]
    f_scr[...] = jnp.broadcast_to(f3[tm - 1:tm, :], f_scr.shape)

    lhs = _aug_lhs(f3 * LOG2E)
    augq = jnp.dot(lhs, selq_ref[...], preferred_element_type=F32)
    augk = jnp.dot(lhs, selk_ref[...], preferred_element_type=F32)
    lane = lax.broadcasted_iota(jnp.int32, (tm, LANES), 1)
    low = lane < HEAD_DIM
    vtail = jnp.where(lane == HEAD_DIM, 1.0, 0.0)
    scale = HEAD_DIM ** -0.5 * LOG2E
    for hp in range(N_HEADS // 2):
        cols = slice(hp * LANES, (hp + 1) * LANES)
        qb = qkv[:, cols] * scale
        kb = k[:, cols]
        vb = v[:, cols]
        for odd in range(2):
            h = 2 * hp + odd
            acols = slice(h * LANES, (h + 1) * LANES)
            if odd:
                qb, kb, vb = (pltpu.roll(qb, HEAD_DIM, 1), pltpu.roll(kb, HEAD_DIM, 1),
                              pltpu.roll(vb, HEAD_DIM, 1))
            qa_ref[h] = jnp.where(low, qb, augq[:, acols]).astype(BF16)
            ka_ref[h] = jnp.where(low, kb, augk[:, acols]).astype(BF16)
            va_ref[h] = jnp.where(low, vb, vtail).astype(BF16)


def _qkv_prompt(x, w_qkv, wf3, bf3, selq, selk, tm):
    t = x.shape[0]
    full = lambda shape: pl.BlockSpec(shape, lambda i: (0,) * len(shape))
    head_major = jax.ShapeDtypeStruct((N_HEADS, t, LANES), BF16)
    head_spec = pl.BlockSpec((N_HEADS, tm, LANES), lambda i: (0, i, 0))
    return pl.pallas_call(
        functools.partial(_qkv_prompt_kernel, tm=tm),
        grid=(t // tm,),
        in_specs=[pl.BlockSpec((tm, D_MODEL), lambda i: (i, 0)),
                  full(w_qkv.shape), full(wf3.shape), full(bf3.shape), full(selq.shape),
                  full(selk.shape)],
        out_specs=[pl.BlockSpec((tm, D_MODEL), lambda i: (i, 0)),
                   pl.BlockSpec((tm, D_MODEL), lambda i: (i, 0)),
                   pl.BlockSpec((tm, LANES), lambda i: (i, 0)),
                   head_spec, head_spec, head_spec],
        out_shape=[jax.ShapeDtypeStruct((t, D_MODEL), F32),
                   jax.ShapeDtypeStruct((t, D_MODEL), F32),
                   jax.ShapeDtypeStruct((t, LANES), F32),
                   head_major, head_major, head_major],
        scratch_shapes=[pltpu.VMEM((8, LANES), F32)],
        compiler_params=_cparams(),
        name="qkv_prompt",
    )(x, w_qkv, wf3, bf3, selq, selk)


def _flash_kernel(it_ref, jt_ref, q_ref, k_ref, v_ref, o_ref, m_scr, acc_scr, *, tq, tk, qc):
    s_idx = pl.program_id(1)
    i = it_ref[s_idx]
    j = jt_ref[s_idx]

    @pl.when(j == 0)
    def _():
        m_scr[...] = jnp.full(m_scr.shape, NEG_BIG, F32)
        acc_scr[...] = jnp.zeros_like(acc_scr)

    def accumulate(masked):
        chunks = [(hh, c) for hh in range(2) for c in range(tq // qc)]

        def scores(hh, c):
            rows = slice(c * qc, (c + 1) * qc)
            s = lax.dot_general(q_ref[hh, rows, :], k_ref[hh], (((1,), (1,)), ((), ())),
                                preferred_element_type=F32)
            if masked:
                qpos = i * tq + c * qc + lax.broadcasted_iota(jnp.int32, (qc, tk), 0)
                kpos = j * tk + lax.broadcasted_iota(jnp.int32, (qc, tk), 1)
                s = jnp.where(kpos <= qpos, s, NEG_BIG)
            return s

        ahead = 2
        pending = [scores(*chunks[n]) for n in range(ahead)]
        for n, (hh, c) in enumerate(chunks):
            s = pending.pop(0)
            if n + ahead < len(chunks):
                pending.append(scores(*chunks[n + ahead]))
            rows = slice(c * qc, (c + 1) * qc)
            m_prev = m_scr[hh, rows, :]
            m_next = jnp.maximum(m_prev, jnp.max(s, axis=1, keepdims=True))
            alpha = jnp.exp2(m_prev - m_next)
            p = jnp.exp2(s - m_next[:, 0:1]).astype(BF16)
            acc_scr[hh, rows, :] = alpha * acc_scr[hh, rows, :] + jnp.dot(
                p, v_ref[hh], preferred_element_type=F32)
            m_scr[hh, rows, :] = m_next

    crosses_diagonal = (j + 1) * tk - 1 > i * tq

    @pl.when(crosses_diagonal)
    def _():
        accumulate(True)

    @pl.when(jnp.logical_not(crosses_diagonal))
    def _():
        accumulate(False)

    @pl.when(j == ((i + 1) * tq - 1) // tk)
    def _():
        lane = lax.broadcasted_iota(jnp.int32, (tq, LANES), 1)
        outs = []
        for hh in range(2):
            acc = acc_scr[hh]
            outs.append(acc / acc[:, HEAD_DIM:HEAD_DIM + 1])
        o_ref[...] = jnp.where(lane < HEAD_DIM, outs[0], pltpu.roll(outs[1], HEAD_DIM, 1))


def _flash(qa, ka, va, tq, tk, qc):
    t = ka.shape[1]
    nq = t // tq
    i_list, j_list = [], []
    for i in range(nq):
        for j in range(((i + 1) * tq - 1) // tk + 1):
            i_list.append(i)
            j_list.append(j)
    it = jnp.asarray(i_list, jnp.int32)
    jt = jnp.asarray(j_list, jnp.int32)
    grid_spec = pltpu.PrefetchScalarGridSpec(
        num_scalar_prefetch=2,
        grid=(N_HEADS // 2, len(i_list)),
        in_specs=[pl.BlockSpec((2, tq, LANES), lambda hp, s, it, jt: (hp, it[s], 0)),
                  pl.BlockSpec((2, tk, LANES), lambda hp, s, it, jt: (hp, jt[s], 0)),
                  pl.BlockSpec((2, tk, LANES), lambda hp, s, it, jt: (hp, jt[s], 0))],
        out_specs=pl.BlockSpec((tq, LANES), lambda hp, s, it, jt: (it[s], hp)),
        scratch_shapes=[pltpu.VMEM((2, tq, LANES), F32), pltpu.VMEM((2, tq, LANES), F32)],
    )
    return pl.pallas_call(
        functools.partial(_flash_kernel, tq=tq, tk=tk, qc=qc),
        grid_spec=grid_spec,
        out_shape=jax.ShapeDtypeStruct((t, D_MODEL), F32),
        compiler_params=_cparams(2),
        name="flash_prompt",
    )(it, jt, qa, ka, va)


def _mix0_sample_kernel(x_ref, w_ref, c0_ref, c1_ref, c2_ref, cw_ref, cb_ref, wa_ref, ba_ref,
                        wx_ref, bx_ref, cl_ref, h0_ref, sr_ref, si_ref, bbr_ref, bbi_ref,
                        ar_ref, ai_ref, cre_ref, cim_ref, d_ref, wglu_ref, bglu_ref,
                        ya_ref, yb_ref, xa_ref, h_ref, xr_ref, xi_ref):
    proj = _dotw(x_ref[...], w_ref[...])
    xa = proj[:, :D_RNN]
    ga = proj[:, D_RNN:2 * D_RNN]
    u = proj[:, 2 * D_RNN:]
    xa_ref[...] = xa
    xc = (cb_ref[...] + c0_ref[...] * cw_ref[0:1, :] + c1_ref[...] * cw_ref[1:2, :]
          + c2_ref[...] * cw_ref[2:3, :] + xa * cw_ref[3:4, :])
    a, inp = _rglru_gates(xc, wa_ref[...], ba_ref[...], wx_ref[...], bx_ref[...], cl_ref[...])
    h = a * h0_ref[...] + inp
    h_ref[...] = h
    ya_ref[...] = h * _gelu(ga)

    xr_blocks, xi_blocks = [], []
    for j in range(N_SSM_BLOCKS):
        cols = slice(j * STATE_BLOCK, (j + 1) * STATE_BLOCK)
        uj = u[:, j * LANES:(j + 1) * LANES]
        ar = ar_ref[:, cols]
        ai = ai_ref[:, cols]
        pr = sr_ref[:, cols]
        pi = si_ref[:, cols]
        xr = _dotw(uj, bbr_ref[j]) + (ar * pr - ai * pi)
        xi = _dotw(uj, bbi_ref[j]) + (ar * pi + ai * pr)
        xr_ref[:, cols] = xr
        xi_ref[:, cols] = xi
        xr_blocks.append(xr)
        xi_blocks.append(xi)
    yb_ref[...] = _s5_readout(u, xr_blocks, xi_blocks, cre_ref, cim_ref, d_ref[...],
                              wglu_ref[...], bglu_ref[...])


def _mix0_sample(*args):
    n = args[0].shape[0]
    shapes = [(n, D_RNN), (n, D_SSM), (n, D_RNN), (n, D_RNN), (n, N_STATE), (n, N_STATE)]
    return pl.pallas_call(
        _mix0_sample_kernel,
        out_shape=[jax.ShapeDtypeStruct(s, F32) for s in shapes],
        compiler_params=pltpu.CompilerParams(vmem_limit_bytes=VMEM_LIMIT),
        name="mix0_sample",
    )(*args)


def _ffn_sample_kernel(x_ref, wup_ref, wgate_ref, b0_ref, b1_ref, cw_ref, cb_ref, wdown_ref,
                       g_ref, b_ref, o_ref, up_ref, acc_scr):
    c = pl.program_id(0)

    @pl.when(c == 0)
    def _():
        acc_scr[...] = jnp.zeros_like(acc_scr)

    x = x_ref[...]
    up = _dotw(x, wup_ref[...])
    gate = _dotw(x, wgate_ref[...])
    up_ref[...] = up
    hc = (cb_ref[...] + b0_ref[...] * cw_ref[0:1, :] + b1_ref[...] * cw_ref[1:2, :]
          + up * cw_ref[2:3, :])
    acc_scr[...] += _dotw(_gelu(hc) * gate, wdown_ref[...])

    @pl.when(c == pl.num_programs(0) - 1)
    def _():
        o_ref[...] = _layer_norm(ALPHA * x + acc_scr[...], g_ref[...], b_ref[...])


def _ffn_sample(x, wup, wgate, buf0, buf1, cw, cb, wdown, g, b):
    n = x.shape[0]
    fc = FF_CHUNK
    col = lambda rows: pl.BlockSpec((rows, fc), lambda c: (0, c))
    full = lambda shape: pl.BlockSpec(shape, lambda c: (0,) * len(shape))
    return pl.pallas_call(
        _ffn_sample_kernel,
        grid=(D_FF // fc,),
        in_specs=[full(x.shape), col(D_MODEL), col(D_MODEL), col(n), col(n), col(cw.shape[0]),
                  col(1), pl.BlockSpec((fc, D_MODEL), lambda c: (c, 0)), full(g.shape),
                  full(b.shape)],
        out_specs=[full((n, D_MODEL)), col(n)],
        out_shape=[jax.ShapeDtypeStruct((n, D_MODEL), F32), jax.ShapeDtypeStruct((n, D_FF), F32)],
        scratch_shapes=[pltpu.VMEM((n, D_MODEL), F32)],
        compiler_params=_cparams(),
        name="ffn_sample",
    )(x, wup, wgate, buf0, buf1, cw, cb, wdown, g, b)


def _qkv_sample_kernel(x_ref, w_ref, wf_ref, bf_ref, o_ref, lf_ref):
    x = x_ref[...]
    o_ref[...] = _dotw(x, w_ref[...])

    @pl.when(pl.program_id(0) == 0)
    def _():
        lf_ref[...] = _log_sigmoid(_dotw(x, wf_ref[...]) + bf_ref[...])


def _qkv_sample(x, w_in1, wf, bf):
    n = x.shape[0]
    full = lambda shape: pl.BlockSpec(shape, lambda c: (0,) * len(shape))
    return pl.pallas_call(
        _qkv_sample_kernel,
        grid=(3,),
        in_specs=[full(x.shape), pl.BlockSpec((D_MODEL, D_MODEL), lambda c: (0, c)),
                  full(wf.shape), full(bf.shape)],
        out_specs=[pl.BlockSpec((n, D_MODEL), lambda c: (0, c)), full((n, LANES))],
        out_shape=[jax.ShapeDtypeStruct((n, 3 * D_MODEL), F32),
                   jax.ShapeDtypeStruct((n, LANES), F32)],
        compiler_params=_cparams(),
        name="qkv_sample",
    )(x, w_in1, wf, bf)


def _attn_sample_kernel(pt_ref, q_ref, kn_ref, vn_ref, gt_ref, *refs, pages_per_step):
    pp = pages_per_step
    k_refs = refs[:pp]
    v_refs = refs[pp:2 * pp]
    lf_refs = refs[2 * pp:3 * pp]
    o_ref, m_scr, l_scr, acc_scr, carry_scr, pad_scr = refs[3 * pp:]
    s_idx = pl.program_id(1)
    rows_per_page = PAGE_SIZE * N_HEADS
    tok_per_row = LANES // N_HEADS
    n_rows = PAGE_SIZE // tok_per_row

    @pl.when(s_idx == 0)
    def _():
        m_scr[...] = jnp.full(m_scr.shape, NEG_BIG, F32)
        l_scr[...] = jnp.zeros_like(l_scr)
        acc_scr[...] = jnp.zeros_like(acc_scr)
        carry_scr[...] = jnp.zeros_like(carry_scr)
        pad_scr[...] = jnp.zeros_like(pad_scr)

    scale = HEAD_DIM ** -0.5
    q = q_ref[0]
    qs = (q * scale).astype(BF16)
    rowi = lax.broadcasted_iota(jnp.int32, (N_HEADS, LANES), 0)
    lanei = lax.broadcasted_iota(jnp.int32, (N_HEADS, LANES), 1)
    own_head = (lanei % N_HEADS) == rowi
    r_i = lax.broadcasted_iota(jnp.int32, (LANES, LANES), 0)
    c_i = lax.broadcasted_iota(jnp.int32, (LANES, LANES), 1)
    same_head = (r_i % N_HEADS) == (c_i % N_HEADS)
    w_later = jnp.where(same_head & (r_i > c_i), 1.0, 0.0).astype(BF16)
    w_all = jnp.where(same_head, 1.0, 0.0).astype(BF16)
    gt = gt_ref[0]

    sts, v2s, withins, row_tots, sufs = [], [], [], [], []
    for r in range(pp):
        k2 = k_refs[r][0].reshape(rows_per_page, HEAD_DIM).astype(BF16)
        v2s.append(v_refs[r][0].reshape(rows_per_page, HEAD_DIM).astype(BF16))
        sts.append(lax.dot_general(qs, k2, (((1,), (1,)), ((), ())),
                                   preferred_element_type=F32))
        base = r * PAGE_SIZE
        pad_scr[base:base + PAGE_SIZE, 0:N_HEADS] = lf_refs[r][0]
        lf16 = pad_scr[pl.ds(base, n_rows, stride=tok_per_row), :]
        for t in range(1, tok_per_row):
            lf16 = lf16 + pltpu.roll(pad_scr[pl.ds(base + t, n_rows, stride=tok_per_row), :],
                                     N_HEADS * t, 1)
        pieces = _split3(lf16)
        withins.append(sum(jnp.dot(pc, w_later, preferred_element_type=F32) for pc in pieces))
        row_tot = sum(jnp.dot(pc, w_all, preferred_element_type=F32) for pc in pieces)
        suf = row_tot
        d = 1
        while d < n_rows:
            suf = suf + jnp.where(rowi + d < n_rows, pltpu.roll(suf, n_rows - d, 0), 0.0)
            d *= 2
        row_tots.append(row_tot)
        sufs.append(suf)

    carry = carry_scr[0:1, :]
    blocks = []
    for r in range(pp):
        bias = withins[r] + (sufs[r] - row_tots[r]) + (carry + gt)
        carry = carry + sufs[r][0:1, :]
        for rr in range(n_rows):
            blk = sts[r][:, rr * LANES:(rr + 1) * LANES] + bias[rr:rr + 1, :]
            blocks.append(jnp.where(own_head, blk, NEG_BIG))
    carry_scr[...] = jnp.broadcast_to(carry, carry_scr.shape)

    mx = blocks[0]
    for blk in blocks[1:]:
        mx = jnp.maximum(mx, blk)
    m_prev = m_scr[:, 0:1]
    m_next = jnp.maximum(m_prev, jnp.max(mx, axis=1, keepdims=True))
    alpha = jnp.exp(m_prev - m_next)
    ps = [jnp.exp(blk - m_next) for blk in blocks]
    psum = ps[0]
    for pblk in ps[1:]:
        psum = psum + pblk
    l_scr[...] = jnp.broadcast_to(
        alpha * l_scr[:, 0:1] + jnp.sum(psum, axis=1, keepdims=True), l_scr.shape)
    acc = alpha * acc_scr[...]
    for r in range(pp):
        p_page = jnp.concatenate(ps[r * n_rows:(r + 1) * n_rows], axis=1).astype(BF16)
        acc = acc + jnp.dot(p_page, v2s[r], preferred_element_type=F32)
    acc_scr[...] = acc
    m_scr[...] = jnp.broadcast_to(m_next, m_scr.shape)

    @pl.when(s_idx == pl.num_programs(1) - 1)
    def _():
        s_self = jnp.sum(q * scale * kn_ref[0], axis=1, keepdims=True)
        m_prev = m_scr[:, 0:1]
        m_fin = jnp.maximum(m_prev, s_self)
        a_past = jnp.exp(m_prev - m_fin)
        a_self = jnp.exp(s_self - m_fin)
        l_fin = a_past * l_scr[:, 0:1] + a_self
        o_ref[0] = (a_past * acc_scr[...] + a_self * vn_ref[0]) / l_fin


def _attn_sample(page_table, q, k_new, v_new, logf_new, cache_k, cache_v, cache_logf,
                 pages_per_step):
    n_seq, n_pages = page_table.shape
    pp = pages_per_step
    pt = page_table.reshape(-1)
    heads = lambda a: a.reshape(n_seq, N_HEADS, HEAD_DIM)
    gt = jnp.tile(logf_new, (1, LANES // N_HEADS)).reshape(n_seq, 1, LANES)

    def page_map(r, ndim):
        return lambda b, s, pt: (pt[b * n_pages + (n_pages - 1 - (s * pp + r))],) + (0,) * ndim

    per_seq = lambda: pl.BlockSpec((1, N_HEADS, HEAD_DIM), lambda b, s, pt: (b, 0, 0))
    in_specs = [per_seq(), per_seq(), per_seq(),
                pl.BlockSpec((1, 1, LANES), lambda b, s, pt: (b, 0, 0))]
    kv_block = (1, PAGE_SIZE, N_HEADS, HEAD_DIM)
    in_specs += [pl.BlockSpec(kv_block, page_map(r, 3)) for r in range(pp)]
    in_specs += [pl.BlockSpec(kv_block, page_map(r, 3)) for r in range(pp)]
    in_specs += [pl.BlockSpec((1, PAGE_SIZE, N_HEADS), page_map(r, 2)) for r in range(pp)]
    grid_spec = pltpu.PrefetchScalarGridSpec(
        num_scalar_prefetch=1,
        grid=(n_seq, n_pages // pp),
        in_specs=in_specs,
        out_specs=per_seq(),
        scratch_shapes=[pltpu.VMEM((N_HEADS, LANES), F32), pltpu.VMEM((N_HEADS, LANES), F32),
                        pltpu.VMEM((N_HEADS, HEAD_DIM), F32), pltpu.VMEM((8, LANES), F32),
                        pltpu.VMEM((pp * PAGE_SIZE, LANES), F32)],
    )
    out = pl.pallas_call(
        functools.partial(_attn_sample_kernel, pages_per_step=pp),
        grid_spec=grid_spec,
        out_shape=jax.ShapeDtypeStruct((n_seq, N_HEADS, HEAD_DIM), F32),
        compiler_params=_cparams(2),
        name="attn_sample",
    )(pt, heads(q), heads(k_new), heads(v_new), gt,
      *([cache_k] * pp), *([cache_v] * pp), *([cache_logf] * pp))
    return out.reshape(n_seq, D_MODEL)


def _block_diag_dense(w):
    nb, n, _ = w.shape
    eye = jnp.eye(nb, dtype=w.dtype)
    return jnp.einsum('hij,hk->hikj', w, eye).reshape(nb * n, nb * n)


def _s5_discretize(lam_re, lam_im, log_dt, b_re, b_im):
    lr = jnp.minimum(lam_re, -1e-4)
    li = lam_im
    dt = jnp.exp(log_dt)[:, None]
    mag = jnp.exp(lr * dt)
    ab_re = mag * jnp.cos(li * dt)
    ab_im = mag * jnp.sin(li * dt)
    den = lr * lr + li * li
    nr = ab_re - 1.0
    zr = (nr * lr + ab_im * li) / den
    zi = (ab_im * lr - nr * li) / den
    bb_re = zr[..., None] * b_re - zi[..., None] * b_im
    bb_im = zr[..., None] * b_im + zi[..., None] * b_re
    return ab_re, ab_im, bb_re, bb_im


def _s5_in_blocks(bb):
    eye = jnp.eye(GROUPS_PER_BLOCK, dtype=bb.dtype)
    b4 = bb.transpose(0, 2, 1).reshape(N_SSM_BLOCKS, GROUPS_PER_BLOCK, S5_GROUP, S5_STATE)
    return jnp.einsum('jgmp,gh->jgmhp', b4, eye).reshape(N_SSM_BLOCKS, LANES, STATE_BLOCK)


def _s5_out_blocks(c):
    eye = jnp.eye(GROUPS_PER_BLOCK, dtype=c.dtype)
    c4 = c.reshape(N_SSM_BLOCKS, GROUPS_PER_BLOCK, S5_GROUP, S5_STATE)
    return jnp.einsum('jgmp,gh->jgphm', c4, eye).reshape(N_SSM_BLOCKS, STATE_BLOCK, LANES)


def _aug_selectors():
    rows = jnp.arange(LANES)[:, None]
    cols = jnp.arange(N_HEADS * LANES)[None, :]
    head = cols // LANES
    c = cols % LANES - HEAD_DIM
    piece = rows // N_HEADS
    rhead = rows % N_HEADS
    same = (rhead == head) & (rows < 4 * N_HEADS)
    f_piece = same & (piece < 3)
    ones_piece = same & (piece == 3)
    selq = jnp.where(f_piece & (c == piece), 1.0, 0.0) + jnp.where(
        ones_piece & (c >= 3) & (c < AUG_COLS), 1.0, 0.0)
    selk = jnp.where(ones_piece & (c >= 0) & (c < 3), 1.0, 0.0) - jnp.where(
        f_piece & (c == piece + 3), 1.0, 0.0)
    return selq.astype(BF16), selk.astype(BF16)


def _forget_cols(w_in1, b_fgt, dtype):
    wf = w_in1[:, 3 * D_MODEL:]
    pad = jnp.zeros((D_MODEL, LANES - 3 * N_HEADS), w_in1.dtype)
    wf3 = jnp.concatenate([wf, wf, wf, pad], axis=1).astype(dtype)
    bf3 = jnp.concatenate([b_fgt, b_fgt, b_fgt, jnp.zeros((LANES - 3 * N_HEADS,), F32)])[None, :]
    return wf3, bf3


def _row(v):
    return v.reshape(1, -1)


def _pick(t, pref):
    return pref if t % pref == 0 else t


def _prompt_path(x, p, s5p, tiles):
    t = x.shape[0]
    bf = lambda w: w.astype(BF16)
    tb = _pick(t, tiles['scan'])
    ya, tail, h_last = _rglru_prompt(
        x, bf(p['w_in0'][:, :2 * D_RNN]), p['rg_conv_w'], _row(p['rg_conv_b']),
        bf(s5p['wa_bd']), _row(p['rg_ba']), bf(s5p['wx_bd']), _row(p['rg_bx']), s5p['c_lam'], tb)
    yb, s_re, s_im = _s5_prompt(
        x, bf(p['w_in0'][:, 2 * D_RNN:]), bf(s5p['bbr']), bf(s5p['bbi']), s5p['a_re'], s5p['a_im'],
        bf(s5p['cre']), bf(s5p['cim']), _row(p['s5_d']), bf(p['s5_w_glu']), _row(p['s5_b_glu']), tb)
    tm = _pick(t, tiles['rows'])
    x1 = _out_ln([ya, yb], [bf(p['w_out0'][:D_RNN]), bf(p['w_out0'][D_RNN:])], x,
                 p['ln_g'][0, 0][None], p['ln_b'][0, 0][None], tm, "out0_prompt")
    x2, ffn_tail0 = _ffn_prompt(x1, bf(p['ffn_w_up'][0]), bf(p['ffn_w_gate'][0]), p['ffn_conv_w'][0],
                                p['ffn_conv_b'][0][None], bf(p['ffn_w_down'][0]),
                                p['ln_g'][0, 1][None], p['ln_b'][0, 1][None], tm)
    wf3, bf3 = _forget_cols(p['w_in1'], p['b_fgt'], BF16)
    selq, selk = _aug_selectors()
    k, v, lf, qa, ka, va = _qkv_prompt(x2, bf(p['w_in1'][:, :3 * D_MODEL]), wf3, bf3, selq, selk,
                                       _pick(t, tiles['qkv']))
    o = _flash(qa, ka, va, _pick(t, tiles['tq']), _pick(t, tiles['tk']), tiles['qc'])
    x3 = _out_ln([o], [bf(p['w_out1'])], x2, p['ln_g'][1, 0][None], p['ln_b'][1, 0][None], tm,
                 "out1_prompt")
    x4, ffn_tail1 = _ffn_prompt(x3, bf(p['ffn_w_up'][1]), bf(p['ffn_w_gate'][1]), p['ffn_conv_w'][1],
                                p['ffn_conv_b'][1][None], bf(p['ffn_w_down'][1]),
                                p['ln_g'][1, 1][None], p['ln_b'][1, 1][None], tm)
    return (x4[None], tail[None, 5:8], h_last[0:1], s_re[0:1].reshape(1, S5_GROUPS, S5_STATE),
            s_im[0:1].reshape(1, S5_GROUPS, S5_STATE), k.reshape(1, t, N_HEADS, HEAD_DIM),
            v.reshape(1, t, N_HEADS, HEAD_DIM), lf[None, :, :N_HEADS],
            jnp.stack([ffn_tail0[None, 6:8], ffn_tail1[None, 6:8]], axis=0))


def _sample_path(x, rg_conv, rg_h, s5_re, s5_im, cache_k, cache_v, cache_logf, ffn_conv,
                 page_table, p, s5p, pages_per_step):
    n = x.shape[0]
    ya, yb, xa, h_new, xr, xi = _mix0_sample(
        x, p['w_in0'], rg_conv[:, 0], rg_conv[:, 1], rg_conv[:, 2], p['rg_conv_w'],
        _row(p['rg_conv_b']), s5p['wa_bd'], _row(p['rg_ba']), s5p['wx_bd'], _row(p['rg_bx']),
        s5p['c_lam'], rg_h, s5_re.reshape(n, N_STATE), s5_im.reshape(n, N_STATE),
        s5p['bbr'], s5p['bbi'], s5p['a_re'], s5p['a_im'], s5p['cre'], s5p['cim'],
        _row(p['s5_d']), p['s5_w_glu'], _row(p['s5_b_glu']))
    x1 = _out_ln([ya, yb], [p['w_out0'][:D_RNN], p['w_out0'][D_RNN:]], x,
                 p['ln_g'][0, 0][None], p['ln_b'][0, 0][None], n, "out0_sample")
    x2, up0 = _ffn_sample(x1, p['ffn_w_up'][0], p['ffn_w_gate'][0], ffn_conv[0, :, 0],
                          ffn_conv[0, :, 1], p['ffn_conv_w'][0], p['ffn_conv_b'][0][None],
                          p['ffn_w_down'][0], p['ln_g'][0, 1][None], p['ln_b'][0, 1][None])
    wf3, bf3 = _forget_cols(p['w_in1'], p['b_fgt'], F32)
    qkv, lf3 = _qkv_sample(x2, p['w_in1'], wf3, bf3)
    q = qkv[:, :D_MODEL]
    k = qkv[:, D_MODEL:2 * D_MODEL]
    v = qkv[:, 2 * D_MODEL:]
    logf = lf3[:, :N_HEADS]
    o = _attn_sample(page_table, q, k, v, logf, cache_k, cache_v, cache_logf, pages_per_step)
    x3 = _out_ln([o], [p['w_out1']], x2, p['ln_g'][1, 0][None], p['ln_b'][1, 0][None], n,
                 "out1_sample")
    x4, up1 = _ffn_sample(x3, p['ffn_w_up'][1], p['ffn_w_gate'][1], ffn_conv[1, :, 0],
                          ffn_conv[1, :, 1], p['ffn_conv_w'][1], p['ffn_conv_b'][1][None],
                          p['ffn_w_down'][1], p['ln_g'][1, 1][None], p['ln_b'][1, 1][None])
    new_rg_conv = jnp.concatenate([rg_conv[:, 1:], xa[:, None]], axis=1)
    new_ffn = jnp.stack([jnp.stack([ffn_conv[0, :, 1], up0], axis=1),
                         jnp.stack([ffn_conv[1, :, 1], up1], axis=1)], axis=0)
    return (x4[:, None], new_rg_conv, h_new, xr.reshape(n, S5_GROUPS, S5_STATE),
            xi.reshape(n, S5_GROUPS, S5_STATE), k.reshape(n, 1, N_HEADS, HEAD_DIM),
            v.reshape(n, 1, N_HEADS, HEAD_DIM), logf[:, None], new_ffn)


def _derived_params(p):
    a_re, a_im, bb_re, bb_im = _s5_discretize(p['s5_lambda_re'], p['s5_lambda_im'], p['s5_log_dt'],
                                              p['s5_b_re'], p['s5_b_im'])
    return dict(
        wa_bd=_block_diag_dense(p['rg_wa']), wx_bd=_block_diag_dense(p['rg_wx']),
        c_lam=_row(-RG_C * jax.nn.softplus(-p['rg_lambda'])),
        a_re=a_re.reshape(1, N_STATE), a_im=a_im.reshape(1, N_STATE),
        bbr=_s5_in_blocks(bb_re), bbi=_s5_in_blocks(bb_im),
        cre=_s5_out_blocks(p['s5_c_re']), cim=_s5_out_blocks(p['s5_c_im']))


PROMPT_TILES = dict(scan=256, rows=512, qkv=256, tq=1024, tk=1024, qc=256)
PAGES_PER_STEP = 8


def kernel(x_prompt, x_sample, state_rglru_conv, state_rglru_h, state_s5_re, state_s5_im, cache_k, cache_v, cache_logf, state_ffn_conv, page_table, w_in0, rg_conv_w, rg_conv_b, rg_wa, rg_ba, rg_wx, rg_bx, rg_lambda, s5_lambda_re, s5_lambda_im, s5_log_dt, s5_b_re, s5_b_im, s5_c_re, s5_c_im, s5_d, s5_w_glu, s5_b_glu, w_out0, w_in1, b_fgt, w_out1, ffn_w_up, ffn_w_gate, ffn_conv_w, ffn_conv_b, ffn_w_down, ln_g, ln_b):
    p = dict(w_in0=w_in0, rg_conv_w=rg_conv_w, rg_conv_b=rg_conv_b, rg_wa=rg_wa, rg_ba=rg_ba,
             rg_wx=rg_wx, rg_bx=rg_bx, rg_lambda=rg_lambda, s5_lambda_re=s5_lambda_re,
             s5_lambda_im=s5_lambda_im, s5_log_dt=s5_log_dt, s5_b_re=s5_b_re, s5_b_im=s5_b_im,
             s5_c_re=s5_c_re, s5_c_im=s5_c_im, s5_d=s5_d, s5_w_glu=s5_w_glu, s5_b_glu=s5_b_glu,
             w_out0=w_out0, w_in1=w_in1, b_fgt=b_fgt, w_out1=w_out1, ffn_w_up=ffn_w_up,
             ffn_w_gate=ffn_w_gate, ffn_conv_w=ffn_conv_w, ffn_conv_b=ffn_conv_b,
             ffn_w_down=ffn_w_down, ln_g=ln_g, ln_b=ln_b)
    s5p = _derived_params(p)
    prompt = _prompt_path(x_prompt[0], p, s5p, PROMPT_TILES)
    sample = _sample_path(x_sample[:, 0], state_rglru_conv, state_rglru_h, state_s5_re, state_s5_im,
                          cache_k, cache_v, cache_logf, state_ffn_conv, page_table, p, s5p,
                          PAGES_PER_STEP)
    return (prompt[0], sample[0]) + tuple(prompt[1:]) + tuple(sample[1:])
```

```python
import functools

import jax
import jax.numpy as jnp
from jax import lax
from jax.experimental import pallas as pl
from jax.experimental.pallas import tpu as pltpu

F32 = jnp.float32
BF16 = jnp.bfloat16

D_MODEL = 1024
DEPTH = 2
PAGE_SIZE = 128
D_RNN = 512
RG_BLOCKS = 8
RG_BW = D_RNN // RG_BLOCKS
RG_C = 8.0
D_SSM = 512
S5_GROUP = 16
S5_GROUPS = D_SSM // S5_GROUP
S5_STATE = 64
N_STATE = S5_GROUPS * S5_STATE
N_HEADS = 16
HEAD_DIM = D_MODEL // N_HEADS
D_FF = 2816
ALPHA = (2.0 * DEPTH) ** 0.25
LN_EPS = 1e-5

LANES = 128
GROUPS_PER_BLOCK = LANES // S5_GROUP
N_SSM_BLOCKS = D_SSM // LANES
STATE_BLOCK = GROUPS_PER_BLOCK * S5_STATE
FF_CHUNK = 256
N_FF_CHUNKS = D_FF // FF_CHUNK
NEG_BIG = -1e30
LOG2E = 1.4426950408889634
AUG_COLS = 6
VMEM_LIMIT = 56 * 1024 * 1024


def _cparams(n_axes=1, vmem=VMEM_LIMIT):
    return pltpu.CompilerParams(dimension_semantics=("arbitrary",) * n_axes,
                                vmem_limit_bytes=vmem)


def _split3(x):
    h1 = x.astype(BF16)
    r1 = x - h1.astype(F32)
    h2 = r1.astype(BF16)
    h3 = (r1 - h2.astype(F32)).astype(BF16)
    return h1, h2, h3


def _dotw(x, w):
    if w.dtype == BF16:
        return jnp.dot(x.astype(BF16), w, preferred_element_type=F32)
    xh = x.astype(BF16)
    xl = (x - xh.astype(F32)).astype(BF16)
    wh = w.astype(BF16)
    wl = (w - wh.astype(F32)).astype(BF16)
    return (jnp.dot(xh, wh, preferred_element_type=F32)
            + jnp.dot(xl, wh, preferred_element_type=F32)
            + jnp.dot(xh, wl, preferred_element_type=F32))


def _dot_exact01(m01, x):
    h1, h2, h3 = _split3(x)
    return (jnp.dot(m01, h1, preferred_element_type=F32)
            + jnp.dot(m01, h2, preferred_element_type=F32)
            + jnp.dot(m01, h3, preferred_element_type=F32))


def _gelu(x):
    return jax.nn.gelu(x)


def _sigmoid(x):
    return jax.nn.sigmoid(x)


def _log_sigmoid(x):
    return jnp.minimum(x, 0.0) - jnp.log1p(jnp.exp(-jnp.abs(x)))


def _expm1(x):
    u = jnp.exp(x)
    um1 = u - 1.0
    return jnp.where(um1 == 0.0, x, jnp.where(um1 == -1.0, -1.0, um1 * x / jnp.log(u)))


def _layer_norm(xf, g, b):
    mean = jnp.mean(xf, axis=-1, keepdims=True)
    xc = xf - mean
    var = jnp.mean(xc * xc, axis=-1, keepdims=True)
    return xc * lax.rsqrt(var + LN_EPS) * g + b


def _rglru_gates(xc, wa, ba, wx, bx, c_lam):
    r = _sigmoid(_dotw(xc, wa) + ba)
    i = _sigmoid(_dotw(xc, wx) + bx)
    log_a = c_lam * r
    a = jnp.exp(log_a)
    inp = jnp.sqrt(-_expm1(2.0 * log_a)) * (i * xc)
    return a, inp


def _scan_rows(a, b):
    n = a.shape[0]
    row = lax.broadcasted_iota(jnp.int32, a.shape, 0)
    d = 1
    while d < n:
        valid = row >= d
        b = b + jnp.where(valid, a * pltpu.roll(b, d, 0), 0.0)
        if 2 * d < n:
            a = jnp.where(valid, a * pltpu.roll(a, d, 0), a)
        d *= 2
    return b


def _scan_rows_const(lvl_re, lvl_im, pow_re, pow_im, br, bi, pr, pi):
    n, c = br.shape
    g = n // 8
    for k in range(3):
        d = 1 << k
        sr = pltpu.roll(br, d, 0).reshape(g, 8, c)
        si = pltpu.roll(bi, d, 0).reshape(g, 8, c)
        cr = lvl_re[k][None]
        ci = lvl_im[k][None]
        br, bi = ((br.reshape(g, 8, c) + (cr * sr - ci * si)).reshape(n, c),
                  (bi.reshape(g, 8, c) + (cr * si + ci * sr)).reshape(n, c))
    b3r = br.reshape(g, 8, c)
    b3i = bi.reshape(g, 8, c)
    outs_r, outs_i = [], []
    for gi in range(g):
        xr = b3r[gi] + (pow_re * pr - pow_im * pi)
        xi = b3i[gi] + (pow_re * pi + pow_im * pr)
        pr, pi = xr[7:8, :], xi[7:8, :]
        outs_r.append(xr)
        outs_i.append(xi)
    return jnp.concatenate(outs_r, axis=0), jnp.concatenate(outs_i, axis=0)


def _rglru_prompt_kernel(x_ref, w_ref, cw_ref, cb_ref, wa_ref, ba_ref, wx_ref, bx_ref, cl_ref,
                         ya_ref, tail_ref, h_ref, ext_scr, h_scr, *, tb):
    step = pl.program_id(0)

    @pl.when(step == 0)
    def _():
        ext_scr[0:8, :] = jnp.zeros((8, D_RNN), F32)
        h_scr[...] = jnp.zeros_like(h_scr)

    proj = _dotw(x_ref[...], w_ref[...])
    xa = proj[:, :D_RNN]
    ga = proj[:, D_RNN:]
    ext_scr[8:8 + tb, :] = xa
    xc = cb_ref[...] + xa * cw_ref[3:4, :]
    for j in range(3):
        xc = xc + ext_scr[pl.ds(5 + j, tb), :] * cw_ref[j:j + 1, :]
    tail = ext_scr[tb:tb + 8, :]
    ext_scr[0:8, :] = tail
    tail_ref[...] = tail

    a, inp = _rglru_gates(xc, wa_ref[...], ba_ref[...], wx_ref[...], bx_ref[...], cl_ref[...])
    row = lax.broadcasted_iota(jnp.int32, a.shape, 0)
    inp = inp + jnp.where(row == 0, a * h_scr[0:1, :], 0.0)
    h = _scan_rows(a, inp)
    h_last = jnp.broadcast_to(h[tb - 1:tb, :], (8, D_RNN))
    h_scr[...] = h_last
    h_ref[...] = h_last
    ya_ref[...] = h * _gelu(ga)


def _rglru_prompt(x, w_ag, cw, cb, wa, ba, wx, bx, c_lam, tb):
    t = x.shape[0]
    full = lambda shape: pl.BlockSpec(shape, lambda i: (0,) * len(shape))
    return pl.pallas_call(
        functools.partial(_rglru_prompt_kernel, tb=tb),
        grid=(t // tb,),
        in_specs=[pl.BlockSpec((tb, D_MODEL), lambda i: (i, 0)),
                  full(w_ag.shape), full(cw.shape), full(cb.shape), full(wa.shape), full(ba.shape),
                  full(wx.shape), full(bx.shape), full(c_lam.shape)],
        out_specs=[pl.BlockSpec((tb, D_RNN), lambda i: (i, 0)),
                   pl.BlockSpec((8, D_RNN), lambda i: (0, 0)),
                   pl.BlockSpec((8, D_RNN), lambda i: (0, 0))],
        out_shape=[jax.ShapeDtypeStruct((t, D_RNN), F32),
                   jax.ShapeDtypeStruct((8, D_RNN), F32),
                   jax.ShapeDtypeStruct((8, D_RNN), F32)],
        scratch_shapes=[pltpu.VMEM((tb + 8, D_RNN), F32), pltpu.VMEM((8, D_RNN), F32)],
        compiler_params=_cparams(),
        name="rglru_prompt",
    )(x, w_ag, cw, cb, wa, ba, wx, bx, c_lam)


def _s5_readout(u, xr_blocks, xi_blocks, cre_ref, cim_ref, d, wglu, bglu):
    ys = []
    for j in range(N_SSM_BLOCKS):
        ys.append(_dotw(xr_blocks[j], cre_ref[j]) - _dotw(xi_blocks[j], cim_ref[j]))
    y = jnp.concatenate(ys, axis=1) + d * u
    g = _gelu(y)
    return g * _sigmoid(_dotw(g, wglu) + bglu)


def _s5_prompt_kernel(x_ref, w_ref, bbr_ref, bbi_ref, lvr_ref, lvi_ref, pwr_ref, pwi_ref,
                      cre_ref, cim_ref, d_ref, wglu_ref, bglu_ref, yb_ref, sre_ref, sim_ref,
                      sr_scr, si_scr, *, tb):
    step = pl.program_id(0)

    @pl.when(step == 0)
    def _():
        sr_scr[...] = jnp.zeros_like(sr_scr)
        si_scr[...] = jnp.zeros_like(si_scr)

    u = _dotw(x_ref[...], w_ref[...])
    xr_blocks, xi_blocks = [], []
    for j in range(N_SSM_BLOCKS):
        cols = slice(j * STATE_BLOCK, (j + 1) * STATE_BLOCK)
        uj = u[:, j * LANES:(j + 1) * LANES]
        br = _dotw(uj, bbr_ref[j])
        bi = _dotw(uj, bbi_ref[j])
        xr, xi = _scan_rows_const(lvr_ref[:, :, cols], lvi_ref[:, :, cols], pwr_ref[:, cols],
                                  pwi_ref[:, cols], br, bi, sr_scr[0:1, cols], si_scr[0:1, cols])
        sr_scr[:, cols] = jnp.broadcast_to(xr[tb - 1:tb, :], (8, STATE_BLOCK))
        si_scr[:, cols] = jnp.broadcast_to(xi[tb - 1:tb, :], (8, STATE_BLOCK))
        xr_blocks.append(xr)
        xi_blocks.append(xi)
    sre_ref[...] = sr_scr[...]
    sim_ref[...] = si_scr[...]
    yb_ref[...] = _s5_readout(u, xr_blocks, xi_blocks, cre_ref, cim_ref, d_ref[...],
                              wglu_ref[...], bglu_ref[...])


def _s5_prompt(x, w_b, bbr, bbi, lvl_re, lvl_im, pow_re, pow_im, cre, cim, d, wglu, bglu, tb):
    t = x.shape[0]
    full = lambda shape: pl.BlockSpec(shape, lambda i: (0,) * len(shape))
    return pl.pallas_call(
        functools.partial(_s5_prompt_kernel, tb=tb),
        grid=(t // tb,),
        in_specs=[pl.BlockSpec((tb, D_MODEL), lambda i: (i, 0)),
                  full(w_b.shape), full(bbr.shape), full(bbi.shape), full(lvl_re.shape),
                  full(lvl_im.shape), full(pow_re.shape), full(pow_im.shape), full(cre.shape),
                  full(cim.shape), full(d.shape), full(wglu.shape), full(bglu.shape)],
        out_specs=[pl.BlockSpec((tb, D_SSM), lambda i: (i, 0)),
                   pl.BlockSpec((8, N_STATE), lambda i: (0, 0)),
                   pl.BlockSpec((8, N_STATE), lambda i: (0, 0))],
        out_shape=[jax.ShapeDtypeStruct((t, D_SSM), F32),
                   jax.ShapeDtypeStruct((8, N_STATE), F32),
                   jax.ShapeDtypeStruct((8, N_STATE), F32)],
        scratch_shapes=[pltpu.VMEM((8, N_STATE), F32), pltpu.VMEM((8, N_STATE), F32)],
        compiler_params=_cparams(),
        name="s5_prompt",
    )(x, w_b, bbr, bbi, lvl_re, lvl_im, pow_re, pow_im, cre, cim, d, wglu, bglu)


def _out_ln_kernel(*refs, n_in):
    a_refs = refs[:n_in]
    w_refs = refs[n_in:2 * n_in]
    x_ref, g_ref, b_ref, o_ref = refs[2 * n_in:]
    y = ALPHA * x_ref[...]
    for a_ref, w_ref in zip(a_refs, w_refs):
        y = y + _dotw(a_ref[...], w_ref[...])
    o_ref[...] = _layer_norm(y, g_ref[...], b_ref[...])


def _out_ln(acts, weights, x, g, b, tm, name):
    m = x.shape[0]
    n_in = len(acts)
    full = lambda shape: pl.BlockSpec(shape, lambda i: (0,) * len(shape))
    in_specs = ([pl.BlockSpec((tm, a.shape[1]), lambda i: (i, 0)) for a in acts]
                + [full(w.shape) for w in weights]
                + [pl.BlockSpec((tm, D_MODEL), lambda i: (i, 0)), full(g.shape), full(b.shape)])
    return pl.pallas_call(
        functools.partial(_out_ln_kernel, n_in=n_in),
        grid=(m // tm,),
        in_specs=in_specs,
        out_specs=pl.BlockSpec((tm, D_MODEL), lambda i: (i, 0)),
        out_shape=jax.ShapeDtypeStruct((m, D_MODEL), F32),
        compiler_params=_cparams(),
        name=name,
    )(*acts, *weights, x, g, b)


def _ffn_prompt_kernel(x_ref, wup_ref, wgate_ref, cw_ref, cb_ref, wdown_ref, g_ref, b_ref,
                       o_ref, tail_ref, h_scr, carry_scr, *, tm):
    step = pl.program_id(0)

    @pl.when(step == 0)
    def _():
        carry_scr[...] = jnp.zeros_like(carry_scr)

    x = x_ref[...]
    xb = x.astype(BF16)
    row = lax.broadcasted_iota(jnp.int32, (tm, FF_CHUNK), 0)
    for c in range(N_FF_CHUNKS):
        cols = slice(c * FF_CHUNK, (c + 1) * FF_CHUNK)
        up = jnp.dot(xb, wup_ref[:, cols], preferred_element_type=F32)
        gate = jnp.dot(xb, wgate_ref[:, cols], preferred_element_type=F32)
        prev1 = carry_scr[7:8, cols]
        prev2 = carry_scr[6:7, cols]
        up1 = jnp.where(row >= 1, pltpu.roll(up, 1, 0), prev1)
        up2 = jnp.where(row >= 2, pltpu.roll(up, 2, 0), jnp.where(row == 1, prev1, prev2))
        hc = (cb_ref[:, cols] + up2 * cw_ref[0:1, cols] + up1 * cw_ref[1:2, cols]
              + up * cw_ref[2:3, cols])
        h_scr[:, cols] = (_gelu(hc) * gate).astype(BF16)
        carry_scr[:, cols] = up[tm - 8:tm, :]
    tail_ref[...] = carry_scr[...]
    f = jnp.dot(h_scr[...], wdown_ref[...], preferred_element_type=F32)
    o_ref[...] = _layer_norm(ALPHA * x + f, g_ref[...], b_ref[...])


def _ffn_prompt(x, wup, wgate, cw, cb, wdown, g, b, tm):
    t = x.shape[0]
    const = lambda shape: pl.BlockSpec(shape, lambda i: (0,) * len(shape),
                                       pipeline_mode=pl.Buffered(1))
    return pl.pallas_call(
        functools.partial(_ffn_prompt_kernel, tm=tm),
        grid=(t // tm,),
        in_specs=[pl.BlockSpec((tm, D_MODEL), lambda i: (i, 0)),
                  const(wup.shape), const(wgate.shape), const(cw.shape), const(cb.shape),
                  const(wdown.shape), const(g.shape), const(b.shape)],
        out_specs=[pl.BlockSpec((tm, D_MODEL), lambda i: (i, 0)),
                   pl.BlockSpec((8, D_FF), lambda i: (0, 0))],
        out_shape=[jax.ShapeDtypeStruct((t, D_MODEL), F32),
                   jax.ShapeDtypeStruct((8, D_FF), F32)],
        scratch_shapes=[pltpu.VMEM((tm, D_FF), BF16), pltpu.VMEM((8, D_FF), F32)],
        compiler_params=_cparams(),
        name="ffn_prompt",
    )(x, wup, wgate, cw, cb, wdown, g, b)


def _aug_lhs(f3):
    h1, h2, h3 = (h.astype(F32) for h in _split3(f3))
    lane = lax.broadcasted_iota(jnp.int32, f3.shape, 1)
    pieces = jnp.where(lane < N_HEADS, h1, jnp.where(lane < 2 * N_HEADS, h2, jnp.where(
        lane < 3 * N_HEADS, h3, jnp.where(lane < 4 * N_HEADS, 1.0, 0.0))))
    return pieces.astype(BF16)


def _qkv_prompt_kernel(x_ref, w_ref, wf_ref, bf_ref, selq_ref, selk_ref,
                       k_ref, v_ref, lf_ref, qa_ref, ka_ref, va_ref, f_scr, *, tm):
    step = pl.program_id(0)

    @pl.when(step == 0)
    def _():
        f_scr[...] = jnp.zeros_like(f_scr)

    x = x_ref[...]
    qkv = _dotw(x, w_ref[...])
    k = qkv[:, D_MODEL:2 * D_MODEL]
    v = qkv[:, 2 * D_MODEL:]
    k_ref[...] = k
    v_ref[...] = v
    logf3 = _log_sigmoid(_dotw(x, wf_ref[...]) + bf_ref[...])
    lf_ref[...] = logf3

    r_i = lax.broadcasted_iota(jnp.int32, (tm, tm), 0)
    c_i = lax.broadcasted_iota(jnp.int32, (tm, tm), 1)
    tri = jnp.where(r_i >= c_i, 1.0, 0.0).astype(BF16)
    f3 = _dot_exact01(tri, logf3) + f_scr[0:1, :]
    f_scr[...] = jnp.broadcast_to(f3[tm - 1:tm, :], f_scr.shape)

    lhs = _aug_lhs(f3 * LOG2E)
    augq = jnp.dot(lhs, selq_ref[...], preferred_element_type=F32)
    augk = jnp.dot(lhs, selk_ref[...], preferred_element_type=F32)
    lane = lax.broadcasted_iota(jnp.int32, (tm, LANES), 1)
    low = lane < HEAD_DIM
    vtail = jnp.where(lane == HEAD_DIM, 1.0, 0.0)
    scale = HEAD_DIM ** -0.5 * LOG2E
    for hp in range(N_HEADS // 2):
        cols = slice(hp * LANES, (hp + 1) * LANES)
        qb = qkv[:, cols] * scale
        kb = k[:, cols]
        vb = v[:, cols]
        for odd in range(2):
            h = 2 * hp + odd
            acols = slice(h * LANES, (h + 1) * LANES)
            if odd:
                qb, kb, vb = (pltpu.roll(qb, HEAD_DIM, 1), pltpu.roll(kb, HEAD_DIM, 1),
                              pltpu.roll(vb, HEAD_DIM, 1))
            qa_ref[h] = jnp.where(low, qb, augq[:, acols]).astype(BF16)
            ka_ref[h] = jnp.where(low, kb, augk[:, acols]).astype(BF16)
            va_ref[h] = jnp.where(low, vb, vtail).astype(BF16)


def _qkv_prompt(x, w_qkv, wf3, bf3, selq, selk, tm):
    t = x.shape[0]
    full = lambda shape: pl.BlockSpec(shape, lambda i: (0,) * len(shape))
    head_major = jax.ShapeDtypeStruct((N_HEADS, t, LANES), BF16)
    head_spec = pl.BlockSpec((N_HEADS, tm, LANES), lambda i: (0, i, 0))
    return pl.pallas_call(
        functools.partial(_qkv_prompt_kernel, tm=tm),
        grid=(t // tm,),
        in_specs=[pl.BlockSpec((tm, D_MODEL), lambda i: (i, 0)),
                  full(w_qkv.shape), full(wf3.shape), full(bf3.shape), full(selq.shape),
                  full(selk.shape)],
        out_specs=[pl.BlockSpec((tm, D_MODEL), lambda i: (i, 0)),
                   pl.BlockSpec((tm, D_MODEL), lambda i: (i, 0)),
                   pl.BlockSpec((tm, LANES), lambda i: (i, 0)),
                   head_spec, head_spec, head_spec],
        out_shape=[jax.ShapeDtypeStruct((t, D_MODEL), F32),
                   jax.ShapeDtypeStruct((t, D_MODEL), F32),
                   jax.ShapeDtypeStruct((t, LANES), F32),
                   head_major, head_major, head_major],
        scratch_shapes=[pltpu.VMEM((8, LANES), F32)],
        compiler_params=_cparams(),
        name="qkv_prompt",
    )(x, w_qkv, wf3, bf3, selq, selk)


def _flash_kernel(it_ref, jt_ref, q_ref, k_ref, v_ref, o_ref, m_scr, acc_scr, *, tq, tk, qc):
    s_idx = pl.program_id(1)
    i = it_ref[s_idx]
    j = jt_ref[s_idx]

    @pl.when(j == 0)
    def _():
        m_scr[...] = jnp.full(m_scr.shape, NEG_BIG, F32)
        acc_scr[...] = jnp.zeros_like(acc_scr)

    def accumulate(masked):
        chunks = [(hh, c) for hh in range(2) for c in range(tq // qc)]

        def scores(hh, c):
            rows = slice(c * qc, (c + 1) * qc)
            s = lax.dot_general(q_ref[hh, rows, :], k_ref[hh], (((1,), (1,)), ((), ())),
                                preferred_element_type=F32)
            if masked:
                qpos = i * tq + c * qc + lax.broadcasted_iota(jnp.int32, (qc, tk), 0)
                kpos = j * tk + lax.broadcasted_iota(jnp.int32, (qc, tk), 1)
                s = jnp.where(kpos <= qpos, s, NEG_BIG)
            return s

        ahead = 2
        pending = [scores(*chunks[n]) for n in range(ahead)]
        for n, (hh, c) in enumerate(chunks):
            s = pending.pop(0)
            if n + ahead < len(chunks):
                pending.append(scores(*chunks[n + ahead]))
            rows = slice(c * qc, (c + 1) * qc)
            m_prev = m_scr[hh, rows, :]
            m_next = jnp.maximum(m_prev, jnp.max(s, axis=1, keepdims=True))
            alpha = jnp.exp2(m_prev - m_next)
            p = jnp.exp2(s - m_next[:, 0:1]).astype(BF16)
            acc_scr[hh, rows, :] = alpha * acc_scr[hh, rows, :] + jnp.dot(
                p, v_ref[hh], preferred_element_type=F32)
            m_scr[hh, rows, :] = m_next

    crosses_diagonal = (j + 1) * tk - 1 > i * tq

    @pl.when(crosses_diagonal)
    def _():
        accumulate(True)

    @pl.when(jnp.logical_not(crosses_diagonal))
    def _():
        accumulate(False)

    @pl.when(j == ((i + 1) * tq - 1) // tk)
    def _():
        lane = lax.broadcasted_iota(jnp.int32, (tq, LANES), 1)
        outs = []
        for hh in range(2):
            acc = acc_scr[hh]
            outs.append(acc / acc[:, HEAD_DIM:HEAD_DIM + 1])
        o_ref[...] = jnp.where(lane < HEAD_DIM, outs[0], pltpu.roll(outs[1], HEAD_DIM, 1))


def _flash(qa, ka, va, tq, tk, qc):
    t = ka.shape[1]
    nq = t // tq
    i_list, j_list = [], []
    for i in range(nq):
        for j in range(((i + 1) * tq - 1) // tk + 1):
            i_list.append(i)
            j_list.append(j)
    it = jnp.asarray(i_list, jnp.int32)
    jt = jnp.asarray(j_list, jnp.int32)
    grid_spec = pltpu.PrefetchScalarGridSpec(
        num_scalar_prefetch=2,
        grid=(N_HEADS // 2, len(i_list)),
        in_specs=[pl.BlockSpec((2, tq, LANES), lambda hp, s, it, jt: (hp, it[s], 0)),
                  pl.BlockSpec((2, tk, LANES), lambda hp, s, it, jt: (hp, jt[s], 0)),
                  pl.BlockSpec((2, tk, LANES), lambda hp, s, it, jt: (hp, jt[s], 0))],
        out_specs=pl.BlockSpec((tq, LANES), lambda hp, s, it, jt: (it[s], hp)),
        scratch_shapes=[pltpu.VMEM((2, tq, LANES), F32), pltpu.VMEM((2, tq, LANES), F32)],
    )
    return pl.pallas_call(
        functools.partial(_flash_kernel, tq=tq, tk=tk, qc=qc),
        grid_spec=grid_spec,
        out_shape=jax.ShapeDtypeStruct((t, D_MODEL), F32),
        compiler_params=_cparams(2),
        name="flash_prompt",
    )(it, jt, qa, ka, va)


def _mix0_sample_kernel(x_ref, w_ref, c0_ref, c1_ref, c2_ref, cw_ref, cb_ref, wa_ref, ba_ref,
                        wx_ref, bx_ref, cl_ref, h0_ref, sr_ref, si_ref, bbr_ref, bbi_ref,
                        ar_ref, ai_ref, cre_ref, cim_ref, d_ref, wglu_ref, bglu_ref,
                        ya_ref, yb_ref, xa_ref, h_ref, xr_ref, xi_ref):
    proj = _dotw(x_ref[...], w_ref[...])
    xa = proj[:, :D_RNN]
    ga = proj[:, D_RNN:2 * D_RNN]
    u = proj[:, 2 * D_RNN:]
    xa_ref[...] = xa
    xc = (cb_ref[...] + c0_ref[...] * cw_ref[0:1, :] + c1_ref[...] * cw_ref[1:2, :]
          + c2_ref[...] * cw_ref[2:3, :] + xa * cw_ref[3:4, :])
    a, inp = _rglru_gates(xc, wa_ref[...], ba_ref[...], wx_ref[...], bx_ref[...], cl_ref[...])
    h = a * h0_ref[...] + inp
    h_ref[...] = h
    ya_ref[...] = h * _gelu(ga)

    xr_blocks, xi_blocks = [], []
    for j in range(N_SSM_BLOCKS):
        cols = slice(j * STATE_BLOCK, (j + 1) * STATE_BLOCK)
        uj = u[:, j * LANES:(j + 1) * LANES]
        ar = ar_ref[:, cols]
        ai = ai_ref[:, cols]
        pr = sr_ref[:, cols]
        pi = si_ref[:, cols]
        xr = _dotw(uj, bbr_ref[j]) + (ar * pr - ai * pi)
        xi = _dotw(uj, bbi_ref[j]) + (ar * pi + ai * pr)
        xr_ref[:, cols] = xr
        xi_ref[:, cols] = xi
        xr_blocks.append(xr)
        xi_blocks.append(xi)
    yb_ref[...] = _s5_readout(u, xr_blocks, xi_blocks, cre_ref, cim_ref, d_ref[...],
                              wglu_ref[...], bglu_ref[...])


def _mix0_sample(*args):
    n = args[0].shape[0]
    shapes = [(n, D_RNN), (n, D_SSM), (n, D_RNN), (n, D_RNN), (n, N_STATE), (n, N_STATE)]
    return pl.pallas_call(
        _mix0_sample_kernel,
        out_shape=[jax.ShapeDtypeStruct(s, F32) for s in shapes],
        compiler_params=pltpu.CompilerParams(vmem_limit_bytes=VMEM_LIMIT),
        name="mix0_sample",
    )(*args)


def _ffn_sample_kernel(x_ref, wup_ref, wgate_ref, b0_ref, b1_ref, cw_ref, cb_ref, wdown_ref,
                       g_ref, b_ref, o_ref, up_ref, acc_scr):
    c = pl.program_id(0)

    @pl.when(c == 0)
    def _():
        acc_scr[...] = jnp.zeros_like(acc_scr)

    x = x_ref[...]
    up = _dotw(x, wup_ref[...])
    gate = _dotw(x, wgate_ref[...])
    up_ref[...] = up
    hc = (cb_ref[...] + b0_ref[...] * cw_ref[0:1, :] + b1_ref[...] * cw_ref[1:2, :]
          + up * cw_ref[2:3, :])
    acc_scr[...] += _dotw(_gelu(hc) * gate, wdown_ref[...])

    @pl.when(c == pl.num_programs(0) - 1)
    def _():
        o_ref[...] = _layer_norm(ALPHA * x + acc_scr[...], g_ref[...], b_ref[...])


def _ffn_sample(x, wup, wgate, buf0, buf1, cw, cb, wdown, g, b):
    n = x.shape[0]
    fc = FF_CHUNK
    col = lambda rows: pl.BlockSpec((rows, fc), lambda c: (0, c))
    full = lambda shape: pl.BlockSpec(shape, lambda c: (0,) * len(shape))
    return pl.pallas_call(
        _ffn_sample_kernel,
        grid=(D_FF // fc,),
        in_specs=[full(x.shape), col(D_MODEL), col(D_MODEL), col(n), col(n), col(cw.shape[0]),
                  col(1), pl.BlockSpec((fc, D_MODEL), lambda c: (c, 0)), full(g.shape),
                  full(b.shape)],
        out_specs=[full((n, D_MODEL)), col(n)],
        out_shape=[jax.ShapeDtypeStruct((n, D_MODEL), F32), jax.ShapeDtypeStruct((n, D_FF), F32)],
        scratch_shapes=[pltpu.VMEM((n, D_MODEL), F32)],
        compiler_params=_cparams(),
        name="ffn_sample",
    )(x, wup, wgate, buf0, buf1, cw, cb, wdown, g, b)


def _qkv_sample_kernel(x_ref, w_ref, wf_ref, bf_ref, o_ref, lf_ref):
    x = x_ref[...]
    o_ref[...] = _dotw(x, w_ref[...])

    @pl.when(pl.program_id(0) == 0)
    def _():
        lf_ref[...] = _log_sigmoid(_dotw(x, wf_ref[...]) + bf_ref[...])


def _qkv_sample(x, w_in1, wf, bf):
    n = x.shape[0]
    full = lambda shape: pl.BlockSpec(shape, lambda c: (0,) * len(shape))
    return pl.pallas_call(
        _qkv_sample_kernel,
        grid=(3,),
        in_specs=[full(x.shape), pl.BlockSpec((D_MODEL, D_MODEL), lambda c: (0, c)),
                  full(wf.shape), full(bf.shape)],
        out_specs=[pl.BlockSpec((n, D_MODEL), lambda c: (0, c)), full((n, LANES))],
        out_shape=[jax.ShapeDtypeStruct((n, 3 * D_MODEL), F32),
                   jax.ShapeDtypeStruct((n, LANES), F32)],
        compiler_params=_cparams(),
        name="qkv_sample",
    )(x, w_in1, wf, bf)


def _attn_sample_kernel(pt_ref, q_ref, kn_ref, vn_ref, gt_ref, *refs, pages_per_step):
    pp = pages_per_step
    k_refs = refs[:pp]
    v_refs = refs[pp:2 * pp]
    lf_refs = refs[2 * pp:3 * pp]
    o_ref, m_scr, l_scr, acc_scr, carry_scr = refs[3 * pp:]
    s_idx = pl.program_id(1)

    @pl.when(s_idx == 0)
    def _():
        m_scr[...] = jnp.full(m_scr.shape, NEG_BIG, F32)
        l_scr[...] = jnp.zeros_like(l_scr)
        acc_scr[...] = jnp.zeros_like(acc_scr)
        carry_scr[...] = jnp.zeros_like(carry_scr)

    hrow = lax.broadcasted_iota(jnp.int32, (N_HEADS, D_MODEL), 0)
    hcol = lax.broadcasted_iota(jnp.int32, (N_HEADS, D_MODEL), 1) // HEAD_DIM
    diag = hrow == hcol
    scale = HEAD_DIM ** -0.5
    qbd = jnp.where(diag, q_ref[0] * scale, 0.0)
    qbd_b = qbd.astype(BF16)
    r_i = lax.broadcasted_iota(jnp.int32, (PAGE_SIZE, PAGE_SIZE), 0)
    c_i = lax.broadcasted_iota(jnp.int32, (PAGE_SIZE, PAGE_SIZE), 1)
    later = jnp.where(r_i > c_i, 1.0, 0.0).astype(BF16)
    gt = gt_ref[0]

    sts, vts, sufs, tots = [], [], [], []
    for r in range(pp):
        kt = k_refs[r][0].reshape(D_MODEL, PAGE_SIZE).astype(BF16)
        vts.append(v_refs[r][0].reshape(D_MODEL, PAGE_SIZE).astype(BF16))
        sts.append(jnp.dot(qbd_b, kt, preferred_element_type=F32))
        lf = lf_refs[r][0]
        sufs.append(sum(jnp.dot(pc, later, preferred_element_type=F32) for pc in _split3(lf)))
        tots.append(jnp.sum(lf, axis=1, keepdims=True))

    carry = carry_scr[:, 0:1]
    logits = []
    for r in range(pp):
        logits.append(sts[r] + sufs[r] + (carry + gt))
        carry = carry + tots[r]
    carry_scr[...] = jnp.broadcast_to(carry, carry_scr.shape)

    mx = logits[0]
    for lg in logits[1:]:
        mx = jnp.maximum(mx, lg)
    m_prev = m_scr[:, 0:1]
    m_next = jnp.maximum(m_prev, jnp.max(mx, axis=1, keepdims=True))
    alpha = jnp.exp(m_prev - m_next)
    ps = [jnp.exp(lg - m_next) for lg in logits]
    psum = ps[0]
    for pblk in ps[1:]:
        psum = psum + pblk
    l_scr[...] = jnp.broadcast_to(
        alpha * l_scr[:, 0:1] + jnp.sum(psum, axis=1, keepdims=True), l_scr.shape)
    acc = alpha * acc_scr[...]
    for r in range(pp):
        acc = acc + lax.dot_general(ps[r].astype(BF16), vts[r], (((1,), (1,)), ((), ())),
                                    preferred_element_type=F32)
    acc_scr[...] = acc
    m_scr[...] = jnp.broadcast_to(m_next, m_scr.shape)

    @pl.when(s_idx == pl.num_programs(1) - 1)
    def _():
        s_self = jnp.sum(qbd * kn_ref[0], axis=1, keepdims=True)
        m_prev = m_scr[:, 0:1]
        m_fin = jnp.maximum(m_prev, s_self)
        a_past = jnp.exp(m_prev - m_fin)
        a_self = jnp.exp(s_self - m_fin)
        l_fin = a_past * l_scr[:, 0:1] + a_self
        acc = (a_past * acc_scr[...] + a_self * vn_ref[0]) / l_fin
        o_ref[0] = jnp.sum(jnp.where(diag, acc, 0.0), axis=0, keepdims=True)


def _attn_sample(page_table, q, k_new, v_new, logf_new, cache_k, cache_v, cache_logf,
                 pages_per_step):
    n_seq, n_pages = page_table.shape
    pp = pages_per_step
    pt = page_table.reshape(-1)
    ck = jnp.transpose(cache_k, (0, 2, 3, 1))
    cv = jnp.transpose(cache_v, (0, 2, 3, 1))
    clf = jnp.transpose(cache_logf, (0, 2, 1))
    row = lambda a: a.reshape(n_seq, 1, D_MODEL)

    def page_map(r, ndim):
        return lambda b, s, pt: (pt[b * n_pages + (n_pages - 1 - (s * pp + r))],) + (0,) * ndim

    per_seq = lambda: pl.BlockSpec((1, 1, D_MODEL), lambda b, s, pt: (b, 0, 0))
    in_specs = [per_seq(), per_seq(), per_seq(),
                pl.BlockSpec((1, N_HEADS, 1), lambda b, s, pt: (b, 0, 0))]
    kv_block = (1, N_HEADS, HEAD_DIM, PAGE_SIZE)
    in_specs += [pl.BlockSpec(kv_block, page_map(r, 3)) for r in range(pp)]
    in_specs += [pl.BlockSpec(kv_block, page_map(r, 3)) for r in range(pp)]
    in_specs += [pl.BlockSpec((1, N_HEADS, PAGE_SIZE), page_map(r, 2)) for r in range(pp)]
    grid_spec = pltpu.PrefetchScalarGridSpec(
        num_scalar_prefetch=1,
        grid=(n_seq, n_pages // pp),
        in_specs=in_specs,
        out_specs=per_seq(),
        scratch_shapes=[pltpu.VMEM((N_HEADS, LANES), F32), pltpu.VMEM((N_HEADS, LANES), F32),
                        pltpu.VMEM((N_HEADS, D_MODEL), F32), pltpu.VMEM((N_HEADS, LANES), F32)],
    )
    out = pl.pallas_call(
        functools.partial(_attn_sample_kernel, pages_per_step=pp),
        grid_spec=grid_spec,
        out_shape=jax.ShapeDtypeStruct((n_seq, 1, D_MODEL), F32),
        compiler_params=_cparams(2),
        name="attn_sample",
    )(pt, row(q), row(k_new), row(v_new), logf_new.reshape(n_seq, N_HEADS, 1),
      *([ck] * pp), *([cv] * pp), *([clf] * pp))
    return out.reshape(n_seq, D_MODEL)


def _block_diag_dense(w):
    nb, n, _ = w.shape
    eye = jnp.eye(nb, dtype=w.dtype)
    return jnp.einsum('hij,hk->hikj', w, eye).reshape(nb * n, nb * n)


def _s5_discretize(lam_re, lam_im, log_dt, b_re, b_im):
    lr = jnp.minimum(lam_re, -1e-4)
    li = lam_im
    dt = jnp.exp(log_dt)[:, None]
    mag = jnp.exp(lr * dt)
    ab_re = mag * jnp.cos(li * dt)
    ab_im = mag * jnp.sin(li * dt)
    den = lr * lr + li * li
    nr = ab_re - 1.0
    zr = (nr * lr + ab_im * li) / den
    zi = (ab_im * lr - nr * li) / den
    bb_re = zr[..., None] * b_re - zi[..., None] * b_im
    bb_im = zr[..., None] * b_im + zi[..., None] * b_re
    return ab_re, ab_im, bb_re, bb_im


def _s5_scan_tables(a_re, a_im):
    ar = a_re.reshape(1, N_STATE)
    ai = a_im.reshape(1, N_STATE)
    pows = [(ar, ai)]
    for _ in range(7):
        qr, qi = pows[-1]
        pows.append((qr * ar - qi * ai, qr * ai + qi * ar))
    pow_re = jnp.concatenate([q[0] for q in pows], axis=0)
    pow_im = jnp.concatenate([q[1] for q in pows], axis=0)
    sub = jnp.arange(8)[:, None]
    lvl_re = jnp.stack([jnp.where(sub >= d, pows[d - 1][0], 0.0) for d in (1, 2, 4)])
    lvl_im = jnp.stack([jnp.where(sub >= d, pows[d - 1][1], 0.0) for d in (1, 2, 4)])
    return dict(lvl_re=lvl_re, lvl_im=lvl_im, pow_re=pow_re, pow_im=pow_im)


def _s5_in_blocks(bb):
    eye = jnp.eye(GROUPS_PER_BLOCK, dtype=bb.dtype)
    b4 = bb.transpose(0, 2, 1).reshape(N_SSM_BLOCKS, GROUPS_PER_BLOCK, S5_GROUP, S5_STATE)
    return jnp.einsum('jgmp,gh->jgmhp', b4, eye).reshape(N_SSM_BLOCKS, LANES, STATE_BLOCK)


def _s5_out_blocks(c):
    eye = jnp.eye(GROUPS_PER_BLOCK, dtype=c.dtype)
    c4 = c.reshape(N_SSM_BLOCKS, GROUPS_PER_BLOCK, S5_GROUP, S5_STATE)
    return jnp.einsum('jgmp,gh->jgphm', c4, eye).reshape(N_SSM_BLOCKS, STATE_BLOCK, LANES)


def _aug_selectors():
    rows = jnp.arange(LANES)[:, None]
    cols = jnp.arange(N_HEADS * LANES)[None, :]
    head = cols // LANES
    c = cols % LANES - HEAD_DIM
    piece = rows // N_HEADS
    rhead = rows % N_HEADS
    same = (rhead == head) & (rows < 4 * N_HEADS)
    f_piece = same & (piece < 3)
    ones_piece = same & (piece == 3)
    selq = jnp.where(f_piece & (c == piece), 1.0, 0.0) + jnp.where(
        ones_piece & (c >= 3) & (c < AUG_COLS), 1.0, 0.0)
    selk = jnp.where(ones_piece & (c >= 0) & (c < 3), 1.0, 0.0) - jnp.where(
        f_piece & (c == piece + 3), 1.0, 0.0)
    return selq.astype(BF16), selk.astype(BF16)


def _forget_cols(w_in1, b_fgt, dtype):
    wf = w_in1[:, 3 * D_MODEL:]
    pad = jnp.zeros((D_MODEL, LANES - 3 * N_HEADS), w_in1.dtype)
    wf3 = jnp.concatenate([wf, wf, wf, pad], axis=1).astype(dtype)
    bf3 = jnp.concatenate([b_fgt, b_fgt, b_fgt, jnp.zeros((LANES - 3 * N_HEADS,), F32)])[None, :]
    return wf3, bf3


def _row(v):
    return v.reshape(1, -1)


def _pick(t, pref):
    return pref if t % pref == 0 else t


def _prompt_path(x, p, s5p, tiles):
    t = x.shape[0]
    bf = lambda w: w.astype(BF16)
    tb = _pick(t, tiles['scan'])
    ya, tail, h_last = _rglru_prompt(
        x, bf(p['w_in0'][:, :2 * D_RNN]), p['rg_conv_w'], _row(p['rg_conv_b']),
        bf(s5p['wa_bd']), _row(p['rg_ba']), bf(s5p['wx_bd']), _row(p['rg_bx']), s5p['c_lam'], tb)
    yb, s_re, s_im = _s5_prompt(
        x, bf(p['w_in0'][:, 2 * D_RNN:]), bf(s5p['bbr']), bf(s5p['bbi']), s5p['lvl_re'],
        s5p['lvl_im'], s5p['pow_re'], s5p['pow_im'],
        bf(s5p['cre']), bf(s5p['cim']), _row(p['s5_d']), bf(p['s5_w_glu']), _row(p['s5_b_glu']), tb)
    tm = _pick(t, tiles['rows'])
    x1 = _out_ln([ya, yb], [bf(p['w_out0'][:D_RNN]), bf(p['w_out0'][D_RNN:])], x,
                 p['ln_g'][0, 0][None], p['ln_b'][0, 0][None], tm, "out0_prompt")
    x2, ffn_tail0 = _ffn_prompt(x1, bf(p['ffn_w_up'][0]), bf(p['ffn_w_gate'][0]), p['ffn_conv_w'][0],
                                p['ffn_conv_b'][0][None], bf(p['ffn_w_down'][0]),
                                p['ln_g'][0, 1][None], p['ln_b'][0, 1][None], tm)
    wf3, bf3 = _forget_cols(p['w_in1'], p['b_fgt'], BF16)
    selq, selk = _aug_selectors()
    k, v, lf, qa, ka, va = _qkv_prompt(x2, bf(p['w_in1'][:, :3 * D_MODEL]), wf3, bf3, selq, selk,
                                       _pick(t, tiles['qkv']))
    o = _flash(qa, ka, va, _pick(t, tiles['tq']), _pick(t, tiles['tk']), tiles['qc'])
    x3 = _out_ln([o], [bf(p['w_out1'])], x2, p['ln_g'][1, 0][None], p['ln_b'][1, 0][None], tm,
                 "out1_prompt")
    x4, ffn_tail1 = _ffn_prompt(x3, bf(p['ffn_w_up'][1]), bf(p['ffn_w_gate'][1]), p['ffn_conv_w'][1],
                                p['ffn_conv_b'][1][None], bf(p['ffn_w_down'][1]),
                                p['ln_g'][1, 1][None], p['ln_b'][1, 1][None], tm)
    return (x4[None], tail[None, 5:8], h_last[0:1], s_re[0:1].reshape(1, S5_GROUPS, S5_STATE),
            s_im[0:1].reshape(1, S5_GROUPS, S5_STATE), k.reshape(1, t, N_HEADS, HEAD_DIM),
            v.reshape(1, t, N_HEADS, HEAD_DIM), lf[None, :, :N_HEADS],
            jnp.stack([ffn_tail0[None, 6:8], ffn_tail1[None, 6:8]], axis=0))


def _sample_path(x, rg_conv, rg_h, s5_re, s5_im, cache_k, cache_v, cache_logf, ffn_conv,
                 page_table, p, s5p, pages_per_step):
    n = x.shape[0]
    ya, yb, xa, h_new, xr, xi = _mix0_sample(
        x, p['w_in0'], rg_conv[:, 0], rg_conv[:, 1], rg_conv[:, 2], p['rg_conv_w'],
        _row(p['rg_conv_b']), s5p['wa_bd'], _row(p['rg_ba']), s5p['wx_bd'], _row(p['rg_bx']),
        s5p['c_lam'], rg_h, s5_re.reshape(n, N_STATE), s5_im.reshape(n, N_STATE),
        s5p['bbr'], s5p['bbi'], s5p['a_re'], s5p['a_im'], s5p['cre'], s5p['cim'],
        _row(p['s5_d']), p['s5_w_glu'], _row(p['s5_b_glu']))
    x1 = _out_ln([ya, yb], [p['w_out0'][:D_RNN], p['w_out0'][D_RNN:]], x,
                 p['ln_g'][0, 0][None], p['ln_b'][0, 0][None], n, "out0_sample")
    x2, up0 = _ffn_sample(x1, p['ffn_w_up'][0], p['ffn_w_gate'][0], ffn_conv[0, :, 0],
                          ffn_conv[0, :, 1], p['ffn_conv_w'][0], p['ffn_conv_b'][0][None],
                          p['ffn_w_down'][0], p['ln_g'][0, 1][None], p['ln_b'][0, 1][None])
    wf3, bf3 = _forget_cols(p['w_in1'], p['b_fgt'], F32)
    qkv, lf3 = _qkv_sample(x2, p['w_in1'], wf3, bf3)
    q = qkv[:, :D_MODEL]
    k = qkv[:, D_MODEL:2 * D_MODEL]
    v = qkv[:, 2 * D_MODEL:]
    logf = lf3[:, :N_HEADS]
    o = _attn_sample(page_table, q, k, v, logf, cache_k, cache_v, cache_logf, pages_per_step)
    x3 = _out_ln([o], [p['w_out1']], x2, p['ln_g'][1, 0][None], p['ln_b'][1, 0][None], n,
                 "out1_sample")
    x4, up1 = _ffn_sample(x3, p['ffn_w_up'][1], p['ffn_w_gate'][1], ffn_conv[1, :, 0],
                          ffn_conv[1, :, 1], p['ffn_conv_w'][1], p['ffn_conv_b'][1][None],
                          p['ffn_w_down'][1], p['ln_g'][1, 1][None], p['ln_b'][1, 1][None])
    new_rg_conv = jnp.concatenate([rg_conv[:, 1:], xa[:, None]], axis=1)
    new_ffn = jnp.stack([jnp.stack([ffn_conv[0, :, 1], up0], axis=1),
                         jnp.stack([ffn_conv[1, :, 1], up1], axis=1)], axis=0)
    return (x4[:, None], new_rg_conv, h_new, xr.reshape(n, S5_GROUPS, S5_STATE),
            xi.reshape(n, S5_GROUPS, S5_STATE), k.reshape(n, 1, N_HEADS, HEAD_DIM),
            v.reshape(n, 1, N_HEADS, HEAD_DIM), logf[:, None], new_ffn)


def _derived_params(p):
    a_re, a_im, bb_re, bb_im = _s5_discretize(p['s5_lambda_re'], p['s5_lambda_im'], p['s5_log_dt'],
                                              p['s5_b_re'], p['s5_b_im'])
    return dict(
        wa_bd=_block_diag_dense(p['rg_wa']), wx_bd=_block_diag_dense(p['rg_wx']),
        c_lam=_row(-RG_C * jax.nn.softplus(-p['rg_lambda'])),
        a_re=a_re.reshape(1, N_STATE), a_im=a_im.reshape(1, N_STATE),
        **_s5_scan_tables(a_re, a_im),
        bbr=_s5_in_blocks(bb_re), bbi=_s5_in_blocks(bb_im),
        cre=_s5_out_blocks(p['s5_c_re']), cim=_s5_out_blocks(p['s5_c_im']))


PROMPT_TILES = dict(scan=256, rows=512, qkv=256, tq=1024, tk=1024, qc=256)
PAGES_PER_STEP = 8


def kernel(x_prompt, x_sample, state_rglru_conv, state_rglru_h, state_s5_re, state_s5_im, cache_k, cache_v, cache_logf, state_ffn_conv, page_table, w_in0, rg_conv_w, rg_conv_b, rg_wa, rg_ba, rg_wx, rg_bx, rg_lambda, s5_lambda_re, s5_lambda_im, s5_log_dt, s5_b_re, s5_b_im, s5_c_re, s5_c_im, s5_d, s5_w_glu, s5_b_glu, w_out0, w_in1, b_fgt, w_out1, ffn_w_up, ffn_w_gate, ffn_conv_w, ffn_conv_b, ffn_w_down, ln_g, ln_b):
    p = dict(w_in0=w_in0, rg_conv_w=rg_conv_w, rg_conv_b=rg_conv_b, rg_wa=rg_wa, rg_ba=rg_ba,
             rg_wx=rg_wx, rg_bx=rg_bx, rg_lambda=rg_lambda, s5_lambda_re=s5_lambda_re,
             s5_lambda_im=s5_lambda_im, s5_log_dt=s5_log_dt, s5_b_re=s5_b_re, s5_b_im=s5_b_im,
             s5_c_re=s5_c_re, s5_c_im=s5_c_im, s5_d=s5_d, s5_w_glu=s5_w_glu, s5_b_glu=s5_b_glu,
             w_out0=w_out0, w_in1=w_in1, b_fgt=b_fgt, w_out1=w_out1, ffn_w_up=ffn_w_up,
             ffn_w_gate=ffn_w_gate, ffn_conv_w=ffn_conv_w, ffn_conv_b=ffn_conv_b,
             ffn_w_down=ffn_w_down, ln_g=ln_g, ln_b=ln_b)
    s5p = _derived_params(p)
    prompt = _prompt_path(x_prompt[0], p, s5p, PROMPT_TILES)
    sample = _sample_path(x_sample[:, 0], state_rglru_conv, state_rglru_h, state_s5_re, state_s5_im,
                          cache_k, cache_v, cache_logf, state_ffn_conv, page_table, p, s5p,
                          PAGES_PER_STEP)
    return (prompt[0], sample[0]) + tuple(prompt[1:]) + tuple(sample[1:])
```

```python
import functools

import jax
import jax.numpy as jnp
from jax import lax
from jax.experimental import pallas as pl
from jax.experimental.pallas import tpu as pltpu

F32 = jnp.float32
BF16 = jnp.bfloat16

D_MODEL = 1024
DEPTH = 2
PAGE_SIZE = 128
D_RNN = 512
RG_BLOCKS = 8
RG_BW = D_RNN // RG_BLOCKS
RG_C = 8.0
D_SSM = 512
S5_GROUP = 16
S5_GROUPS = D_SSM // S5_GROUP
S5_STATE = 64
N_STATE = S5_GROUPS * S5_STATE
N_HEADS = 16
HEAD_DIM = D_MODEL // N_HEADS
D_FF = 2816
ALPHA = (2.0 * DEPTH) ** 0.25
LN_EPS = 1e-5

LANES = 128
GROUPS_PER_BLOCK = LANES // S5_GROUP
N_SSM_BLOCKS = D_SSM // LANES
STATE_BLOCK = GROUPS_PER_BLOCK * S5_STATE
FF_CHUNK = 256
N_FF_CHUNKS = D_FF // FF_CHUNK
NEG_BIG = -1e30
LOG2E = 1.4426950408889634
AUG_COLS = 6
VMEM_LIMIT = 56 * 1024 * 1024


def _cparams(n_axes=1, vmem=VMEM_LIMIT):
    return pltpu.CompilerParams(dimension_semantics=("arbitrary",) * n_axes,
                                vmem_limit_bytes=vmem)


def _split3(x):
    h1 = x.astype(BF16)
    r1 = x - h1.astype(F32)
    h2 = r1.astype(BF16)
    h3 = (r1 - h2.astype(F32)).astype(BF16)
    return h1, h2, h3


def _dotw(x, w):
    if w.dtype == BF16:
        return jnp.dot(x.astype(BF16), w, preferred_element_type=F32)
    xh = x.astype(BF16)
    xl = (x - xh.astype(F32)).astype(BF16)
    wh = w.astype(BF16)
    wl = (w - wh.astype(F32)).astype(BF16)
    return (jnp.dot(xh, wh, preferred_element_type=F32)
            + jnp.dot(xl, wh, preferred_element_type=F32)
            + jnp.dot(xh, wl, preferred_element_type=F32))


def _dot_exact01(m01, x):
    h1, h2, h3 = _split3(x)
    return (jnp.dot(m01, h1, preferred_element_type=F32)
            + jnp.dot(m01, h2, preferred_element_type=F32)
            + jnp.dot(m01, h3, preferred_element_type=F32))


def _gelu(x):
    return jax.nn.gelu(x)


def _sigmoid(x):
    return jax.nn.sigmoid(x)


def _log_sigmoid(x):
    return jnp.minimum(x, 0.0) - jnp.log1p(jnp.exp(-jnp.abs(x)))


def _expm1(x):
    u = jnp.exp(x)
    um1 = u - 1.0
    return jnp.where(um1 == 0.0, x, jnp.where(um1 == -1.0, -1.0, um1 * x / jnp.log(u)))


def _layer_norm(xf, g, b):
    mean = jnp.mean(xf, axis=-1, keepdims=True)
    xc = xf - mean
    var = jnp.mean(xc * xc, axis=-1, keepdims=True)
    return xc * lax.rsqrt(var + LN_EPS) * g + b


def _rglru_gates(xc, wa, ba, wx, bx, c_lam):
    r = _sigmoid(_dotw(xc, wa) + ba)
    i = _sigmoid(_dotw(xc, wx) + bx)
    log_a = c_lam * r
    a = jnp.exp(log_a)
    inp = jnp.sqrt(-_expm1(2.0 * log_a)) * (i * xc)
    return a, inp


def _scan_rows(a, b):
    n = a.shape[0]
    row = lax.broadcasted_iota(jnp.int32, a.shape, 0)
    d = 1
    while d < n:
        valid = row >= d
        b = b + jnp.where(valid, a * pltpu.roll(b, d, 0), 0.0)
        if 2 * d < n:
            a = jnp.where(valid, a * pltpu.roll(a, d, 0), a)
        d *= 2
    return b


def _scan_rows_const(lvl_re, lvl_im, pow_re, pow_im, br, bi, pr, pi):
    n, c = br.shape
    g = n // 8
    for k in range(3):
        d = 1 << k
        sr = pltpu.roll(br, d, 0).reshape(g, 8, c)
        si = pltpu.roll(bi, d, 0).reshape(g, 8, c)
        cr = lvl_re[k][None]
        ci = lvl_im[k][None]
        br, bi = ((br.reshape(g, 8, c) + (cr * sr - ci * si)).reshape(n, c),
                  (bi.reshape(g, 8, c) + (cr * si + ci * sr)).reshape(n, c))
    b3r = br.reshape(g, 8, c)
    b3i = bi.reshape(g, 8, c)
    outs_r, outs_i = [], []
    for gi in range(g):
        xr = b3r[gi] + (pow_re * pr - pow_im * pi)
        xi = b3i[gi] + (pow_re * pi + pow_im * pr)
        pr, pi = xr[7:8, :], xi[7:8, :]
        outs_r.append(xr)
        outs_i.append(xi)
    return jnp.concatenate(outs_r, axis=0), jnp.concatenate(outs_i, axis=0)


def _rglru_prompt_kernel(x_ref, w_ref, cw_ref, cb_ref, wa_ref, ba_ref, wx_ref, bx_ref, cl_ref,
                         ya_ref, tail_ref, h_ref, ext_scr, h_scr, *, tb):
    step = pl.program_id(0)

    @pl.when(step == 0)
    def _():
        ext_scr[0:8, :] = jnp.zeros((8, D_RNN), F32)
        h_scr[...] = jnp.zeros_like(h_scr)

    proj = _dotw(x_ref[...], w_ref[...])
    xa = proj[:, :D_RNN]
    ga = proj[:, D_RNN:]
    ext_scr[8:8 + tb, :] = xa
    xc = cb_ref[...] + xa * cw_ref[3:4, :]
    for j in range(3):
        xc = xc + ext_scr[pl.ds(5 + j, tb), :] * cw_ref[j:j + 1, :]
    tail = ext_scr[tb:tb + 8, :]
    ext_scr[0:8, :] = tail
    tail_ref[...] = tail

    a, inp = _rglru_gates(xc, wa_ref[...], ba_ref[...], wx_ref[...], bx_ref[...], cl_ref[...])
    row = lax.broadcasted_iota(jnp.int32, a.shape, 0)
    inp = inp + jnp.where(row == 0, a * h_scr[0:1, :], 0.0)
    h = _scan_rows(a, inp)
    h_last = jnp.broadcast_to(h[tb - 1:tb, :], (8, D_RNN))
    h_scr[...] = h_last
    h_ref[...] = h_last
    ya_ref[...] = h * _gelu(ga)


def _rglru_prompt(x, w_ag, cw, cb, wa, ba, wx, bx, c_lam, tb):
    t = x.shape[0]
    full = lambda shape: pl.BlockSpec(shape, lambda i: (0,) * len(shape))
    return pl.pallas_call(
        functools.partial(_rglru_prompt_kernel, tb=tb),
        grid=(t // tb,),
        in_specs=[pl.BlockSpec((tb, D_MODEL), lambda i: (i, 0)),
                  full(w_ag.shape), full(cw.shape), full(cb.shape), full(wa.shape), full(ba.shape),
                  full(wx.shape), full(bx.shape), full(c_lam.shape)],
        out_specs=[pl.BlockSpec((tb, D_RNN), lambda i: (i, 0)),
                   pl.BlockSpec((8, D_RNN), lambda i: (0, 0)),
                   pl.BlockSpec((8, D_RNN), lambda i: (0, 0))],
        out_shape=[jax.ShapeDtypeStruct((t, D_RNN), F32),
                   jax.ShapeDtypeStruct((8, D_RNN), F32),
                   jax.ShapeDtypeStruct((8, D_RNN), F32)],
        scratch_shapes=[pltpu.VMEM((tb + 8, D_RNN), F32), pltpu.VMEM((8, D_RNN), F32)],
        compiler_params=_cparams(),
        name="rglru_prompt",
    )(x, w_ag, cw, cb, wa, ba, wx, bx, c_lam)


def _s5_readout(u, xr_blocks, xi_blocks, cre_ref, cim_ref, d, wglu, bglu):
    ys = []
    for j in range(N_SSM_BLOCKS):
        ys.append(_dotw(xr_blocks[j], cre_ref[j]) - _dotw(xi_blocks[j], cim_ref[j]))
    y = jnp.concatenate(ys, axis=1) + d * u
    g = _gelu(y)
    return g * _sigmoid(_dotw(g, wglu) + bglu)


def _s5_prompt_kernel(x_ref, w_ref, bbr_ref, bbi_ref, lvr_ref, lvi_ref, pwr_ref, pwi_ref,
                      cre_ref, cim_ref, d_ref, wglu_ref, bglu_ref, yb_ref, sre_ref, sim_ref,
                      sr_scr, si_scr, *, tb):
    step = pl.program_id(0)

    @pl.when(step == 0)
    def _():
        sr_scr[...] = jnp.zeros_like(sr_scr)
        si_scr[...] = jnp.zeros_like(si_scr)

    u = _dotw(x_ref[...], w_ref[...])
    xr_blocks, xi_blocks = [], []
    for j in range(N_SSM_BLOCKS):
        cols = slice(j * STATE_BLOCK, (j + 1) * STATE_BLOCK)
        uj = u[:, j * LANES:(j + 1) * LANES]
        br = _dotw(uj, bbr_ref[j])
        bi = _dotw(uj, bbi_ref[j])
        xr, xi = _scan_rows_const(lvr_ref[:, :, cols], lvi_ref[:, :, cols], pwr_ref[:, cols],
                                  pwi_ref[:, cols], br, bi, sr_scr[0:1, cols], si_scr[0:1, cols])
        sr_scr[:, cols] = jnp.broadcast_to(xr[tb - 1:tb, :], (8, STATE_BLOCK))
        si_scr[:, cols] = jnp.broadcast_to(xi[tb - 1:tb, :], (8, STATE_BLOCK))
        xr_blocks.append(xr)
        xi_blocks.append(xi)
    sre_ref[...] = sr_scr[...]
    sim_ref[...] = si_scr[...]
    yb_ref[...] = _s5_readout(u, xr_blocks, xi_blocks, cre_ref, cim_ref, d_ref[...],
                              wglu_ref[...], bglu_ref[...])


def _s5_prompt(x, w_b, bbr, bbi, lvl_re, lvl_im, pow_re, pow_im, cre, cim, d, wglu, bglu, tb):
    t = x.shape[0]
    full = lambda shape: pl.BlockSpec(shape, lambda i: (0,) * len(shape))
    return pl.pallas_call(
        functools.partial(_s5_prompt_kernel, tb=tb),
        grid=(t // tb,),
        in_specs=[pl.BlockSpec((tb, D_MODEL), lambda i: (i, 0)),
                  full(w_b.shape), full(bbr.shape), full(bbi.shape), full(lvl_re.shape),
                  full(lvl_im.shape), full(pow_re.shape), full(pow_im.shape), full(cre.shape),
                  full(cim.shape), full(d.shape), full(wglu.shape), full(bglu.shape)],
        out_specs=[pl.BlockSpec((tb, D_SSM), lambda i: (i, 0)),
                   pl.BlockSpec((8, N_STATE), lambda i: (0, 0)),
                   pl.BlockSpec((8, N_STATE), lambda i: (0, 0))],
        out_shape=[jax.ShapeDtypeStruct((t, D_SSM), F32),
                   jax.ShapeDtypeStruct((8, N_STATE), F32),
                   jax.ShapeDtypeStruct((8, N_STATE), F32)],
        scratch_shapes=[pltpu.VMEM((8, N_STATE), F32), pltpu.VMEM((8, N_STATE), F32)],
        compiler_params=_cparams(),
        name="s5_prompt",
    )(x, w_b, bbr, bbi, lvl_re, lvl_im, pow_re, pow_im, cre, cim, d, wglu, bglu)


def _out_ln_kernel(*refs, n_in):
    a_refs = refs[:n_in]
    w_refs = refs[n_in:2 * n_in]
    x_ref, g_ref, b_ref, o_ref = refs[2 * n_in:]
    y = ALPHA * x_ref[...]
    for a_ref, w_ref in zip(a_refs, w_refs):
        y = y + _dotw(a_ref[...], w_ref[...])
    o_ref[...] = _layer_norm(y, g_ref[...], b_ref[...])


def _out_ln(acts, weights, x, g, b, tm, name):
    m = x.shape[0]
    n_in = len(acts)
    full = lambda shape: pl.BlockSpec(shape, lambda i: (0,) * len(shape))
    in_specs = ([pl.BlockSpec((tm, a.shape[1]), lambda i: (i, 0)) for a in acts]
                + [full(w.shape) for w in weights]
                + [pl.BlockSpec((tm, D_MODEL), lambda i: (i, 0)), full(g.shape), full(b.shape)])
    return pl.pallas_call(
        functools.partial(_out_ln_kernel, n_in=n_in),
        grid=(m // tm,),
        in_specs=in_specs,
        out_specs=pl.BlockSpec((tm, D_MODEL), lambda i: (i, 0)),
        out_shape=jax.ShapeDtypeStruct((m, D_MODEL), F32),
        compiler_params=_cparams(),
        name=name,
    )(*acts, *weights, x, g, b)


def _ffn_prompt_kernel(*refs, n_in, tm):
    a_refs = refs[:n_in]
    w_refs = refs[n_in:2 * n_in]
    (x_ref, g0_ref, b0_ref, wup_ref, wgate_ref, cw_ref, cb_ref, wdown_ref, g_ref, b_ref,
     o_ref, tail_ref, h_scr, carry_scr) = refs[2 * n_in:]
    step = pl.program_id(0)

    @pl.when(step == 0)
    def _():
        carry_scr[...] = jnp.zeros_like(carry_scr)

    y = ALPHA * x_ref[...]
    for a_ref, w_ref in zip(a_refs, w_refs):
        y = y + _dotw(a_ref[...], w_ref[...])
    x = _layer_norm(y, g0_ref[...], b0_ref[...])
    xb = x.astype(BF16)
    row = lax.broadcasted_iota(jnp.int32, (tm, FF_CHUNK), 0)
    for c in range(N_FF_CHUNKS):
        cols = slice(c * FF_CHUNK, (c + 1) * FF_CHUNK)
        up = jnp.dot(xb, wup_ref[:, cols], preferred_element_type=F32)
        gate = jnp.dot(xb, wgate_ref[:, cols], preferred_element_type=F32)
        prev1 = carry_scr[7:8, cols]
        prev2 = carry_scr[6:7, cols]
        up1 = jnp.where(row >= 1, pltpu.roll(up, 1, 0), prev1)
        up2 = jnp.where(row >= 2, pltpu.roll(up, 2, 0), jnp.where(row == 1, prev1, prev2))
        hc = (cb_ref[:, cols] + up2 * cw_ref[0:1, cols] + up1 * cw_ref[1:2, cols]
              + up * cw_ref[2:3, cols])
        h_scr[:, cols] = (_gelu(hc) * gate).astype(BF16)
        carry_scr[:, cols] = up[tm - 8:tm, :]
    tail_ref[...] = carry_scr[...]
    f = jnp.dot(h_scr[...], wdown_ref[...], preferred_element_type=F32)
    o_ref[...] = _layer_norm(ALPHA * x + f, g_ref[...], b_ref[...])


def _ffn_prompt(acts, weights, x, g0, b0, wup, wgate, cw, cb, wdown, g, b, tm):
    t = x.shape[0]
    n_in = len(acts)
    const = lambda a: pl.BlockSpec(a.shape, lambda i: (0,) * a.ndim,
                                   pipeline_mode=pl.Buffered(1))
    rows = lambda a: pl.BlockSpec((tm, a.shape[1]), lambda i: (i, 0))
    consts = [g0, b0, wup, wgate, cw, cb, wdown, g, b]
    return pl.pallas_call(
        functools.partial(_ffn_prompt_kernel, n_in=n_in, tm=tm),
        grid=(t // tm,),
        in_specs=([rows(a) for a in acts] + [const(w) for w in weights] + [rows(x)]
                  + [const(c) for c in consts]),
        out_specs=[pl.BlockSpec((tm, D_MODEL), lambda i: (i, 0)),
                   pl.BlockSpec((8, D_FF), lambda i: (0, 0))],
        out_shape=[jax.ShapeDtypeStruct((t, D_MODEL), F32),
                   jax.ShapeDtypeStruct((8, D_FF), F32)],
        scratch_shapes=[pltpu.VMEM((tm, D_FF), BF16), pltpu.VMEM((8, D_FF), F32)],
        compiler_params=_cparams(),
        name="ffn_prompt",
    )(*acts, *weights, x, *consts)


def _aug_lhs(f3):
    h1, h2, h3 = (h.astype(F32) for h in _split3(f3))
    lane = lax.broadcasted_iota(jnp.int32, f3.shape, 1)
    pieces = jnp.where(lane < N_HEADS, h1, jnp.where(lane < 2 * N_HEADS, h2, jnp.where(
        lane < 3 * N_HEADS, h3, jnp.where(lane < 4 * N_HEADS, 1.0, 0.0))))
    return pieces.astype(BF16)


def _qkv_prompt_kernel(x_ref, w_ref, wf_ref, bf_ref, selq_ref, selk_ref,
                       k_ref, v_ref, lf_ref, qa_ref, ka_ref, va_ref, f_scr, *, tm):
    step = pl.program_id(0)

    @pl.when(step == 0)
    def _():
        f_scr[...] = jnp.zeros_like(f_scr)

    x = x_ref[...]
    qkv = _dotw(x, w_ref[...])
    k = qkv[:, D_MODEL:2 * D_MODEL]
    v = qkv[:, 2 * D_MODEL:]
    k_ref[...] = k
    v_ref[...] = v
    logf3 = _log_sigmoid(_dotw(x, wf_ref[...]) + bf_ref[...])
    lf_ref[...] = logf3

    r_i = lax.broadcasted_iota(jnp.int32, (tm, tm), 0)
    c_i = lax.broadcasted_iota(jnp.int32, (tm, tm), 1)
    tri = jnp.where(r_i >= c_i, 1.0, 0.0).astype(BF16)
    f3 = _dot_exact01(tri, logf3) + f_scr[0:1, :]
    f_scr[...] = jnp.broadcast_to(f3[tm - 1:tm, :], f_scr.shape)

    lhs = _aug_lhs(f3 * LOG2E)
    augq = jnp.dot(lhs, selq_ref[...], preferred_element_type=F32)
    augk = jnp.dot(lhs, selk_ref[...], preferred_element_type=F32)
    lane = lax.broadcasted_iota(jnp.int32, (tm, LANES), 1)
    low = lane < HEAD_DIM
    vtail = jnp.where(lane == HEAD_DIM, 1.0, 0.0)
    scale = HEAD_DIM ** -0.5 * LOG2E
    for hp in range(N_HEADS // 2):
        cols = slice(hp * LANES, (hp + 1) * LANES)
        qb = qkv[:, cols] * scale
        kb = k[:, cols]
        vb = v[:, cols]
        for odd in range(2):
            h = 2 * hp + odd
            acols = slice(h * LANES, (h + 1) * LANES)
            if odd:
                qb, kb, vb = (pltpu.roll(qb, HEAD_DIM, 1), pltpu.roll(kb, HEAD_DIM, 1),
                              pltpu.roll(vb, HEAD_DIM, 1))
            qa_ref[h] = jnp.where(low, qb, augq[:, acols]).astype(BF16)
            ka_ref[h] = jnp.where(low, kb, augk[:, acols]).astype(BF16)
            va_ref[h] = jnp.where(low, vb, vtail).astype(BF16)


def _qkv_prompt(x, w_qkv, wf3, bf3, selq, selk, tm):
    t = x.shape[0]
    full = lambda shape: pl.BlockSpec(shape, lambda i: (0,) * len(shape))
    head_major = jax.ShapeDtypeStruct((N_HEADS, t, LANES), BF16)
    head_spec = pl.BlockSpec((N_HEADS, tm, LANES), lambda i: (0, i, 0))
    return pl.pallas_call(
        functools.partial(_qkv_prompt_kernel, tm=tm),
        grid=(t // tm,),
        in_specs=[pl.BlockSpec((tm, D_MODEL), lambda i: (i, 0)),
                  full(w_qkv.shape), full(wf3.shape), full(bf3.shape), full(selq.shape),
                  full(selk.shape)],
        out_specs=[pl.BlockSpec((tm, D_MODEL), lambda i: (i, 0)),
                   pl.BlockSpec((tm, D_MODEL), lambda i: (i, 0)),
                   pl.BlockSpec((tm, LANES), lambda i: (i, 0)),
                   head_spec, head_spec, head_spec],
        out_shape=[jax.ShapeDtypeStruct((t, D_MODEL), F32),
                   jax.ShapeDtypeStruct((t, D_MODEL), F32),
                   jax.ShapeDtypeStruct((t, LANES), F32),
                   head_major, head_major, head_major],
        scratch_shapes=[pltpu.VMEM((8, LANES), F32)],
        compiler_params=_cparams(),
        name="qkv_prompt",
    )(x, w_qkv, wf3, bf3, selq, selk)


def _flash_kernel(it_ref, jt_ref, q_ref, k_ref, v_ref, o_ref, m_scr, acc_scr, *, tq, tk, qc):
    s_idx = pl.program_id(1)
    i = it_ref[s_idx]
    j = jt_ref[s_idx]

    @pl.when(j == 0)
    def _():
        m_scr[...] = jnp.full(m_scr.shape, NEG_BIG, F32)
        acc_scr[...] = jnp.zeros_like(acc_scr)

    def accumulate(masked):
        chunks = [(hh, c) for hh in range(2) for c in range(tq // qc)]

        def n_keys(c):
            return (c + 1) * qc if masked else tk

        def scores(hh, c):
            rows = slice(c * qc, (c + 1) * qc)
            nk = n_keys(c)
            s = lax.dot_general(q_ref[hh, rows, :], k_ref[hh, 0:nk, :], (((1,), (1,)), ((), ())),
                                preferred_element_type=F32)
            if masked:
                qpos = c * qc + lax.broadcasted_iota(jnp.int32, (qc, nk), 0)
                kpos = lax.broadcasted_iota(jnp.int32, (qc, nk), 1)
                s = jnp.where(kpos <= qpos, s, NEG_BIG)
            return s

        ahead = 2
        pending = [scores(*chunks[n]) for n in range(ahead)]
        for n, (hh, c) in enumerate(chunks):
            s = pending.pop(0)
            if n + ahead < len(chunks):
                pending.append(scores(*chunks[n + ahead]))
            rows = slice(c * qc, (c + 1) * qc)
            m_prev = m_scr[hh, rows, :]
            m_next = jnp.maximum(m_prev, jnp.max(s, axis=1, keepdims=True))
            alpha = jnp.exp2(m_prev - m_next)
            p = jnp.exp2(s - m_next[:, 0:1]).astype(BF16)
            acc_scr[hh, rows, :] = alpha * acc_scr[hh, rows, :] + jnp.dot(
                p, v_ref[hh, 0:n_keys(c), :], preferred_element_type=F32)
            m_scr[hh, rows, :] = m_next

    assert tq == tk, "the diagonal-block key trimming assumes square blocks"
    crosses_diagonal = j == i

    @pl.when(crosses_diagonal)
    def _():
        accumulate(True)

    @pl.when(jnp.logical_not(crosses_diagonal))
    def _():
        accumulate(False)

    @pl.when(j == ((i + 1) * tq - 1) // tk)
    def _():
        lane = lax.broadcasted_iota(jnp.int32, (tq, LANES), 1)
        outs = []
        for hh in range(2):
            acc = acc_scr[hh]
            outs.append(acc / acc[:, HEAD_DIM:HEAD_DIM + 1])
        o_ref[...] = jnp.where(lane < HEAD_DIM, outs[0], pltpu.roll(outs[1], HEAD_DIM, 1))


def _flash(qa, ka, va, tq, tk, qc):
    t = ka.shape[1]
    nq = t // tq
    i_list, j_list = [], []
    for i in range(nq):
        for j in range(((i + 1) * tq - 1) // tk + 1):
            i_list.append(i)
            j_list.append(j)
    it = jnp.asarray(i_list, jnp.int32)
    jt = jnp.asarray(j_list, jnp.int32)
    grid_spec = pltpu.PrefetchScalarGridSpec(
        num_scalar_prefetch=2,
        grid=(N_HEADS // 2, len(i_list)),
        in_specs=[pl.BlockSpec((2, tq, LANES), lambda hp, s, it, jt: (hp, it[s], 0)),
                  pl.BlockSpec((2, tk, LANES), lambda hp, s, it, jt: (hp, jt[s], 0)),
                  pl.BlockSpec((2, tk, LANES), lambda hp, s, it, jt: (hp, jt[s], 0))],
        out_specs=pl.BlockSpec((tq, LANES), lambda hp, s, it, jt: (it[s], hp)),
        scratch_shapes=[pltpu.VMEM((2, tq, LANES), F32), pltpu.VMEM((2, tq, LANES), F32)],
    )
    return pl.pallas_call(
        functools.partial(_flash_kernel, tq=tq, tk=tk, qc=qc),
        grid_spec=grid_spec,
        out_shape=jax.ShapeDtypeStruct((t, D_MODEL), F32),
        compiler_params=_cparams(2),
        name="flash_prompt",
    )(it, jt, qa, ka, va)


def _mix0_sample_kernel(x_ref, w_ref, c0_ref, c1_ref, c2_ref, cw_ref, cb_ref, wa_ref, ba_ref,
                        wx_ref, bx_ref, cl_ref, h0_ref, sr_ref, si_ref, bbr_ref, bbi_ref,
                        ar_ref, ai_ref, cre_ref, cim_ref, d_ref, wglu_ref, bglu_ref,
                        ya_ref, yb_ref, xa_ref, h_ref, xr_ref, xi_ref):
    proj = _dotw(x_ref[...], w_ref[...])
    xa = proj[:, :D_RNN]
    ga = proj[:, D_RNN:2 * D_RNN]
    u = proj[:, 2 * D_RNN:]
    xa_ref[...] = xa
    xc = (cb_ref[...] + c0_ref[...] * cw_ref[0:1, :] + c1_ref[...] * cw_ref[1:2, :]
          + c2_ref[...] * cw_ref[2:3, :] + xa * cw_ref[3:4, :])
    a, inp = _rglru_gates(xc, wa_ref[...], ba_ref[...], wx_ref[...], bx_ref[...], cl_ref[...])
    h = a * h0_ref[...] + inp
    h_ref[...] = h
    ya_ref[...] = h * _gelu(ga)

    xr_blocks, xi_blocks = [], []
    for j in range(N_SSM_BLOCKS):
        cols = slice(j * STATE_BLOCK, (j + 1) * STATE_BLOCK)
        uj = u[:, j * LANES:(j + 1) * LANES]
        ar = ar_ref[:, cols]
        ai = ai_ref[:, cols]
        pr = sr_ref[:, cols]
        pi = si_ref[:, cols]
        xr = _dotw(uj, bbr_ref[j]) + (ar * pr - ai * pi)
        xi = _dotw(uj, bbi_ref[j]) + (ar * pi + ai * pr)
        xr_ref[:, cols] = xr
        xi_ref[:, cols] = xi
        xr_blocks.append(xr)
        xi_blocks.append(xi)
    yb_ref[...] = _s5_readout(u, xr_blocks, xi_blocks, cre_ref, cim_ref, d_ref[...],
                              wglu_ref[...], bglu_ref[...])


def _mix0_sample(*args):
    n = args[0].shape[0]
    shapes = [(n, D_RNN), (n, D_SSM), (n, D_RNN), (n, D_RNN), (n, N_STATE), (n, N_STATE)]
    return pl.pallas_call(
        _mix0_sample_kernel,
        out_shape=[jax.ShapeDtypeStruct(s, F32) for s in shapes],
        compiler_params=pltpu.CompilerParams(vmem_limit_bytes=VMEM_LIMIT),
        name="mix0_sample",
    )(*args)


def _ffn_sample_kernel(x_ref, wup_ref, wgate_ref, b0_ref, b1_ref, cw_ref, cb_ref, wdown_ref,
                       g_ref, b_ref, o_ref, up_ref, acc_scr):
    c = pl.program_id(0)

    @pl.when(c == 0)
    def _():
        acc_scr[...] = jnp.zeros_like(acc_scr)

    x = x_ref[...]
    up = _dotw(x, wup_ref[...])
    gate = _dotw(x, wgate_ref[...])
    up_ref[...] = up
    hc = (cb_ref[...] + b0_ref[...] * cw_ref[0:1, :] + b1_ref[...] * cw_ref[1:2, :]
          + up * cw_ref[2:3, :])
    acc_scr[...] += _dotw(_gelu(hc) * gate, wdown_ref[...])

    @pl.when(c == pl.num_programs(0) - 1)
    def _():
        o_ref[...] = _layer_norm(ALPHA * x + acc_scr[...], g_ref[...], b_ref[...])


def _ffn_sample(x, wup, wgate, buf0, buf1, cw, cb, wdown, g, b):
    n = x.shape[0]
    fc = FF_CHUNK
    col = lambda rows: pl.BlockSpec((rows, fc), lambda c: (0, c))
    full = lambda shape: pl.BlockSpec(shape, lambda c: (0,) * len(shape))
    return pl.pallas_call(
        _ffn_sample_kernel,
        grid=(D_FF // fc,),
        in_specs=[full(x.shape), col(D_MODEL), col(D_MODEL), col(n), col(n), col(cw.shape[0]),
                  col(1), pl.BlockSpec((fc, D_MODEL), lambda c: (c, 0)), full(g.shape),
                  full(b.shape)],
        out_specs=[full((n, D_MODEL)), col(n)],
        out_shape=[jax.ShapeDtypeStruct((n, D_MODEL), F32), jax.ShapeDtypeStruct((n, D_FF), F32)],
        scratch_shapes=[pltpu.VMEM((n, D_MODEL), F32)],
        compiler_params=_cparams(),
        name="ffn_sample",
    )(x, wup, wgate, buf0, buf1, cw, cb, wdown, g, b)


def _qkv_sample_kernel(x_ref, w_ref, wf_ref, bf_ref, o_ref, lf_ref):
    x = x_ref[...]
    o_ref[...] = _dotw(x, w_ref[...])

    @pl.when(pl.program_id(0) == 0)
    def _():
        lf_ref[...] = _log_sigmoid(_dotw(x, wf_ref[...]) + bf_ref[...])


def _qkv_sample(x, w_in1, wf, bf):
    n = x.shape[0]
    full = lambda shape: pl.BlockSpec(shape, lambda c: (0,) * len(shape))
    return pl.pallas_call(
        _qkv_sample_kernel,
        grid=(3,),
        in_specs=[full(x.shape), pl.BlockSpec((D_MODEL, D_MODEL), lambda c: (0, c)),
                  full(wf.shape), full(bf.shape)],
        out_specs=[pl.BlockSpec((n, D_MODEL), lambda c: (0, c)), full((n, LANES))],
        out_shape=[jax.ShapeDtypeStruct((n, 3 * D_MODEL), F32),
                   jax.ShapeDtypeStruct((n, LANES), F32)],
        compiler_params=_cparams(),
        name="qkv_sample",
    )(x, w_in1, wf, bf)


def _attn_sample_kernel(pt_ref, q_ref, kn_ref, vn_ref, gt_ref, *refs, pages_per_step):
    pp = pages_per_step
    k_refs = refs[:pp]
    v_refs = refs[pp:2 * pp]
    lf_refs = refs[2 * pp:3 * pp]
    o_ref, m_scr, l_scr, acc_scr, carry_scr = refs[3 * pp:]
    s_idx = pl.program_id(1)

    @pl.when(s_idx == 0)
    def _():
        m_scr[...] = jnp.full(m_scr.shape, NEG_BIG, F32)
        l_scr[...] = jnp.zeros_like(l_scr)
        acc_scr[...] = jnp.zeros_like(acc_scr)
        carry_scr[...] = jnp.zeros_like(carry_scr)

    hrow = lax.broadcasted_iota(jnp.int32, (N_HEADS, D_MODEL), 0)
    hcol = lax.broadcasted_iota(jnp.int32, (N_HEADS, D_MODEL), 1) // HEAD_DIM
    diag = hrow == hcol
    scale = HEAD_DIM ** -0.5
    qbd = jnp.where(diag, q_ref[0] * scale, 0.0)
    qbd_b = qbd.astype(BF16)
    r_i = lax.broadcasted_iota(jnp.int32, (PAGE_SIZE, PAGE_SIZE), 0)
    c_i = lax.broadcasted_iota(jnp.int32, (PAGE_SIZE, PAGE_SIZE), 1)
    later = jnp.where(r_i > c_i, 1.0, 0.0).astype(BF16)
    gt = gt_ref[0]

    sts, vts, sufs, tots = [], [], [], []
    for r in range(pp):
        kt = k_refs[r][0].reshape(D_MODEL, PAGE_SIZE).astype(BF16)
        vts.append(v_refs[r][0].reshape(D_MODEL, PAGE_SIZE).astype(BF16))
        sts.append(jnp.dot(qbd_b, kt, preferred_element_type=F32))
        lf = lf_refs[r][0]
        sufs.append(sum(jnp.dot(pc, later, preferred_element_type=F32) for pc in _split3(lf)))
        tots.append(jnp.sum(lf, axis=1, keepdims=True))

    carry = carry_scr[:, 0:1]
    logits = []
    for r in range(pp):
        logits.append(sts[r] + sufs[r] + (carry + gt))
        carry = carry + tots[r]
    carry_scr[...] = jnp.broadcast_to(carry, carry_scr.shape)

    mx = logits[0]
    for lg in logits[1:]:
        mx = jnp.maximum(mx, lg)
    m_prev = m_scr[:, 0:1]
    m_next = jnp.maximum(m_prev, jnp.max(mx, axis=1, keepdims=True))
    alpha = jnp.exp(m_prev - m_next)
    ps = [jnp.exp(lg - m_next) for lg in logits]
    psum = ps[0]
    for pblk in ps[1:]:
        psum = psum + pblk
    l_scr[...] = jnp.broadcast_to(
        alpha * l_scr[:, 0:1] + jnp.sum(psum, axis=1, keepdims=True), l_scr.shape)
    acc = alpha * acc_scr[...]
    for r in range(pp):
        acc = acc + lax.dot_general(ps[r].astype(BF16), vts[r], (((1,), (1,)), ((), ())),
                                    preferred_element_type=F32)
    acc_scr[...] = acc
    m_scr[...] = jnp.broadcast_to(m_next, m_scr.shape)

    @pl.when(s_idx == pl.num_programs(1) - 1)
    def _():
        s_self = jnp.sum(qbd * kn_ref[0], axis=1, keepdims=True)
        m_prev = m_scr[:, 0:1]
        m_fin = jnp.maximum(m_prev, s_self)
        a_past = jnp.exp(m_prev - m_fin)
        a_self = jnp.exp(s_self - m_fin)
        l_fin = a_past * l_scr[:, 0:1] + a_self
        acc = (a_past * acc_scr[...] + a_self * vn_ref[0]) / l_fin
        o_ref[0] = jnp.sum(jnp.where(diag, acc, 0.0), axis=0, keepdims=True)


def _attn_sample(page_table, q, k_new, v_new, logf_new, cache_k, cache_v, cache_logf,
                 pages_per_step):
    n_seq, n_pages = page_table.shape
    pp = pages_per_step
    pt = page_table.reshape(-1)
    ck = jnp.transpose(cache_k, (0, 2, 3, 1))
    cv = jnp.transpose(cache_v, (0, 2, 3, 1))
    clf = jnp.transpose(cache_logf, (0, 2, 1))
    row = lambda a: a.reshape(n_seq, 1, D_MODEL)

    def page_map(r, ndim):
        return lambda b, s, pt: (pt[b * n_pages + (n_pages - 1 - (s * pp + r))],) + (0,) * ndim

    per_seq = lambda: pl.BlockSpec((1, 1, D_MODEL), lambda b, s, pt: (b, 0, 0))
    in_specs = [per_seq(), per_seq(), per_seq(),
                pl.BlockSpec((1, N_HEADS, 1), lambda b, s, pt: (b, 0, 0))]
    kv_block = (1, N_HEADS, HEAD_DIM, PAGE_SIZE)
    in_specs += [pl.BlockSpec(kv_block, page_map(r, 3)) for r in range(pp)]
    in_specs += [pl.BlockSpec(kv_block, page_map(r, 3)) for r in range(pp)]
    in_specs += [pl.BlockSpec((1, N_HEADS, PAGE_SIZE), page_map(r, 2)) for r in range(pp)]
    grid_spec = pltpu.PrefetchScalarGridSpec(
        num_scalar_prefetch=1,
        grid=(n_seq, n_pages // pp),
        in_specs=in_specs,
        out_specs=per_seq(),
        scratch_shapes=[pltpu.VMEM((N_HEADS, LANES), F32), pltpu.VMEM((N_HEADS, LANES), F32),
                        pltpu.VMEM((N_HEADS, D_MODEL), F32), pltpu.VMEM((N_HEADS, LANES), F32)],
    )
    out = pl.pallas_call(
        functools.partial(_attn_sample_kernel, pages_per_step=pp),
        grid_spec=grid_spec,
        out_shape=jax.ShapeDtypeStruct((n_seq, 1, D_MODEL), F32),
        compiler_params=_cparams(2),
        name="attn_sample",
    )(pt, row(q), row(k_new), row(v_new), logf_new.reshape(n_seq, N_HEADS, 1),
      *([ck] * pp), *([cv] * pp), *([clf] * pp))
    return out.reshape(n_seq, D_MODEL)


def _block_diag_dense(w):
    nb, n, _ = w.shape
    eye = jnp.eye(nb, dtype=w.dtype)
    return jnp.einsum('hij,hk->hikj', w, eye).reshape(nb * n, nb * n)


def _s5_discretize(lam_re, lam_im, log_dt, b_re, b_im):
    lr = jnp.minimum(lam_re, -1e-4)
    li = lam_im
    dt = jnp.exp(log_dt)[:, None]
    mag = jnp.exp(lr * dt)
    ab_re = mag * jnp.cos(li * dt)
    ab_im = mag * jnp.sin(li * dt)
    den = lr * lr + li * li
    nr = ab_re - 1.0
    zr = (nr * lr + ab_im * li) / den
    zi = (ab_im * lr - nr * li) / den
    bb_re = zr[..., None] * b_re - zi[..., None] * b_im
    bb_im = zr[..., None] * b_im + zi[..., None] * b_re
    return ab_re, ab_im, bb_re, bb_im


def _s5_scan_tables(a_re, a_im):
    ar = a_re.reshape(1, N_STATE)
    ai = a_im.reshape(1, N_STATE)
    pows = [(ar, ai)]
    for _ in range(7):
        qr, qi = pows[-1]
        pows.append((qr * ar - qi * ai, qr * ai + qi * ar))
    pow_re = jnp.concatenate([q[0] for q in pows], axis=0)
    pow_im = jnp.concatenate([q[1] for q in pows], axis=0)
    sub = jnp.arange(8)[:, None]
    lvl_re = jnp.stack([jnp.where(sub >= d, pows[d - 1][0], 0.0) for d in (1, 2, 4)])
    lvl_im = jnp.stack([jnp.where(sub >= d, pows[d - 1][1], 0.0) for d in (1, 2, 4)])
    return dict(lvl_re=lvl_re, lvl_im=lvl_im, pow_re=pow_re, pow_im=pow_im)


def _s5_in_blocks(bb):
    eye = jnp.eye(GROUPS_PER_BLOCK, dtype=bb.dtype)
    b4 = bb.transpose(0, 2, 1).reshape(N_SSM_BLOCKS, GROUPS_PER_BLOCK, S5_GROUP, S5_STATE)
    return jnp.einsum('jgmp,gh->jgmhp', b4, eye).reshape(N_SSM_BLOCKS, LANES, STATE_BLOCK)


def _s5_out_blocks(c):
    eye = jnp.eye(GROUPS_PER_BLOCK, dtype=c.dtype)
    c4 = c.reshape(N_SSM_BLOCKS, GROUPS_PER_BLOCK, S5_GROUP, S5_STATE)
    return jnp.einsum('jgmp,gh->jgphm', c4, eye).reshape(N_SSM_BLOCKS, STATE_BLOCK, LANES)


def _aug_selectors():
    rows = jnp.arange(LANES)[:, None]
    cols = jnp.arange(N_HEADS * LANES)[None, :]
    head = cols // LANES
    c = cols % LANES - HEAD_DIM
    piece = rows // N_HEADS
    rhead = rows % N_HEADS
    same = (rhead == head) & (rows < 4 * N_HEADS)
    f_piece = same & (piece < 3)
    ones_piece = same & (piece == 3)
    selq = jnp.where(f_piece & (c == piece), 1.0, 0.0) + jnp.where(
        ones_piece & (c >= 3) & (c < AUG_COLS), 1.0, 0.0)
    selk = jnp.where(ones_piece & (c >= 0) & (c < 3), 1.0, 0.0) - jnp.where(
        f_piece & (c == piece + 3), 1.0, 0.0)
    return selq.astype(BF16), selk.astype(BF16)


def _forget_cols(w_in1, b_fgt, dtype):
    wf = w_in1[:, 3 * D_MODEL:]
    pad = jnp.zeros((D_MODEL, LANES - 3 * N_HEADS), w_in1.dtype)
    wf3 = jnp.concatenate([wf, wf, wf, pad], axis=1).astype(dtype)
    bf3 = jnp.concatenate([b_fgt, b_fgt, b_fgt, jnp.zeros((LANES - 3 * N_HEADS,), F32)])[None, :]
    return wf3, bf3


def _row(v):
    return v.reshape(1, -1)


def _pick(t, pref):
    return pref if t % pref == 0 else t


def _prompt_path(x, p, s5p, tiles):
    t = x.shape[0]
    bf = lambda w: w.astype(BF16)
    tb = _pick(t, tiles['scan'])
    ya, tail, h_last = _rglru_prompt(
        x, bf(p['w_in0'][:, :2 * D_RNN]), p['rg_conv_w'], _row(p['rg_conv_b']),
        bf(s5p['wa_bd']), _row(p['rg_ba']), bf(s5p['wx_bd']), _row(p['rg_bx']), s5p['c_lam'], tb)
    yb, s_re, s_im = _s5_prompt(
        x, bf(p['w_in0'][:, 2 * D_RNN:]), bf(s5p['bbr']), bf(s5p['bbi']), s5p['lvl_re'],
        s5p['lvl_im'], s5p['pow_re'], s5p['pow_im'],
        bf(s5p['cre']), bf(s5p['cim']), _row(p['s5_d']), bf(p['s5_w_glu']), _row(p['s5_b_glu']), tb)
    tm = _pick(t, tiles['rows'])
    ln = lambda a, layer, which: a[layer, which][None]
    x2, ffn_tail0 = _ffn_prompt(
        [ya, yb], [bf(p['w_out0'][:D_RNN]), bf(p['w_out0'][D_RNN:])], x,
        ln(p['ln_g'], 0, 0), ln(p['ln_b'], 0, 0), bf(p['ffn_w_up'][0]), bf(p['ffn_w_gate'][0]),
        p['ffn_conv_w'][0], p['ffn_conv_b'][0][None], bf(p['ffn_w_down'][0]),
        ln(p['ln_g'], 0, 1), ln(p['ln_b'], 0, 1), tm)
    wf3, bf3 = _forget_cols(p['w_in1'], p['b_fgt'], BF16)
    selq, selk = _aug_selectors()
    k, v, lf, qa, ka, va = _qkv_prompt(x2, bf(p['w_in1'][:, :3 * D_MODEL]), wf3, bf3, selq, selk,
                                       _pick(t, tiles['qkv']))
    o = _flash(qa, ka, va, _pick(t, tiles['tq']), _pick(t, tiles['tk']), tiles['qc'])
    x4, ffn_tail1 = _ffn_prompt(
        [o], [bf(p['w_out1'])], x2, ln(p['ln_g'], 1, 0), ln(p['ln_b'], 1, 0),
        bf(p['ffn_w_up'][1]), bf(p['ffn_w_gate'][1]), p['ffn_conv_w'][1],
        p['ffn_conv_b'][1][None], bf(p['ffn_w_down'][1]), ln(p['ln_g'], 1, 1),
        ln(p['ln_b'], 1, 1), tm)
    return (x4[None], tail[None, 5:8], h_last[0:1], s_re[0:1].reshape(1, S5_GROUPS, S5_STATE),
            s_im[0:1].reshape(1, S5_GROUPS, S5_STATE), k.reshape(1, t, N_HEADS, HEAD_DIM),
            v.reshape(1, t, N_HEADS, HEAD_DIM), lf[None, :, :N_HEADS],
            jnp.stack([ffn_tail0[None, 6:8], ffn_tail1[None, 6:8]], axis=0))


def _sample_path(x, rg_conv, rg_h, s5_re, s5_im, cache_k, cache_v, cache_logf, ffn_conv,
                 page_table, p, s5p, pages_per_step):
    n = x.shape[0]
    ya, yb, xa, h_new, xr, xi = _mix0_sample(
        x, p['w_in0'], rg_conv[:, 0], rg_conv[:, 1], rg_conv[:, 2], p['rg_conv_w'],
        _row(p['rg_conv_b']), s5p['wa_bd'], _row(p['rg_ba']), s5p['wx_bd'], _row(p['rg_bx']),
        s5p['c_lam'], rg_h, s5_re.reshape(n, N_STATE), s5_im.reshape(n, N_STATE),
        s5p['bbr'], s5p['bbi'], s5p['a_re'], s5p['a_im'], s5p['cre'], s5p['cim'],
        _row(p['s5_d']), p['s5_w_glu'], _row(p['s5_b_glu']))
    x1 = _out_ln([ya, yb], [p['w_out0'][:D_RNN], p['w_out0'][D_RNN:]], x,
                 p['ln_g'][0, 0][None], p['ln_b'][0, 0][None], n, "out0_sample")
    x2, up0 = _ffn_sample(x1, p['ffn_w_up'][0], p['ffn_w_gate'][0], ffn_conv[0, :, 0],
                          ffn_conv[0, :, 1], p['ffn_conv_w'][0], p['ffn_conv_b'][0][None],
                          p['ffn_w_down'][0], p['ln_g'][0, 1][None], p['ln_b'][0, 1][None])
    wf3, bf3 = _forget_cols(p['w_in1'], p['b_fgt'], F32)
    qkv, lf3 = _qkv_sample(x2, p['w_in1'], wf3, bf3)
    q = qkv[:, :D_MODEL]
    k = qkv[:, D_MODEL:2 * D_MODEL]
    v = qkv[:, 2 * D_MODEL:]
    logf = lf3[:, :N_HEADS]
    o = _attn_sample(page_table, q, k, v, logf, cache_k, cache_v, cache_logf, pages_per_step)
    x3 = _out_ln([o], [p['w_out1']], x2, p['ln_g'][1, 0][None], p['ln_b'][1, 0][None], n,
                 "out1_sample")
    x4, up1 = _ffn_sample(x3, p['ffn_w_up'][1], p['ffn_w_gate'][1], ffn_conv[1, :, 0],
                          ffn_conv[1, :, 1], p['ffn_conv_w'][1], p['ffn_conv_b'][1][None],
                          p['ffn_w_down'][1], p['ln_g'][1, 1][None], p['ln_b'][1, 1][None])
    new_rg_conv = jnp.concatenate([rg_conv[:, 1:], xa[:, None]], axis=1)
    new_ffn = jnp.stack([jnp.stack([ffn_conv[0, :, 1], up0], axis=1),
                         jnp.stack([ffn_conv[1, :, 1], up1], axis=1)], axis=0)
    return (x4[:, None], new_rg_conv, h_new, xr.reshape(n, S5_GROUPS, S5_STATE),
            xi.reshape(n, S5_GROUPS, S5_STATE), k.reshape(n, 1, N_HEADS, HEAD_DIM),
            v.reshape(n, 1, N_HEADS, HEAD_DIM), logf[:, None], new_ffn)


def _derived_params(p):
    a_re, a_im, bb_re, bb_im = _s5_discretize(p['s5_lambda_re'], p['s5_lambda_im'], p['s5_log_dt'],
                                              p['s5_b_re'], p['s5_b_im'])
    return dict(
        wa_bd=_block_diag_dense(p['rg_wa']), wx_bd=_block_diag_dense(p['rg_wx']),
        c_lam=_row(-RG_C * jax.nn.softplus(-p['rg_lambda'])),
        a_re=a_re.reshape(1, N_STATE), a_im=a_im.reshape(1, N_STATE),
        **_s5_scan_tables(a_re, a_im),
        bbr=_s5_in_blocks(bb_re), bbi=_s5_in_blocks(bb_im),
        cre=_s5_out_blocks(p['s5_c_re']), cim=_s5_out_blocks(p['s5_c_im']))


PROMPT_TILES = dict(scan=256, rows=512, qkv=256, tq=1024, tk=1024, qc=256)
PAGES_PER_STEP = 8


def kernel(x_prompt, x_sample, state_rglru_conv, state_rglru_h, state_s5_re, state_s5_im, cache_k, cache_v, cache_logf, state_ffn_conv, page_table, w_in0, rg_conv_w, rg_conv_b, rg_wa, rg_ba, rg_wx, rg_bx, rg_lambda, s5_lambda_re, s5_lambda_im, s5_log_dt, s5_b_re, s5_b_im, s5_c_re, s5_c_im, s5_d, s5_w_glu, s5_b_glu, w_out0, w_in1, b_fgt, w_out1, ffn_w_up, ffn_w_gate, ffn_conv_w, ffn_conv_b, ffn_w_down, ln_g, ln_b):
    p = dict(w_in0=w_in0, rg_conv_w=rg_conv_w, rg_conv_b=rg_conv_b, rg_wa=rg_wa, rg_ba=rg_ba,
             rg_wx=rg_wx, rg_bx=rg_bx, rg_lambda=rg_lambda, s5_lambda_re=s5_lambda_re,
             s5_lambda_im=s5_lambda_im, s5_log_dt=s5_log_dt, s5_b_re=s5_b_re, s5_b_im=s5_b_im,
             s5_c_re=s5_c_re, s5_c_im=s5_c_im, s5_d=s5_d, s5_w_glu=s5_w_glu, s5_b_glu=s5_b_glu,
             w_out0=w_out0, w_in1=w_in1, b_fgt=b_fgt, w_out1=w_out1, ffn_w_up=ffn_w_up,
             ffn_w_gate=ffn_w_gate, ffn_conv_w=ffn_conv_w, ffn_conv_b=ffn_conv_b,
             ffn_w_down=ffn_w_down, ln_g=ln_g, ln_b=ln_b)
    s5p = _derived_params(p)
    prompt = _prompt_path(x_prompt[0], p, s5p, PROMPT_TILES)
    sample = _sample_path(x_sample[:, 0], state_rglru_conv, state_rglru_h, state_s5_re, state_s5_im,
                          cache_k, cache_v, cache_logf, state_ffn_conv, page_table, p, s5p,
                          PAGES_PER_STEP)
    return (prompt[0], sample[0]) + tuple(prompt[1:]) + tuple(sample[1:])
```

```python
import functools

import jax
import jax.numpy as jnp
from jax import lax
from jax.experimental import pallas as pl
from jax.experimental.pallas import tpu as pltpu

F32 = jnp.float32
BF16 = jnp.bfloat16

D_MODEL = 1024
DEPTH = 2
PAGE_SIZE = 128
D_RNN = 512
RG_BLOCKS = 8
RG_BW = D_RNN // RG_BLOCKS
RG_C = 8.0
D_SSM = 512
S5_GROUP = 16
S5_GROUPS = D_SSM // S5_GROUP
S5_STATE = 64
N_STATE = S5_GROUPS * S5_STATE
N_HEADS = 16
HEAD_DIM = D_MODEL // N_HEADS
D_FF = 2816
ALPHA = (2.0 * DEPTH) ** 0.25
LN_EPS = 1e-5

LANES = 128
GROUPS_PER_BLOCK = LANES // S5_GROUP
N_SSM_BLOCKS = D_SSM // LANES
STATE_BLOCK = GROUPS_PER_BLOCK * S5_STATE
FF_CHUNK = 256
N_FF_CHUNKS = D_FF // FF_CHUNK
NEG_BIG = -1e30
LOG2E = 1.4426950408889634
AUG_COLS = 6
VMEM_LIMIT = 56 * 1024 * 1024


def _cparams(n_axes=1, vmem=VMEM_LIMIT):
    return pltpu.CompilerParams(dimension_semantics=("arbitrary",) * n_axes,
                                vmem_limit_bytes=vmem)


def _split3(x):
    h1 = x.astype(BF16)
    r1 = x - h1.astype(F32)
    h2 = r1.astype(BF16)
    h3 = (r1 - h2.astype(F32)).astype(BF16)
    return h1, h2, h3


def _dotw(x, w):
    if w.dtype == BF16:
        return jnp.dot(x.astype(BF16), w, preferred_element_type=F32)
    xh = x.astype(BF16)
    xl = (x - xh.astype(F32)).astype(BF16)
    wh = w.astype(BF16)
    wl = (w - wh.astype(F32)).astype(BF16)
    return (jnp.dot(xh, wh, preferred_element_type=F32)
            + jnp.dot(xl, wh, preferred_element_type=F32)
            + jnp.dot(xh, wl, preferred_element_type=F32))


def _dot_exact01(m01, x):
    h1, h2, h3 = _split3(x)
    return (jnp.dot(m01, h1, preferred_element_type=F32)
            + jnp.dot(m01, h2, preferred_element_type=F32)
            + jnp.dot(m01, h3, preferred_element_type=F32))


def _gelu(x):
    return jax.nn.gelu(x)


def _sigmoid(x):
    return jax.nn.sigmoid(x)


def _log_sigmoid(x):
    return jnp.minimum(x, 0.0) - jnp.log1p(jnp.exp(-jnp.abs(x)))


def _expm1(x):
    u = jnp.exp(x)
    um1 = u - 1.0
    return jnp.where(um1 == 0.0, x, jnp.where(um1 == -1.0, -1.0, um1 * x / jnp.log(u)))


def _layer_norm(xf, g, b):
    mean = jnp.mean(xf, axis=-1, keepdims=True)
    xc = xf - mean
    var = jnp.mean(xc * xc, axis=-1, keepdims=True)
    return xc * lax.rsqrt(var + LN_EPS) * g + b


def _rglru_gates(xc, wa, ba, wx, bx, c_lam):
    r = _sigmoid(_dotw(xc, wa) + ba)
    i = _sigmoid(_dotw(xc, wx) + bx)
    log_a = c_lam * r
    a = jnp.exp(log_a)
    inp = jnp.sqrt(-_expm1(2.0 * log_a)) * (i * xc)
    return a, inp


def _scan_rows(a, b):
    n = a.shape[0]
    row = lax.broadcasted_iota(jnp.int32, a.shape, 0)
    d = 1
    while d < n:
        valid = row >= d
        b = b + jnp.where(valid, a * pltpu.roll(b, d, 0), 0.0)
        if 2 * d < n:
            a = jnp.where(valid, a * pltpu.roll(a, d, 0), a)
        d *= 2
    return b


def _scan_rows_const(lvl_re, lvl_im, pow_re, pow_im, br, bi, pr, pi):
    n, c = br.shape
    g = n // 8
    for k in range(3):
        d = 1 << k
        sr = pltpu.roll(br, d, 0).reshape(g, 8, c)
        si = pltpu.roll(bi, d, 0).reshape(g, 8, c)
        cr = lvl_re[k][None]
        ci = lvl_im[k][None]
        br, bi = ((br.reshape(g, 8, c) + (cr * sr - ci * si)).reshape(n, c),
                  (bi.reshape(g, 8, c) + (cr * si + ci * sr)).reshape(n, c))
    b3r = br.reshape(g, 8, c)
    b3i = bi.reshape(g, 8, c)
    outs_r, outs_i = [], []
    for gi in range(g):
        xr = b3r[gi] + (pow_re * pr - pow_im * pi)
        xi = b3i[gi] + (pow_re * pi + pow_im * pr)
        pr, pi = xr[7:8, :], xi[7:8, :]
        outs_r.append(xr)
        outs_i.append(xi)
    return jnp.concatenate(outs_r, axis=0), jnp.concatenate(outs_i, axis=0)


def _rglru_prompt_kernel(x_ref, w_ref, cw_ref, cb_ref, wa_ref, ba_ref, wx_ref, bx_ref, cl_ref,
                         ya_ref, tail_ref, h_ref, ext_scr, h_scr, *, tb):
    step = pl.program_id(0)

    @pl.when(step == 0)
    def _():
        ext_scr[0:8, :] = jnp.zeros((8, D_RNN), F32)
        h_scr[...] = jnp.zeros_like(h_scr)

    proj = _dotw(x_ref[...], w_ref[...])
    xa = proj[:, :D_RNN]
    ga = proj[:, D_RNN:]
    ext_scr[8:8 + tb, :] = xa
    xc = cb_ref[...] + xa * cw_ref[3:4, :]
    for j in range(3):
        xc = xc + ext_scr[pl.ds(5 + j, tb), :] * cw_ref[j:j + 1, :]
    tail = ext_scr[tb:tb + 8, :]
    ext_scr[0:8, :] = tail
    tail_ref[...] = tail

    a, inp = _rglru_gates(xc, wa_ref[...], ba_ref[...], wx_ref[...], bx_ref[...], cl_ref[...])
    row = lax.broadcasted_iota(jnp.int32, a.shape, 0)
    inp = inp + jnp.where(row == 0, a * h_scr[0:1, :], 0.0)
    h = _scan_rows(a, inp)
    h_last = jnp.broadcast_to(h[tb - 1:tb, :], (8, D_RNN))
    h_scr[...] = h_last
    h_ref[...] = h_last
    ya_ref[...] = h * _gelu(ga)


def _rglru_prompt(x, w_ag, cw, cb, wa, ba, wx, bx, c_lam, tb):
    t = x.shape[0]
    full = lambda shape: pl.BlockSpec(shape, lambda i: (0,) * len(shape))
    return pl.pallas_call(
        functools.partial(_rglru_prompt_kernel, tb=tb),
        grid=(t // tb,),
        in_specs=[pl.BlockSpec((tb, D_MODEL), lambda i: (i, 0)),
                  full(w_ag.shape), full(cw.shape), full(cb.shape), full(wa.shape), full(ba.shape),
                  full(wx.shape), full(bx.shape), full(c_lam.shape)],
        out_specs=[pl.BlockSpec((tb, D_RNN), lambda i: (i, 0)),
                   pl.BlockSpec((8, D_RNN), lambda i: (0, 0)),
                   pl.BlockSpec((8, D_RNN), lambda i: (0, 0))],
        out_shape=[jax.ShapeDtypeStruct((t, D_RNN), F32),
                   jax.ShapeDtypeStruct((8, D_RNN), F32),
                   jax.ShapeDtypeStruct((8, D_RNN), F32)],
        scratch_shapes=[pltpu.VMEM((tb + 8, D_RNN), F32), pltpu.VMEM((8, D_RNN), F32)],
        compiler_params=_cparams(),
        name="rglru_prompt",
    )(x, w_ag, cw, cb, wa, ba, wx, bx, c_lam)


def _s5_readout(u, xr_blocks, xi_blocks, cre_ref, cim_ref, d, wglu, bglu):
    ys = []
    for j in range(N_SSM_BLOCKS):
        ys.append(_dotw(xr_blocks[j], cre_ref[j]) - _dotw(xi_blocks[j], cim_ref[j]))
    y = jnp.concatenate(ys, axis=1) + d * u
    g = _gelu(y)
    return g * _sigmoid(_dotw(g, wglu) + bglu)


def _s5_prompt_kernel(x_ref, w_ref, bbr_ref, bbi_ref, lvr_ref, lvi_ref, pwr_ref, pwi_ref,
                      cre_ref, cim_ref, d_ref, wglu_ref, bglu_ref, yb_ref, sre_ref, sim_ref,
                      sr_scr, si_scr, *, tb):
    step = pl.program_id(0)

    @pl.when(step == 0)
    def _():
        sr_scr[...] = jnp.zeros_like(sr_scr)
        si_scr[...] = jnp.zeros_like(si_scr)

    u = _dotw(x_ref[...], w_ref[...])
    xr_blocks, xi_blocks = [], []
    for j in range(N_SSM_BLOCKS):
        cols = slice(j * STATE_BLOCK, (j + 1) * STATE_BLOCK)
        uj = u[:, j * LANES:(j + 1) * LANES]
        br = _dotw(uj, bbr_ref[j])
        bi = _dotw(uj, bbi_ref[j])
        xr, xi = _scan_rows_const(lvr_ref[:, :, cols], lvi_ref[:, :, cols], pwr_ref[:, cols],
                                  pwi_ref[:, cols], br, bi, sr_scr[0:1, cols], si_scr[0:1, cols])
        sr_scr[:, cols] = jnp.broadcast_to(xr[tb - 1:tb, :], (8, STATE_BLOCK))
        si_scr[:, cols] = jnp.broadcast_to(xi[tb - 1:tb, :], (8, STATE_BLOCK))
        xr_blocks.append(xr)
        xi_blocks.append(xi)
    sre_ref[...] = sr_scr[...]
    sim_ref[...] = si_scr[...]
    yb_ref[...] = _s5_readout(u, xr_blocks, xi_blocks, cre_ref, cim_ref, d_ref[...],
                              wglu_ref[...], bglu_ref[...])


def _s5_prompt(x, w_b, bbr, bbi, lvl_re, lvl_im, pow_re, pow_im, cre, cim, d, wglu, bglu, tb):
    t = x.shape[0]
    full = lambda shape: pl.BlockSpec(shape, lambda i: (0,) * len(shape))
    return pl.pallas_call(
        functools.partial(_s5_prompt_kernel, tb=tb),
        grid=(t // tb,),
        in_specs=[pl.BlockSpec((tb, D_MODEL), lambda i: (i, 0)),
                  full(w_b.shape), full(bbr.shape), full(bbi.shape), full(lvl_re.shape),
                  full(lvl_im.shape), full(pow_re.shape), full(pow_im.shape), full(cre.shape),
                  full(cim.shape), full(d.shape), full(wglu.shape), full(bglu.shape)],
        out_specs=[pl.BlockSpec((tb, D_SSM), lambda i: (i, 0)),
                   pl.BlockSpec((8, N_STATE), lambda i: (0, 0)),
                   pl.BlockSpec((8, N_STATE), lambda i: (0, 0))],
        out_shape=[jax.ShapeDtypeStruct((t, D_SSM), F32),
                   jax.ShapeDtypeStruct((8, N_STATE), F32),
                   jax.ShapeDtypeStruct((8, N_STATE), F32)],
        scratch_shapes=[pltpu.VMEM((8, N_STATE), F32), pltpu.VMEM((8, N_STATE), F32)],
        compiler_params=_cparams(),
        name="s5_prompt",
    )(x, w_b, bbr, bbi, lvl_re, lvl_im, pow_re, pow_im, cre, cim, d, wglu, bglu)


def _out_ln_kernel(*refs, n_in):
    a_refs = refs[:n_in]
    w_refs = refs[n_in:2 * n_in]
    x_ref, g_ref, b_ref, o_ref = refs[2 * n_in:]
    y = ALPHA * x_ref[...]
    for a_ref, w_ref in zip(a_refs, w_refs):
        y = y + _dotw(a_ref[...], w_ref[...])
    o_ref[...] = _layer_norm(y, g_ref[...], b_ref[...])


def _out_ln(acts, weights, x, g, b, tm, name):
    m = x.shape[0]
    n_in = len(acts)
    full = lambda shape: pl.BlockSpec(shape, lambda i: (0,) * len(shape))
    in_specs = ([pl.BlockSpec((tm, a.shape[1]), lambda i: (i, 0)) for a in acts]
                + [full(w.shape) for w in weights]
                + [pl.BlockSpec((tm, D_MODEL), lambda i: (i, 0)), full(g.shape), full(b.shape)])
    return pl.pallas_call(
        functools.partial(_out_ln_kernel, n_in=n_in),
        grid=(m // tm,),
        in_specs=in_specs,
        out_specs=pl.BlockSpec((tm, D_MODEL), lambda i: (i, 0)),
        out_shape=jax.ShapeDtypeStruct((m, D_MODEL), F32),
        compiler_params=_cparams(),
        name=name,
    )(*acts, *weights, x, g, b)


def _ffn_prompt_kernel(*refs, n_in, tm):
    a_refs = refs[:n_in]
    w_refs = refs[n_in:2 * n_in]
    (x_ref, g0_ref, b0_ref, wup_ref, wgate_ref, cw_ref, cb_ref, wdown_ref, g_ref, b_ref,
     o_ref, tail_ref, h_scr, carry_scr) = refs[2 * n_in:]
    step = pl.program_id(0)

    @pl.when(step == 0)
    def _():
        carry_scr[...] = jnp.zeros_like(carry_scr)

    y = ALPHA * x_ref[...]
    for a_ref, w_ref in zip(a_refs, w_refs):
        y = y + _dotw(a_ref[...], w_ref[...])
    x = _layer_norm(y, g0_ref[...], b0_ref[...])
    xb = x.astype(BF16)
    row = lax.broadcasted_iota(jnp.int32, (tm, FF_CHUNK), 0)
    for c in range(N_FF_CHUNKS):
        cols = slice(c * FF_CHUNK, (c + 1) * FF_CHUNK)
        up = jnp.dot(xb, wup_ref[:, cols], preferred_element_type=F32)
        gate = jnp.dot(xb, wgate_ref[:, cols], preferred_element_type=F32)
        prev1 = carry_scr[7:8, cols]
        prev2 = carry_scr[6:7, cols]
        up1 = jnp.where(row >= 1, pltpu.roll(up, 1, 0), prev1)
        up2 = jnp.where(row >= 2, pltpu.roll(up, 2, 0), jnp.where(row == 1, prev1, prev2))
        hc = (cb_ref[:, cols] + up2 * cw_ref[0:1, cols] + up1 * cw_ref[1:2, cols]
              + up * cw_ref[2:3, cols])
        h_scr[:, cols] = (_gelu(hc) * gate).astype(BF16)
        carry_scr[:, cols] = up[tm - 8:tm, :]
    tail_ref[...] = carry_scr[...]
    f = jnp.dot(h_scr[...], wdown_ref[...], preferred_element_type=F32)
    o_ref[...] = _layer_norm(ALPHA * x + f, g_ref[...], b_ref[...])


def _ffn_prompt(acts, weights, x, g0, b0, wup, wgate, cw, cb, wdown, g, b, tm):
    t = x.shape[0]
    n_in = len(acts)
    const = lambda a: pl.BlockSpec(a.shape, lambda i: (0,) * a.ndim,
                                   pipeline_mode=pl.Buffered(1))
    rows = lambda a: pl.BlockSpec((tm, a.shape[1]), lambda i: (i, 0))
    consts = [g0, b0, wup, wgate, cw, cb, wdown, g, b]
    return pl.pallas_call(
        functools.partial(_ffn_prompt_kernel, n_in=n_in, tm=tm),
        grid=(t // tm,),
        in_specs=([rows(a) for a in acts] + [const(w) for w in weights] + [rows(x)]
                  + [const(c) for c in consts]),
        out_specs=[pl.BlockSpec((tm, D_MODEL), lambda i: (i, 0)),
                   pl.BlockSpec((8, D_FF), lambda i: (0, 0))],
        out_shape=[jax.ShapeDtypeStruct((t, D_MODEL), F32),
                   jax.ShapeDtypeStruct((8, D_FF), F32)],
        scratch_shapes=[pltpu.VMEM((tm, D_FF), BF16), pltpu.VMEM((8, D_FF), F32)],
        compiler_params=_cparams(),
        name="ffn_prompt",
    )(*acts, *weights, x, *consts)


def _aug_lhs(f3):
    h1, h2, h3 = (h.astype(F32) for h in _split3(f3))
    lane = lax.broadcasted_iota(jnp.int32, f3.shape, 1)
    pieces = jnp.where(lane < N_HEADS, h1, jnp.where(lane < 2 * N_HEADS, h2, jnp.where(
        lane < 3 * N_HEADS, h3, jnp.where(lane < 4 * N_HEADS, 1.0, 0.0))))
    return pieces.astype(BF16)


def _qkv_prompt_kernel(x_ref, w_ref, wf_ref, bf_ref, selq_ref, selk_ref,
                       k_ref, v_ref, lf_ref, qa_ref, ka_ref, va_ref, f_scr, *, tm):
    step = pl.program_id(0)

    @pl.when(step == 0)
    def _():
        f_scr[...] = jnp.zeros_like(f_scr)

    x = x_ref[...]
    qkv = _dotw(x, w_ref[...])
    k = qkv[:, D_MODEL:2 * D_MODEL]
    v = qkv[:, 2 * D_MODEL:]
    k_ref[...] = k
    v_ref[...] = v
    logf3 = _log_sigmoid(_dotw(x, wf_ref[...]) + bf_ref[...])
    lf_ref[...] = logf3

    r_i = lax.broadcasted_iota(jnp.int32, (tm, tm), 0)
    c_i = lax.broadcasted_iota(jnp.int32, (tm, tm), 1)
    tri = jnp.where(r_i >= c_i, 1.0, 0.0).astype(BF16)
    f3 = _dot_exact01(tri, logf3) + f_scr[0:1, :]
    f_scr[...] = jnp.broadcast_to(f3[tm - 1:tm, :], f_scr.shape)

    lhs = _aug_lhs(f3 * LOG2E)
    augq = jnp.dot(lhs, selq_ref[...], preferred_element_type=F32)
    augk = jnp.dot(lhs, selk_ref[...], preferred_element_type=F32)
    lane = lax.broadcasted_iota(jnp.int32, (tm, LANES), 1)
    low = lane < HEAD_DIM
    vtail = jnp.where(lane == HEAD_DIM, 1.0, 0.0)
    scale = HEAD_DIM ** -0.5 * LOG2E
    for hp in range(N_HEADS // 2):
        cols = slice(hp * LANES, (hp + 1) * LANES)
        qb = qkv[:, cols] * scale
        kb = k[:, cols]
        vb = v[:, cols]
        for odd in range(2):
            h = 2 * hp + odd
            acols = slice(h * LANES, (h + 1) * LANES)
            if odd:
                qb, kb, vb = (pltpu.roll(qb, HEAD_DIM, 1), pltpu.roll(kb, HEAD_DIM, 1),
                              pltpu.roll(vb, HEAD_DIM, 1))
            qa_ref[h] = jnp.where(low, qb, augq[:, acols]).astype(BF16)
            ka_ref[h] = jnp.where(low, kb, augk[:, acols]).astype(BF16)
            va_ref[h] = jnp.where(low, vb, vtail).astype(BF16)


def _qkv_prompt(x, w_qkv, wf3, bf3, selq, selk, tm):
    t = x.shape[0]
    full = lambda shape: pl.BlockSpec(shape, lambda i: (0,) * len(shape))
    head_major = jax.ShapeDtypeStruct((N_HEADS, t, LANES), BF16)
    head_spec = pl.BlockSpec((N_HEADS, tm, LANES), lambda i: (0, i, 0))
    return pl.pallas_call(
        functools.partial(_qkv_prompt_kernel, tm=tm),
        grid=(t // tm,),
        in_specs=[pl.BlockSpec((tm, D_MODEL), lambda i: (i, 0)),
                  full(w_qkv.shape), full(wf3.shape), full(bf3.shape), full(selq.shape),
                  full(selk.shape)],
        out_specs=[pl.BlockSpec((tm, D_MODEL), lambda i: (i, 0)),
                   pl.BlockSpec((tm, D_MODEL), lambda i: (i, 0)),
                   pl.BlockSpec((tm, LANES), lambda i: (i, 0)),
                   head_spec, head_spec, head_spec],
        out_shape=[jax.ShapeDtypeStruct((t, D_MODEL), F32),
                   jax.ShapeDtypeStruct((t, D_MODEL), F32),
                   jax.ShapeDtypeStruct((t, LANES), F32),
                   head_major, head_major, head_major],
        scratch_shapes=[pltpu.VMEM((8, LANES), F32)],
        compiler_params=_cparams(),
        name="qkv_prompt",
    )(x, w_qkv, wf3, bf3, selq, selk)


def _repeat_rows(x):
    return jnp.broadcast_to(x[:, None, :], (N_HEADS, HEAD_DIM, x.shape[1])).reshape(
        D_MODEL, x.shape[1])


def _head_sum(x):
    return jnp.sum(x.reshape(N_HEADS, HEAD_DIM, x.shape[1]), axis=1)


def _flash_kernel(it_ref, jt_ref, pt_ref, q_ref, k_ref, v_ref, qb_ref, kn_ref, vn_ref, gt_ref,
                  *refs, tq, tk, qc, pp, n_tri, steps_per_seq, n_seq):
    k_pages = refs[:pp]
    v_pages = refs[pp:2 * pp]
    lf_pages = refs[2 * pp:3 * pp]
    o_ref, os_ref, m_scr, acc_scr, sm_scr, sl_scr, sacc_scr, scarry_scr = refs[3 * pp:]
    s_idx = pl.program_id(1)
    i = it_ref[s_idx]
    j = jt_ref[s_idx]
    t = pl.program_id(0) * n_tri + s_idx
    grp = t % steps_per_seq
    seq_active = t < n_seq * steps_per_seq
    first_group = grp == 0

    @pl.when(t == 0)
    def _():
        sm_scr[...] = jnp.zeros_like(sm_scr)
        sl_scr[...] = jnp.zeros_like(sl_scr)
        sacc_scr[...] = jnp.zeros_like(sacc_scr)
        scarry_scr[...] = jnp.zeros_like(scarry_scr)

    @pl.when(j == 0)
    def _():
        m_scr[...] = jnp.full(m_scr.shape, NEG_BIG, F32)
        acc_scr[...] = jnp.zeros_like(acc_scr)

    r_i = lax.broadcasted_iota(jnp.int32, (PAGE_SIZE, PAGE_SIZE), 0)
    c_i = lax.broadcasted_iota(jnp.int32, (PAGE_SIZE, PAGE_SIZE), 1)
    later = jnp.where(r_i > c_i, 1.0, 0.0).astype(BF16)

    def sample_pieces():
        st = {}

        def qk(r):
            def run():
                kt = k_pages[r][0].reshape(D_MODEL, PAGE_SIZE)
                st['s%d' % r] = _head_sum(kt * qb_ref[0])
            return run

        def softmax():
            gt = gt_ref[0]
            carry = jnp.where(first_group, 0.0, scarry_scr[:, 0:1])
            logits = []
            for r in range(pp):
                lf = lf_pages[r][0]
                suf = sum(jnp.dot(pc, later, preferred_element_type=F32) for pc in _split3(lf))
                logits.append(st['s%d' % r] + suf + (carry + gt))
                carry = carry + jnp.sum(lf, axis=1, keepdims=True)
            scarry_scr[...] = jnp.broadcast_to(carry, scarry_scr.shape)
            mx = logits[0]
            for lg in logits[1:]:
                mx = jnp.maximum(mx, lg)
            m_prev = jnp.where(first_group, NEG_BIG, sm_scr[:, 0:1])
            l_prev = jnp.where(first_group, 0.0, sl_scr[:, 0:1])
            m_next = jnp.maximum(m_prev, jnp.max(mx, axis=1, keepdims=True))
            alpha = jnp.exp(m_prev - m_next)
            ps = [jnp.exp(lg - m_next) for lg in logits]
            psum = ps[0]
            for pblk in ps[1:]:
                psum = psum + pblk
            sl_scr[...] = jnp.broadcast_to(
                alpha * l_prev + jnp.sum(psum, axis=1, keepdims=True), sl_scr.shape)
            sm_scr[...] = jnp.broadcast_to(m_next, sm_scr.shape)
            st['ps'] = ps
            st['alpha'] = jnp.broadcast_to(alpha, (N_HEADS, PAGE_SIZE))

        def pv(r):
            def run():
                vt = v_pages[r][0].reshape(D_MODEL, PAGE_SIZE)
                if r == 0:
                    st['acc'] = _repeat_rows(st['alpha']) * sacc_scr[...]
                st['acc'] = st['acc'] + _repeat_rows(st['ps'][r]) * vt
                if r == pp - 1:
                    sacc_scr[...] = st['acc']
            return run

        return [qk(r) for r in range(pp)] + [softmax] + [pv(r) for r in range(pp)]

    def accumulate(masked):
        chunks = [(hh, c) for hh in range(2) for c in range(tq // qc)]
        pieces = sample_pieces()

        def n_keys(c):
            return (c + 1) * qc if masked else tk

        def scores(hh, c):
            rows = slice(c * qc, (c + 1) * qc)
            nk = n_keys(c)
            s = lax.dot_general(q_ref[hh, rows, :], k_ref[hh, 0:nk, :], (((1,), (1,)), ((), ())),
                                preferred_element_type=F32)
            if masked:
                qpos = c * qc + lax.broadcasted_iota(jnp.int32, (qc, nk), 0)
                kpos = lax.broadcasted_iota(jnp.int32, (qc, nk), 1)
                s = jnp.where(kpos <= qpos, s, NEG_BIG)
            return s

        ahead = 2
        pending = [scores(*chunks[n]) for n in range(ahead)]
        per_chunk = -(-len(pieces) // len(chunks))
        for n, (hh, c) in enumerate(chunks):
            s = pending.pop(0)
            if n + ahead < len(chunks):
                pending.append(scores(*chunks[n + ahead]))
            for piece in pieces[n * per_chunk:(n + 1) * per_chunk]:
                piece()
            rows = slice(c * qc, (c + 1) * qc)
            m_prev = m_scr[hh, rows, :]
            m_next = jnp.maximum(m_prev, jnp.max(s, axis=1, keepdims=True))
            alpha = jnp.exp2(m_prev - m_next)
            p = jnp.exp2(s - m_next[:, 0:1]).astype(BF16)
            acc_scr[hh, rows, :] = alpha * acc_scr[hh, rows, :] + jnp.dot(
                p, v_ref[hh, 0:n_keys(c), :], preferred_element_type=F32)
            m_scr[hh, rows, :] = m_next

    assert tq == tk, "the diagonal-block key trimming assumes square blocks"
    crosses_diagonal = j == i

    @pl.when(crosses_diagonal)
    def _():
        accumulate(True)

    @pl.when(jnp.logical_not(crosses_diagonal))
    def _():
        accumulate(False)

    @pl.when(j == ((i + 1) * tq - 1) // tk)
    def _():
        lane = lax.broadcasted_iota(jnp.int32, (tq, LANES), 1)
        outs = []
        for hh in range(2):
            acc = acc_scr[hh]
            outs.append(acc / acc[:, HEAD_DIM:HEAD_DIM + 1])
        o_ref[...] = jnp.where(lane < HEAD_DIM, outs[0], pltpu.roll(outs[1], HEAD_DIM, 1))

    @pl.when(jnp.logical_and(seq_active, grp == steps_per_seq - 1))
    def _():
        qcol = qb_ref[0][:, 0:1]
        s_self = _head_sum(qcol * kn_ref[0])
        m_prev = sm_scr[:, 0:1]
        m_fin = jnp.maximum(m_prev, s_self)
        a_past = jnp.exp(m_prev - m_fin)
        a_self = jnp.exp(s_self - m_fin)
        l_fin = a_past * sl_scr[:, 0:1] + a_self
        past = jnp.sum(sacc_scr[...], axis=1, keepdims=True)
        os_ref[0] = (_repeat_rows(a_past) * past + _repeat_rows(a_self) * vn_ref[0]) / _repeat_rows(
            l_fin)


def _flash(qa, ka, va, tq, tk, qc, page_table, q_s, k_s, v_s, logf_s, cache_k, cache_v,
           cache_logf, pp):
    t = ka.shape[1]
    nq = t // tq
    i_list, j_list = [], []
    for i in range(nq):
        for j in range(((i + 1) * tq - 1) // tk + 1):
            i_list.append(i)
            j_list.append(j)
    n_tri = len(i_list)
    it = jnp.asarray(i_list, jnp.int32)
    jt = jnp.asarray(j_list, jnp.int32)

    n_seq, n_pages = page_table.shape
    steps_per_seq = n_pages // pp
    n_steps = (N_HEADS // 2) * n_tri
    assert n_pages % pp == 0 and n_seq * steps_per_seq <= n_steps
    pt = page_table.reshape(-1)
    ck = jnp.transpose(cache_k, (0, 2, 3, 1))
    cv = jnp.transpose(cache_v, (0, 2, 3, 1))
    clf = jnp.transpose(cache_logf, (0, 2, 1))
    scale = HEAD_DIM ** -0.5
    qb = jnp.broadcast_to((q_s * scale)[:, :, None], (n_seq, D_MODEL, PAGE_SIZE))
    col = lambda a: a.reshape(n_seq, D_MODEL, 1)

    def seq_of(hp, s):
        return jnp.minimum((hp * n_tri + s) // steps_per_seq, n_seq - 1)

    def page_map(r, ndim):
        def index(hp, s, it, jt, pt):
            step = hp * n_tri + s
            grp = jnp.where(step < n_seq * steps_per_seq, step % steps_per_seq,
                            steps_per_seq - 1)
            page = pt[seq_of(hp, s) * n_pages + (n_pages - 1 - (grp * pp + r))]
            return (page,) + (0,) * ndim
        return index

    per_seq = lambda shape: pl.BlockSpec((1,) + shape,
                                         lambda hp, s, it, jt, pt: (seq_of(hp, s), 0, 0))
    kv_block = (1, N_HEADS, HEAD_DIM, PAGE_SIZE)
    in_specs = [pl.BlockSpec((2, tq, LANES), lambda hp, s, it, jt, pt: (hp, it[s], 0)),
                pl.BlockSpec((2, tk, LANES), lambda hp, s, it, jt, pt: (hp, jt[s], 0)),
                pl.BlockSpec((2, tk, LANES), lambda hp, s, it, jt, pt: (hp, jt[s], 0)),
                per_seq((D_MODEL, PAGE_SIZE)), per_seq((D_MODEL, 1)), per_seq((D_MODEL, 1)),
                per_seq((N_HEADS, 1))]
    in_specs += [pl.BlockSpec(kv_block, page_map(r, 3)) for r in range(pp)]
    in_specs += [pl.BlockSpec(kv_block, page_map(r, 3)) for r in range(pp)]
    in_specs += [pl.BlockSpec((1, N_HEADS, PAGE_SIZE), page_map(r, 2)) for r in range(pp)]
    grid_spec = pltpu.PrefetchScalarGridSpec(
        num_scalar_prefetch=3,
        grid=(N_HEADS // 2, n_tri),
        in_specs=in_specs,
        out_specs=[pl.BlockSpec((tq, LANES), lambda hp, s, it, jt, pt: (it[s], hp)),
                   per_seq((D_MODEL, 1))],
        scratch_shapes=[pltpu.VMEM((2, tq, LANES), F32), pltpu.VMEM((2, tq, LANES), F32),
                        pltpu.VMEM((N_HEADS, LANES), F32), pltpu.VMEM((N_HEADS, LANES), F32),
                        pltpu.VMEM((D_MODEL, PAGE_SIZE), F32), pltpu.VMEM((N_HEADS, LANES), F32)],
    )
    o, o_s = pl.pallas_call(
        functools.partial(_flash_kernel, tq=tq, tk=tk, qc=qc, pp=pp, n_tri=n_tri,
                          steps_per_seq=steps_per_seq, n_seq=n_seq),
        grid_spec=grid_spec,
        out_shape=[jax.ShapeDtypeStruct((t, D_MODEL), F32),
                   jax.ShapeDtypeStruct((n_seq, D_MODEL, 1), F32)],
        compiler_params=_cparams(2),
        name="flash_prompt",
    )(it, jt, pt, qa, ka, va, qb, col(k_s), col(v_s), logf_s.reshape(n_seq, N_HEADS, 1),
      *([ck] * pp), *([cv] * pp), *([clf] * pp))
    return o, o_s.reshape(n_seq, D_MODEL)


def _mix0_sample_kernel(x_ref, w_ref, c0_ref, c1_ref, c2_ref, cw_ref, cb_ref, wa_ref, ba_ref,
                        wx_ref, bx_ref, cl_ref, h0_ref, sr_ref, si_ref, bbr_ref, bbi_ref,
                        ar_ref, ai_ref, cre_ref, cim_ref, d_ref, wglu_ref, bglu_ref,
                        ya_ref, yb_ref, xa_ref, h_ref, xr_ref, xi_ref):
    proj = _dotw(x_ref[...], w_ref[...])
    xa = proj[:, :D_RNN]
    ga = proj[:, D_RNN:2 * D_RNN]
    u = proj[:, 2 * D_RNN:]
    xa_ref[...] = xa
    xc = (cb_ref[...] + c0_ref[...] * cw_ref[0:1, :] + c1_ref[...] * cw_ref[1:2, :]
          + c2_ref[...] * cw_ref[2:3, :] + xa * cw_ref[3:4, :])
    a, inp = _rglru_gates(xc, wa_ref[...], ba_ref[...], wx_ref[...], bx_ref[...], cl_ref[...])
    h = a * h0_ref[...] + inp
    h_ref[...] = h
    ya_ref[...] = h * _gelu(ga)

    xr_blocks, xi_blocks = [], []
    for j in range(N_SSM_BLOCKS):
        cols = slice(j * STATE_BLOCK, (j + 1) * STATE_BLOCK)
        uj = u[:, j * LANES:(j + 1) * LANES]
        ar = ar_ref[:, cols]
        ai = ai_ref[:, cols]
        pr = sr_ref[:, cols]
        pi = si_ref[:, cols]
        xr = _dotw(uj, bbr_ref[j]) + (ar * pr - ai * pi)
        xi = _dotw(uj, bbi_ref[j]) + (ar * pi + ai * pr)
        xr_ref[:, cols] = xr
        xi_ref[:, cols] = xi
        xr_blocks.append(xr)
        xi_blocks.append(xi)
    yb_ref[...] = _s5_readout(u, xr_blocks, xi_blocks, cre_ref, cim_ref, d_ref[...],
                              wglu_ref[...], bglu_ref[...])


def _mix0_sample(*args):
    n = args[0].shape[0]
    shapes = [(n, D_RNN), (n, D_SSM), (n, D_RNN), (n, D_RNN), (n, N_STATE), (n, N_STATE)]
    return pl.pallas_call(
        _mix0_sample_kernel,
        out_shape=[jax.ShapeDtypeStruct(s, F32) for s in shapes],
        compiler_params=pltpu.CompilerParams(vmem_limit_bytes=VMEM_LIMIT),
        name="mix0_sample",
    )(*args)


def _ffn_sample_kernel(x_ref, wup_ref, wgate_ref, b0_ref, b1_ref, cw_ref, cb_ref, wdown_ref,
                       g_ref, b_ref, o_ref, up_ref, acc_scr):
    c = pl.program_id(0)

    @pl.when(c == 0)
    def _():
        acc_scr[...] = jnp.zeros_like(acc_scr)

    x = x_ref[...]
    up = _dotw(x, wup_ref[...])
    gate = _dotw(x, wgate_ref[...])
    up_ref[...] = up
    hc = (cb_ref[...] + b0_ref[...] * cw_ref[0:1, :] + b1_ref[...] * cw_ref[1:2, :]
          + up * cw_ref[2:3, :])
    acc_scr[...] += _dotw(_gelu(hc) * gate, wdown_ref[...])

    @pl.when(c == pl.num_programs(0) - 1)
    def _():
        o_ref[...] = _layer_norm(ALPHA * x + acc_scr[...], g_ref[...], b_ref[...])


def _ffn_sample(x, wup, wgate, buf0, buf1, cw, cb, wdown, g, b):
    n = x.shape[0]
    fc = FF_CHUNK
    col = lambda rows: pl.BlockSpec((rows, fc), lambda c: (0, c))
    full = lambda shape: pl.BlockSpec(shape, lambda c: (0,) * len(shape))
    return pl.pallas_call(
        _ffn_sample_kernel,
        grid=(D_FF // fc,),
        in_specs=[full(x.shape), col(D_MODEL), col(D_MODEL), col(n), col(n), col(cw.shape[0]),
                  col(1), pl.BlockSpec((fc, D_MODEL), lambda c: (c, 0)), full(g.shape),
                  full(b.shape)],
        out_specs=[full((n, D_MODEL)), col(n)],
        out_shape=[jax.ShapeDtypeStruct((n, D_MODEL), F32), jax.ShapeDtypeStruct((n, D_FF), F32)],
        scratch_shapes=[pltpu.VMEM((n, D_MODEL), F32)],
        compiler_params=_cparams(),
        name="ffn_sample",
    )(x, wup, wgate, buf0, buf1, cw, cb, wdown, g, b)


def _qkv_sample_kernel(x_ref, w_ref, wf_ref, bf_ref, o_ref, lf_ref):
    x = x_ref[...]
    o_ref[...] = _dotw(x, w_ref[...])

    @pl.when(pl.program_id(0) == 0)
    def _():
        lf_ref[...] = _log_sigmoid(_dotw(x, wf_ref[...]) + bf_ref[...])


def _qkv_sample(x, w_in1, wf, bf):
    n = x.shape[0]
    full = lambda shape: pl.BlockSpec(shape, lambda c: (0,) * len(shape))
    return pl.pallas_call(
        _qkv_sample_kernel,
        grid=(3,),
        in_specs=[full(x.shape), pl.BlockSpec((D_MODEL, D_MODEL), lambda c: (0, c)),
                  full(wf.shape), full(bf.shape)],
        out_specs=[pl.BlockSpec((n, D_MODEL), lambda c: (0, c)), full((n, LANES))],
        out_shape=[jax.ShapeDtypeStruct((n, 3 * D_MODEL), F32),
                   jax.ShapeDtypeStruct((n, LANES), F32)],
        compiler_params=_cparams(),
        name="qkv_sample",
    )(x, w_in1, wf, bf)


def _block_diag_dense(w):
    nb, n, _ = w.shape
    eye = jnp.eye(nb, dtype=w.dtype)
    return jnp.einsum('hij,hk->hikj', w, eye).reshape(nb * n, nb * n)


def _s5_discretize(lam_re, lam_im, log_dt, b_re, b_im):
    lr = jnp.minimum(lam_re, -1e-4)
    li = lam_im
    dt = jnp.exp(log_dt)[:, None]
    mag = jnp.exp(lr * dt)
    ab_re = mag * jnp.cos(li * dt)
    ab_im = mag * jnp.sin(li * dt)
    den = lr * lr + li * li
    nr = ab_re - 1.0
    zr = (nr * lr + ab_im * li) / den
    zi = (ab_im * lr - nr * li) / den
    bb_re = zr[..., None] * b_re - zi[..., None] * b_im
    bb_im = zr[..., None] * b_im + zi[..., None] * b_re
    return ab_re, ab_im, bb_re, bb_im


def _s5_scan_tables(a_re, a_im):
    ar = a_re.reshape(1, N_STATE)
    ai = a_im.reshape(1, N_STATE)
    pows = [(ar, ai)]
    for _ in range(7):
        qr, qi = pows[-1]
        pows.append((qr * ar - qi * ai, qr * ai + qi * ar))
    pow_re = jnp.concatenate([q[0] for q in pows], axis=0)
    pow_im = jnp.concatenate([q[1] for q in pows], axis=0)
    sub = jnp.arange(8)[:, None]
    lvl_re = jnp.stack([jnp.where(sub >= d, pows[d - 1][0], 0.0) for d in (1, 2, 4)])
    lvl_im = jnp.stack([jnp.where(sub >= d, pows[d - 1][1], 0.0) for d in (1, 2, 4)])
    return dict(lvl_re=lvl_re, lvl_im=lvl_im, pow_re=pow_re, pow_im=pow_im)


def _s5_in_blocks(bb):
    eye = jnp.eye(GROUPS_PER_BLOCK, dtype=bb.dtype)
    b4 = bb.transpose(0, 2, 1).reshape(N_SSM_BLOCKS, GROUPS_PER_BLOCK, S5_GROUP, S5_STATE)
    return jnp.einsum('jgmp,gh->jgmhp', b4, eye).reshape(N_SSM_BLOCKS, LANES, STATE_BLOCK)


def _s5_out_blocks(c):
    eye = jnp.eye(GROUPS_PER_BLOCK, dtype=c.dtype)
    c4 = c.reshape(N_SSM_BLOCKS, GROUPS_PER_BLOCK, S5_GROUP, S5_STATE)
    return jnp.einsum('jgmp,gh->jgphm', c4, eye).reshape(N_SSM_BLOCKS, STATE_BLOCK, LANES)


def _aug_selectors():
    rows = jnp.arange(LANES)[:, None]
    cols = jnp.arange(N_HEADS * LANES)[None, :]
    head = cols // LANES
    c = cols % LANES - HEAD_DIM
    piece = rows // N_HEADS
    rhead = rows % N_HEADS
    same = (rhead == head) & (rows < 4 * N_HEADS)
    f_piece = same & (piece < 3)
    ones_piece = same & (piece == 3)
    selq = jnp.where(f_piece & (c == piece), 1.0, 0.0) + jnp.where(
        ones_piece & (c >= 3) & (c < AUG_COLS), 1.0, 0.0)
    selk = jnp.where(ones_piece & (c >= 0) & (c < 3), 1.0, 0.0) - jnp.where(
        f_piece & (c == piece + 3), 1.0, 0.0)
    return selq.astype(BF16), selk.astype(BF16)


def _forget_cols(w_in1, b_fgt, dtype):
    wf = w_in1[:, 3 * D_MODEL:]
    pad = jnp.zeros((D_MODEL, LANES - 3 * N_HEADS), w_in1.dtype)
    wf3 = jnp.concatenate([wf, wf, wf, pad], axis=1).astype(dtype)
    bf3 = jnp.concatenate([b_fgt, b_fgt, b_fgt, jnp.zeros((LANES - 3 * N_HEADS,), F32)])[None, :]
    return wf3, bf3


def _row(v):
    return v.reshape(1, -1)


def _pick(t, pref):
    return pref if t % pref == 0 else t


def _forward(x, xs, rg_conv, rg_h, s5_re, s5_im, cache_k, cache_v, cache_logf, ffn_conv,
             page_table, p, s5p, tiles, pages_per_step):
    t = x.shape[0]
    n = xs.shape[0]
    bf = lambda w: w.astype(BF16)
    ln = lambda a, layer, which: a[layer, which][None]

    tb = _pick(t, tiles['scan'])
    ya, tail, h_last = _rglru_prompt(
        x, bf(p['w_in0'][:, :2 * D_RNN]), p['rg_conv_w'], _row(p['rg_conv_b']),
        bf(s5p['wa_bd']), _row(p['rg_ba']), bf(s5p['wx_bd']), _row(p['rg_bx']), s5p['c_lam'], tb)
    yb, s_re, s_im = _s5_prompt(
        x, bf(p['w_in0'][:, 2 * D_RNN:]), bf(s5p['bbr']), bf(s5p['bbi']), s5p['lvl_re'],
        s5p['lvl_im'], s5p['pow_re'], s5p['pow_im'],
        bf(s5p['cre']), bf(s5p['cim']), _row(p['s5_d']), bf(p['s5_w_glu']), _row(p['s5_b_glu']), tb)
    tm = _pick(t, tiles['rows'])
    x2, ffn_tail0 = _ffn_prompt(
        [ya, yb], [bf(p['w_out0'][:D_RNN]), bf(p['w_out0'][D_RNN:])], x,
        ln(p['ln_g'], 0, 0), ln(p['ln_b'], 0, 0), bf(p['ffn_w_up'][0]), bf(p['ffn_w_gate'][0]),
        p['ffn_conv_w'][0], p['ffn_conv_b'][0][None], bf(p['ffn_w_down'][0]),
        ln(p['ln_g'], 0, 1), ln(p['ln_b'], 0, 1), tm)
    wf3, bf3 = _forget_cols(p['w_in1'], p['b_fgt'], BF16)
    selq, selk = _aug_selectors()
    k, v, lf, qa, ka, va = _qkv_prompt(x2, bf(p['w_in1'][:, :3 * D_MODEL]), wf3, bf3, selq, selk,
                                       _pick(t, tiles['qkv']))

    ya_s, yb_s, xa_s, h_new, xr, xi = _mix0_sample(
        xs, p['w_in0'], rg_conv[:, 0], rg_conv[:, 1], rg_conv[:, 2], p['rg_conv_w'],
        _row(p['rg_conv_b']), s5p['wa_bd'], _row(p['rg_ba']), s5p['wx_bd'], _row(p['rg_bx']),
        s5p['c_lam'], rg_h, s5_re.reshape(n, N_STATE), s5_im.reshape(n, N_STATE),
        s5p['bbr'], s5p['bbi'], s5p['a_re'], s5p['a_im'], s5p['cre'], s5p['cim'],
        _row(p['s5_d']), p['s5_w_glu'], _row(p['s5_b_glu']))
    xs1 = _out_ln([ya_s, yb_s], [p['w_out0'][:D_RNN], p['w_out0'][D_RNN:]], xs,
                  ln(p['ln_g'], 0, 0), ln(p['ln_b'], 0, 0), n, "out0_sample")
    xs2, up0 = _ffn_sample(xs1, p['ffn_w_up'][0], p['ffn_w_gate'][0], ffn_conv[0, :, 0],
                           ffn_conv[0, :, 1], p['ffn_conv_w'][0], p['ffn_conv_b'][0][None],
                           p['ffn_w_down'][0], ln(p['ln_g'], 0, 1), ln(p['ln_b'], 0, 1))
    wf3_s, bf3_s = _forget_cols(p['w_in1'], p['b_fgt'], F32)
    qkv_s, lf3_s = _qkv_sample(xs2, p['w_in1'], wf3_s, bf3_s)
    q_s = qkv_s[:, :D_MODEL]
    k_s = qkv_s[:, D_MODEL:2 * D_MODEL]
    v_s = qkv_s[:, 2 * D_MODEL:]
    logf_s = lf3_s[:, :N_HEADS]

    o, o_s = _flash(qa, ka, va, _pick(t, tiles['tq']), _pick(t, tiles['tk']), tiles['qc'],
                    page_table, q_s, k_s, v_s, logf_s, cache_k, cache_v, cache_logf,
                    pages_per_step)

    x4, ffn_tail1 = _ffn_prompt(
        [o], [bf(p['w_out1'])], x2, ln(p['ln_g'], 1, 0), ln(p['ln_b'], 1, 0),
        bf(p['ffn_w_up'][1]), bf(p['ffn_w_gate'][1]), p['ffn_conv_w'][1],
        p['ffn_conv_b'][1][None], bf(p['ffn_w_down'][1]), ln(p['ln_g'], 1, 1),
        ln(p['ln_b'], 1, 1), tm)
    xs3 = _out_ln([o_s], [p['w_out1']], xs2, ln(p['ln_g'], 1, 0), ln(p['ln_b'], 1, 0), n,
                  "out1_sample")
    xs4, up1 = _ffn_sample(xs3, p['ffn_w_up'][1], p['ffn_w_gate'][1], ffn_conv[1, :, 0],
                           ffn_conv[1, :, 1], p['ffn_conv_w'][1], p['ffn_conv_b'][1][None],
                           p['ffn_w_down'][1], ln(p['ln_g'], 1, 1), ln(p['ln_b'], 1, 1))

    prompt = (x4[None], tail[None, 5:8], h_last[0:1], s_re[0:1].reshape(1, S5_GROUPS, S5_STATE),
              s_im[0:1].reshape(1, S5_GROUPS, S5_STATE), k.reshape(1, t, N_HEADS, HEAD_DIM),
              v.reshape(1, t, N_HEADS, HEAD_DIM), lf[None, :, :N_HEADS],
              jnp.stack([ffn_tail0[None, 6:8], ffn_tail1[None, 6:8]], axis=0))
    new_rg_conv = jnp.concatenate([rg_conv[:, 1:], xa_s[:, None]], axis=1)
    new_ffn = jnp.stack([jnp.stack([ffn_conv[0, :, 1], up0], axis=1),
                         jnp.stack([ffn_conv[1, :, 1], up1], axis=1)], axis=0)
    sample = (xs4[:, None], new_rg_conv, h_new, xr.reshape(n, S5_GROUPS, S5_STATE),
              xi.reshape(n, S5_GROUPS, S5_STATE), k_s.reshape(n, 1, N_HEADS, HEAD_DIM),
              v_s.reshape(n, 1, N_HEADS, HEAD_DIM), logf_s[:, None], new_ffn)
    return prompt, sample


def _derived_params(p):
    a_re, a_im, bb_re, bb_im = _s5_discretize(p['s5_lambda_re'], p['s5_lambda_im'], p['s5_log_dt'],
                                              p['s5_b_re'], p['s5_b_im'])
    return dict(
        wa_bd=_block_diag_dense(p['rg_wa']), wx_bd=_block_diag_dense(p['rg_wx']),
        c_lam=_row(-RG_C * jax.nn.softplus(-p['rg_lambda'])),
        a_re=a_re.reshape(1, N_STATE), a_im=a_im.reshape(1, N_STATE),
        **_s5_scan_tables(a_re, a_im),
        bbr=_s5_in_blocks(bb_re), bbi=_s5_in_blocks(bb_im),
        cre=_s5_out_blocks(p['s5_c_re']), cim=_s5_out_blocks(p['s5_c_im']))


PROMPT_TILES = dict(scan=256, rows=512, qkv=256, tq=1024, tk=1024, qc=256)
PAGES_PER_STEP = 4


def kernel(x_prompt, x_sample, state_rglru_conv, state_rglru_h, state_s5_re, state_s5_im, cache_k, cache_v, cache_logf, state_ffn_conv, page_table, w_in0, rg_conv_w, rg_conv_b, rg_wa, rg_ba, rg_wx, rg_bx, rg_lambda, s5_lambda_re, s5_lambda_im, s5_log_dt, s5_b_re, s5_b_im, s5_c_re, s5_c_im, s5_d, s5_w_glu, s5_b_glu, w_out0, w_in1, b_fgt, w_out1, ffn_w_up, ffn_w_gate, ffn_conv_w, ffn_conv_b, ffn_w_down, ln_g, ln_b):
    p = dict(w_in0=w_in0, rg_conv_w=rg_conv_w, rg_conv_b=rg_conv_b, rg_wa=rg_wa, rg_ba=rg_ba,
             rg_wx=rg_wx, rg_bx=rg_bx, rg_lambda=rg_lambda, s5_lambda_re=s5_lambda_re,
             s5_lambda_im=s5_lambda_im, s5_log_dt=s5_log_dt, s5_b_re=s5_b_re, s5_b_im=s5_b_im,
             s5_c_re=s5_c_re, s5_c_im=s5_c_im, s5_d=s5_d, s5_w_glu=s5_w_glu, s5_b_glu=s5_b_glu,
             w_out0=w_out0, w_in1=w_in1, b_fgt=b_fgt, w_out1=w_out1, ffn_w_up=ffn_w_up,
             ffn_w_gate=ffn_w_gate, ffn_conv_w=ffn_conv_w, ffn_conv_b=ffn_conv_b,
             ffn_w_down=ffn_w_down, ln_g=ln_g, ln_b=ln_b)
    s5p = _derived_params(p)
    prompt, sample = _forward(x_prompt[0], x_sample[:, 0], state_rglru_conv, state_rglru_h,
                              state_s5_re, state_s5_im, cache_k, cache_v, cache_logf,
                              state_ffn_conv, page_table, p, s5p, PROMPT_TILES, PAGES_PER_STEP)
    return (prompt[0], sample[0]) + tuple(prompt[1:]) + tuple(sample[1:])
```

```python
import functools

import jax
import jax.numpy as jnp
from jax import lax
from jax.experimental import pallas as pl
from jax.experimental.pallas import tpu as pltpu

F32 = jnp.float32
BF16 = jnp.bfloat16

D_MODEL = 1024
DEPTH = 2
PAGE_SIZE = 128
D_RNN = 512
RG_BLOCKS = 8
RG_BW = D_RNN // RG_BLOCKS
RG_C = 8.0
D_SSM = 512
S5_GROUP = 16
S5_GROUPS = D_SSM // S5_GROUP
S5_STATE = 64
N_STATE = S5_GROUPS * S5_STATE
N_HEADS = 16
HEAD_DIM = D_MODEL // N_HEADS
D_FF = 2816
ALPHA = (2.0 * DEPTH) ** 0.25
LN_EPS = 1e-5

LANES = 128
GROUPS_PER_BLOCK = LANES // S5_GROUP
N_SSM_BLOCKS = D_SSM // LANES
STATE_BLOCK = GROUPS_PER_BLOCK * S5_STATE
FF_CHUNK = 256
N_FF_CHUNKS = D_FF // FF_CHUNK
NEG_BIG = -1e30
LOG2E = 1.4426950408889634
AUG_COLS = 6
VMEM_LIMIT = 56 * 1024 * 1024


def _cparams(n_axes=1, vmem=VMEM_LIMIT):
    return pltpu.CompilerParams(dimension_semantics=("arbitrary",) * n_axes,
                                vmem_limit_bytes=vmem)


def _split3(x):
    h1 = x.astype(BF16)
    r1 = x - h1.astype(F32)
    h2 = r1.astype(BF16)
    h3 = (r1 - h2.astype(F32)).astype(BF16)
    return h1, h2, h3


def _dotw(x, w):
    if w.dtype == BF16:
        return jnp.dot(x.astype(BF16), w, preferred_element_type=F32)
    xh = x.astype(BF16)
    xl = (x - xh.astype(F32)).astype(BF16)
    wh = w.astype(BF16)
    wl = (w - wh.astype(F32)).astype(BF16)
    return (jnp.dot(xh, wh, preferred_element_type=F32)
            + jnp.dot(xl, wh, preferred_element_type=F32)
            + jnp.dot(xh, wl, preferred_element_type=F32))


def _dot_exact01(m01, x):
    h1, h2, h3 = _split3(x)
    return (jnp.dot(m01, h1, preferred_element_type=F32)
            + jnp.dot(m01, h2, preferred_element_type=F32)
            + jnp.dot(m01, h3, preferred_element_type=F32))


def _gelu(x):
    return jax.nn.gelu(x)


def _sigmoid(x):
    return jax.nn.sigmoid(x)


def _log_sigmoid(x):
    return jnp.minimum(x, 0.0) - jnp.log1p(jnp.exp(-jnp.abs(x)))


def _expm1(x):
    u = jnp.exp(x)
    um1 = u - 1.0
    return jnp.where(um1 == 0.0, x, jnp.where(um1 == -1.0, -1.0, um1 * x / jnp.log(u)))


def _layer_norm(xf, g, b):
    mean = jnp.mean(xf, axis=-1, keepdims=True)
    xc = xf - mean
    var = jnp.mean(xc * xc, axis=-1, keepdims=True)
    return xc * lax.rsqrt(var + LN_EPS) * g + b


def _rglru_gates(xc, wa, ba, wx, bx, c_lam):
    r = _sigmoid(_dotw(xc, wa) + ba)
    i = _sigmoid(_dotw(xc, wx) + bx)
    log_a = c_lam * r
    a = jnp.exp(log_a)
    inp = jnp.sqrt(-_expm1(2.0 * log_a)) * (i * xc)
    return a, inp


def _scan_rows(a, b):
    n = a.shape[0]
    row = lax.broadcasted_iota(jnp.int32, a.shape, 0)
    d = 1
    while d < n:
        valid = row >= d
        b = b + jnp.where(valid, a * pltpu.roll(b, d, 0), 0.0)
        if 2 * d < n:
            a = jnp.where(valid, a * pltpu.roll(a, d, 0), a)
        d *= 2
    return b


def _scan_rows_const(lvl_re, lvl_im, pow_re, pow_im, br, bi, pr, pi):
    n, c = br.shape
    g = n // 8
    for k in range(3):
        d = 1 << k
        sr = pltpu.roll(br, d, 0).reshape(g, 8, c)
        si = pltpu.roll(bi, d, 0).reshape(g, 8, c)
        cr = lvl_re[k][None]
        ci = lvl_im[k][None]
        br, bi = ((br.reshape(g, 8, c) + (cr * sr - ci * si)).reshape(n, c),
                  (bi.reshape(g, 8, c) + (cr * si + ci * sr)).reshape(n, c))
    b3r = br.reshape(g, 8, c)
    b3i = bi.reshape(g, 8, c)
    outs_r, outs_i = [], []
    for gi in range(g):
        xr = b3r[gi] + (pow_re * pr - pow_im * pi)
        xi = b3i[gi] + (pow_re * pi + pow_im * pr)
        pr, pi = xr[7:8, :], xi[7:8, :]
        outs_r.append(xr)
        outs_i.append(xi)
    return jnp.concatenate(outs_r, axis=0), jnp.concatenate(outs_i, axis=0)


def _rglru_prompt_kernel(x_ref, w_ref, cw_ref, cb_ref, wa_ref, ba_ref, wx_ref, bx_ref, cl_ref,
                         ya_ref, tail_ref, h_ref, ext_scr, h_scr, *, tb):
    step = pl.program_id(0)

    @pl.when(step == 0)
    def _():
        ext_scr[0:8, :] = jnp.zeros((8, D_RNN), F32)
        h_scr[...] = jnp.zeros_like(h_scr)

    proj = _dotw(x_ref[...], w_ref[...])
    xa = proj[:, :D_RNN]
    ga = proj[:, D_RNN:]
    ext_scr[8:8 + tb, :] = xa
    xc = cb_ref[...] + xa * cw_ref[3:4, :]
    for j in range(3):
        xc = xc + ext_scr[pl.ds(5 + j, tb), :] * cw_ref[j:j + 1, :]
    tail = ext_scr[tb:tb + 8, :]
    ext_scr[0:8, :] = tail
    tail_ref[...] = tail

    a, inp = _rglru_gates(xc, wa_ref[...], ba_ref[...], wx_ref[...], bx_ref[...], cl_ref[...])
    row = lax.broadcasted_iota(jnp.int32, a.shape, 0)
    inp = inp + jnp.where(row == 0, a * h_scr[0:1, :], 0.0)
    h = _scan_rows(a, inp)
    h_last = jnp.broadcast_to(h[tb - 1:tb, :], (8, D_RNN))
    h_scr[...] = h_last
    h_ref[...] = h_last
    ya_ref[...] = h * _gelu(ga)


def _rglru_prompt(x, w_ag, cw, cb, wa, ba, wx, bx, c_lam, tb):
    t = x.shape[0]
    full = lambda shape: pl.BlockSpec(shape, lambda i: (0,) * len(shape))
    return pl.pallas_call(
        functools.partial(_rglru_prompt_kernel, tb=tb),
        grid=(t // tb,),
        in_specs=[pl.BlockSpec((tb, D_MODEL), lambda i: (i, 0)),
                  full(w_ag.shape), full(cw.shape), full(cb.shape), full(wa.shape), full(ba.shape),
                  full(wx.shape), full(bx.shape), full(c_lam.shape)],
        out_specs=[pl.BlockSpec((tb, D_RNN), lambda i: (i, 0)),
                   pl.BlockSpec((8, D_RNN), lambda i: (0, 0)),
                   pl.BlockSpec((8, D_RNN), lambda i: (0, 0))],
        out_shape=[jax.ShapeDtypeStruct((t, D_RNN), F32),
                   jax.ShapeDtypeStruct((8, D_RNN), F32),
                   jax.ShapeDtypeStruct((8, D_RNN), F32)],
        scratch_shapes=[pltpu.VMEM((tb + 8, D_RNN), F32), pltpu.VMEM((8, D_RNN), F32)],
        compiler_params=_cparams(),
        name="rglru_prompt",
    )(x, w_ag, cw, cb, wa, ba, wx, bx, c_lam)


def _s5_readout(u, xr_blocks, xi_blocks, cre_ref, cim_ref, d, wglu, bglu):
    ys = []
    for j in range(N_SSM_BLOCKS):
        ys.append(_dotw(xr_blocks[j], cre_ref[j]) - _dotw(xi_blocks[j], cim_ref[j]))
    y = jnp.concatenate(ys, axis=1) + d * u
    g = _gelu(y)
    return g * _sigmoid(_dotw(g, wglu) + bglu)


def _s5_prompt_kernel(x_ref, w_ref, bbr_ref, bbi_ref, lvr_ref, lvi_ref, pwr_ref, pwi_ref,
                      cre_ref, cim_ref, d_ref, wglu_ref, bglu_ref, yb_ref, sre_ref, sim_ref,
                      sr_scr, si_scr, *, tb):
    step = pl.program_id(0)

    @pl.when(step == 0)
    def _():
        sr_scr[...] = jnp.zeros_like(sr_scr)
        si_scr[...] = jnp.zeros_like(si_scr)

    u = _dotw(x_ref[...], w_ref[...])
    xr_blocks, xi_blocks = [], []
    for j in range(N_SSM_BLOCKS):
        cols = slice(j * STATE_BLOCK, (j + 1) * STATE_BLOCK)
        uj = u[:, j * LANES:(j + 1) * LANES]
        br = _dotw(uj, bbr_ref[j])
        bi = _dotw(uj, bbi_ref[j])
        xr, xi = _scan_rows_const(lvr_ref[:, :, cols], lvi_ref[:, :, cols], pwr_ref[:, cols],
                                  pwi_ref[:, cols], br, bi, sr_scr[0:1, cols], si_scr[0:1, cols])
        sr_scr[:, cols] = jnp.broadcast_to(xr[tb - 1:tb, :], (8, STATE_BLOCK))
        si_scr[:, cols] = jnp.broadcast_to(xi[tb - 1:tb, :], (8, STATE_BLOCK))
        xr_blocks.append(xr)
        xi_blocks.append(xi)
    sre_ref[...] = sr_scr[...]
    sim_ref[...] = si_scr[...]
    yb_ref[...] = _s5_readout(u, xr_blocks, xi_blocks, cre_ref, cim_ref, d_ref[...],
                              wglu_ref[...], bglu_ref[...])


def _s5_prompt(x, w_b, bbr, bbi, lvl_re, lvl_im, pow_re, pow_im, cre, cim, d, wglu, bglu, tb):
    t = x.shape[0]
    full = lambda shape: pl.BlockSpec(shape, lambda i: (0,) * len(shape))
    return pl.pallas_call(
        functools.partial(_s5_prompt_kernel, tb=tb),
        grid=(t // tb,),
        in_specs=[pl.BlockSpec((tb, D_MODEL), lambda i: (i, 0)),
                  full(w_b.shape), full(bbr.shape), full(bbi.shape), full(lvl_re.shape),
                  full(lvl_im.shape), full(pow_re.shape), full(pow_im.shape), full(cre.shape),
                  full(cim.shape), full(d.shape), full(wglu.shape), full(bglu.shape)],
        out_specs=[pl.BlockSpec((tb, D_SSM), lambda i: (i, 0)),
                   pl.BlockSpec((8, N_STATE), lambda i: (0, 0)),
                   pl.BlockSpec((8, N_STATE), lambda i: (0, 0))],
        out_shape=[jax.ShapeDtypeStruct((t, D_SSM), F32),
                   jax.ShapeDtypeStruct((8, N_STATE), F32),
                   jax.ShapeDtypeStruct((8, N_STATE), F32)],
        scratch_shapes=[pltpu.VMEM((8, N_STATE), F32), pltpu.VMEM((8, N_STATE), F32)],
        compiler_params=_cparams(),
        name="s5_prompt",
    )(x, w_b, bbr, bbi, lvl_re, lvl_im, pow_re, pow_im, cre, cim, d, wglu, bglu)


def _out_ln_kernel(*refs, n_in):
    a_refs = refs[:n_in]
    w_refs = refs[n_in:2 * n_in]
    x_ref, g_ref, b_ref, o_ref = refs[2 * n_in:]
    y = ALPHA * x_ref[...]
    for a_ref, w_ref in zip(a_refs, w_refs):
        y = y + _dotw(a_ref[...], w_ref[...])
    o_ref[...] = _layer_norm(y, g_ref[...], b_ref[...])


def _out_ln(acts, weights, x, g, b, tm, name):
    m = x.shape[0]
    n_in = len(acts)
    full = lambda shape: pl.BlockSpec(shape, lambda i: (0,) * len(shape))
    in_specs = ([pl.BlockSpec((tm, a.shape[1]), lambda i: (i, 0)) for a in acts]
                + [full(w.shape) for w in weights]
                + [pl.BlockSpec((tm, D_MODEL), lambda i: (i, 0)), full(g.shape), full(b.shape)])
    return pl.pallas_call(
        functools.partial(_out_ln_kernel, n_in=n_in),
        grid=(m // tm,),
        in_specs=in_specs,
        out_specs=pl.BlockSpec((tm, D_MODEL), lambda i: (i, 0)),
        out_shape=jax.ShapeDtypeStruct((m, D_MODEL), F32),
        compiler_params=_cparams(),
        name=name,
    )(*acts, *weights, x, g, b)


def _ffn_prompt_kernel(*refs, n_in, tm):
    a_refs = refs[:n_in]
    w_refs = refs[n_in:2 * n_in]
    (x_ref, g0_ref, b0_ref, wup_ref, wgate_ref, cw_ref, cb_ref, wdown_ref, g_ref, b_ref,
     o_ref, tail_ref, h_scr, carry_scr) = refs[2 * n_in:]
    step = pl.program_id(0)

    @pl.when(step == 0)
    def _():
        carry_scr[...] = jnp.zeros_like(carry_scr)

    y = ALPHA * x_ref[...]
    for a_ref, w_ref in zip(a_refs, w_refs):
        y = y + _dotw(a_ref[...], w_ref[...])
    x = _layer_norm(y, g0_ref[...], b0_ref[...])
    xb = x.astype(BF16)
    row = lax.broadcasted_iota(jnp.int32, (tm, FF_CHUNK), 0)
    for c in range(N_FF_CHUNKS):
        cols = slice(c * FF_CHUNK, (c + 1) * FF_CHUNK)
        up = jnp.dot(xb, wup_ref[:, cols], preferred_element_type=F32)
        gate = jnp.dot(xb, wgate_ref[:, cols], preferred_element_type=F32)
        prev1 = carry_scr[7:8, cols]
        prev2 = carry_scr[6:7, cols]
        up1 = jnp.where(row >= 1, pltpu.roll(up, 1, 0), prev1)
        up2 = jnp.where(row >= 2, pltpu.roll(up, 2, 0), jnp.where(row == 1, prev1, prev2))
        hc = (cb_ref[:, cols] + up2 * cw_ref[0:1, cols] + up1 * cw_ref[1:2, cols]
              + up * cw_ref[2:3, cols])
        h_scr[:, cols] = (_gelu(hc) * gate).astype(BF16)
        carry_scr[:, cols] = up[tm - 8:tm, :]
    tail_ref[...] = carry_scr[...]
    f = jnp.dot(h_scr[...], wdown_ref[...], preferred_element_type=F32)
    o_ref[...] = _layer_norm(ALPHA * x + f, g_ref[...], b_ref[...])


def _ffn_prompt(acts, weights, x, g0, b0, wup, wgate, cw, cb, wdown, g, b, tm):
    t = x.shape[0]
    n_in = len(acts)
    const = lambda a: pl.BlockSpec(a.shape, lambda i: (0,) * a.ndim,
                                   pipeline_mode=pl.Buffered(1))
    rows = lambda a: pl.BlockSpec((tm, a.shape[1]), lambda i: (i, 0))
    consts = [g0, b0, wup, wgate, cw, cb, wdown, g, b]
    return pl.pallas_call(
        functools.partial(_ffn_prompt_kernel, n_in=n_in, tm=tm),
        grid=(t // tm,),
        in_specs=([rows(a) for a in acts] + [const(w) for w in weights] + [rows(x)]
                  + [const(c) for c in consts]),
        out_specs=[pl.BlockSpec((tm, D_MODEL), lambda i: (i, 0)),
                   pl.BlockSpec((8, D_FF), lambda i: (0, 0))],
        out_shape=[jax.ShapeDtypeStruct((t, D_MODEL), F32),
                   jax.ShapeDtypeStruct((8, D_FF), F32)],
        scratch_shapes=[pltpu.VMEM((tm, D_FF), BF16), pltpu.VMEM((8, D_FF), F32)],
        compiler_params=_cparams(),
        name="ffn_prompt",
    )(*acts, *weights, x, *consts)


def _aug_lhs(f3):
    h1, h2, h3 = (h.astype(F32) for h in _split3(f3))
    lane = lax.broadcasted_iota(jnp.int32, f3.shape, 1)
    pieces = jnp.where(lane < N_HEADS, h1, jnp.where(lane < 2 * N_HEADS, h2, jnp.where(
        lane < 3 * N_HEADS, h3, jnp.where(lane < 4 * N_HEADS, 1.0, 0.0))))
    return pieces.astype(BF16)


def _qkv_prompt_kernel(x_ref, w_ref, wf_ref, bf_ref, selq_ref, selk_ref,
                       kt_ref, vt_ref, lft_ref, qa_ref, ka_ref, va_ref, f_scr, *, tm):
    step = pl.program_id(0)

    @pl.when(step == 0)
    def _():
        f_scr[...] = jnp.zeros_like(f_scr)

    x = x_ref[...]
    qkv = _dotw(x, w_ref[...])
    k = qkv[:, D_MODEL:2 * D_MODEL]
    v = qkv[:, 2 * D_MODEL:]
    logf3 = _log_sigmoid(_dotw(x, wf_ref[...]) + bf_ref[...])
    lft_ref[...] = logf3.T

    r_i = lax.broadcasted_iota(jnp.int32, (tm, tm), 0)
    c_i = lax.broadcasted_iota(jnp.int32, (tm, tm), 1)
    tri = jnp.where(r_i >= c_i, 1.0, 0.0).astype(BF16)
    f3 = _dot_exact01(tri, logf3) + f_scr[0:1, :]
    f_scr[...] = jnp.broadcast_to(f3[tm - 1:tm, :], f_scr.shape)

    lhs = _aug_lhs(f3 * LOG2E)
    augq = jnp.dot(lhs, selq_ref[...], preferred_element_type=F32)
    augk = jnp.dot(lhs, selk_ref[...], preferred_element_type=F32)
    lane = lax.broadcasted_iota(jnp.int32, (tm, LANES), 1)
    low = lane < HEAD_DIM
    vtail = jnp.where(lane == HEAD_DIM, 1.0, 0.0)
    scale = HEAD_DIM ** -0.5 * LOG2E
    for hp in range(N_HEADS // 2):
        cols = slice(hp * LANES, (hp + 1) * LANES)
        qb = qkv[:, cols] * scale
        kb = k[:, cols]
        vb = v[:, cols]
        kt_ref[cols, :] = kb.T
        vt_ref[cols, :] = vb.T
        for odd in range(2):
            h = 2 * hp + odd
            acols = slice(h * LANES, (h + 1) * LANES)
            if odd:
                qb, kb, vb = (pltpu.roll(qb, HEAD_DIM, 1), pltpu.roll(kb, HEAD_DIM, 1),
                              pltpu.roll(vb, HEAD_DIM, 1))
            qa_ref[h] = jnp.where(low, qb, augq[:, acols]).astype(BF16)
            ka_ref[h] = jnp.where(low, kb, augk[:, acols]).astype(BF16)
            va_ref[h] = jnp.where(low, vb, vtail).astype(BF16)


def _qkv_prompt(x, w_qkv, wf3, bf3, selq, selk, tm):
    t = x.shape[0]
    full = lambda shape: pl.BlockSpec(shape, lambda i: (0,) * len(shape))
    head_major = jax.ShapeDtypeStruct((N_HEADS, t, LANES), BF16)
    head_spec = pl.BlockSpec((N_HEADS, tm, LANES), lambda i: (0, i, 0))
    return pl.pallas_call(
        functools.partial(_qkv_prompt_kernel, tm=tm),
        grid=(t // tm,),
        in_specs=[pl.BlockSpec((tm, D_MODEL), lambda i: (i, 0)),
                  full(w_qkv.shape), full(wf3.shape), full(bf3.shape), full(selq.shape),
                  full(selk.shape)],
        out_specs=[pl.BlockSpec((D_MODEL, tm), lambda i: (0, i)),
                   pl.BlockSpec((D_MODEL, tm), lambda i: (0, i)),
                   pl.BlockSpec((LANES, tm), lambda i: (0, i)),
                   head_spec, head_spec, head_spec],
        out_shape=[jax.ShapeDtypeStruct((D_MODEL, t), F32),
                   jax.ShapeDtypeStruct((D_MODEL, t), F32),
                   jax.ShapeDtypeStruct((LANES, t), F32),
                   head_major, head_major, head_major],
        scratch_shapes=[pltpu.VMEM((8, LANES), F32)],
        compiler_params=_cparams(),
        name="qkv_prompt",
    )(x, w_qkv, wf3, bf3, selq, selk)


def _repeat_rows(x):
    return jnp.broadcast_to(x[:, None, :], (N_HEADS, HEAD_DIM, x.shape[1])).reshape(
        D_MODEL, x.shape[1])


def _head_sum(x):
    return jnp.sum(x.reshape(N_HEADS, HEAD_DIM, x.shape[1]), axis=1)


def _flash_kernel(it_ref, jt_ref, pt_ref, q_ref, k_ref, v_ref, qb_ref, kn_ref, vn_ref, gt_ref,
                  *refs, tq, tk, qc, pp, n_tri, steps_per_seq, n_seq):
    k_pages = refs[:pp]
    v_pages = refs[pp:2 * pp]
    lf_pages = refs[2 * pp:3 * pp]
    o_ref, os_ref, m_scr, acc_scr, sm_scr, sl_scr, sacc_scr, scarry_scr = refs[3 * pp:]
    s_idx = pl.program_id(1)
    i = it_ref[s_idx]
    j = jt_ref[s_idx]
    t = pl.program_id(0) * n_tri + s_idx
    grp = t % steps_per_seq
    seq_active = t < n_seq * steps_per_seq
    first_group = grp == 0

    @pl.when(t == 0)
    def _():
        sm_scr[...] = jnp.zeros_like(sm_scr)
        sl_scr[...] = jnp.zeros_like(sl_scr)
        sacc_scr[...] = jnp.zeros_like(sacc_scr)
        scarry_scr[...] = jnp.zeros_like(scarry_scr)

    @pl.when(j == 0)
    def _():
        m_scr[...] = jnp.full(m_scr.shape, NEG_BIG, F32)
        acc_scr[...] = jnp.zeros_like(acc_scr)

    r_i = lax.broadcasted_iota(jnp.int32, (PAGE_SIZE, PAGE_SIZE), 0)
    c_i = lax.broadcasted_iota(jnp.int32, (PAGE_SIZE, PAGE_SIZE), 1)
    later = jnp.where(r_i > c_i, 1.0, 0.0).astype(BF16)

    def sample_pieces():
        st = {}

        def qk(r):
            def run():
                kt = k_pages[r][0].reshape(D_MODEL, PAGE_SIZE)
                st['s%d' % r] = _head_sum(kt * qb_ref[0])
            return run

        def softmax():
            gt = gt_ref[0]
            carry = jnp.where(first_group, 0.0, scarry_scr[:, 0:1])
            logits = []
            for r in range(pp):
                lf = lf_pages[r][0]
                suf = sum(jnp.dot(pc, later, preferred_element_type=F32) for pc in _split3(lf))
                logits.append(st['s%d' % r] + suf + (carry + gt))
                carry = carry + jnp.sum(lf, axis=1, keepdims=True)
            scarry_scr[...] = jnp.broadcast_to(carry, scarry_scr.shape)
            mx = logits[0]
            for lg in logits[1:]:
                mx = jnp.maximum(mx, lg)
            m_prev = jnp.where(first_group, NEG_BIG, sm_scr[:, 0:1])
            l_prev = jnp.where(first_group, 0.0, sl_scr[:, 0:1])
            m_next = jnp.maximum(m_prev, jnp.max(mx, axis=1, keepdims=True))
            alpha = jnp.exp(m_prev - m_next)
            ps = [jnp.exp(lg - m_next) for lg in logits]
            psum = ps[0]
            for pblk in ps[1:]:
                psum = psum + pblk
            sl_scr[...] = jnp.broadcast_to(
                alpha * l_prev + jnp.sum(psum, axis=1, keepdims=True), sl_scr.shape)
            sm_scr[...] = jnp.broadcast_to(m_next, sm_scr.shape)
            st['ps'] = ps
            st['alpha'] = jnp.broadcast_to(alpha, (N_HEADS, PAGE_SIZE))

        def pv(r):
            def run():
                vt = v_pages[r][0].reshape(D_MODEL, PAGE_SIZE)
                if r == 0:
                    st['acc'] = _repeat_rows(st['alpha']) * sacc_scr[...]
                st['acc'] = st['acc'] + _repeat_rows(st['ps'][r]) * vt
                if r == pp - 1:
                    sacc_scr[...] = st['acc']
            return run

        return [qk(r) for r in range(pp)] + [softmax] + [pv(r) for r in range(pp)]

    def accumulate(masked):
        chunks = [(hh, c) for hh in range(2) for c in range(tq // qc)]
        pieces = sample_pieces()

        def n_keys(c):
            return (c + 1) * qc if masked else tk

        def scores(hh, c):
            rows = slice(c * qc, (c + 1) * qc)
            nk = n_keys(c)
            s = lax.dot_general(q_ref[hh, rows, :], k_ref[hh, 0:nk, :], (((1,), (1,)), ((), ())),
                                preferred_element_type=F32)
            if masked:
                qpos = c * qc + lax.broadcasted_iota(jnp.int32, (qc, nk), 0)
                kpos = lax.broadcasted_iota(jnp.int32, (qc, nk), 1)
                s = jnp.where(kpos <= qpos, s, NEG_BIG)
            return s

        ahead = 2
        pending = [scores(*chunks[n]) for n in range(ahead)]
        per_chunk = -(-len(pieces) // len(chunks))
        for n, (hh, c) in enumerate(chunks):
            s = pending.pop(0)
            if n + ahead < len(chunks):
                pending.append(scores(*chunks[n + ahead]))
            for piece in pieces[n * per_chunk:(n + 1) * per_chunk]:
                piece()
            rows = slice(c * qc, (c + 1) * qc)
            m_prev = m_scr[hh, rows, :]
            m_next = jnp.maximum(m_prev, jnp.max(s, axis=1, keepdims=True))
            alpha = jnp.exp2(m_prev - m_next)
            p = jnp.exp2(s - m_next[:, 0:1]).astype(BF16)
            acc_scr[hh, rows, :] = alpha * acc_scr[hh, rows, :] + jnp.dot(
                p, v_ref[hh, 0:n_keys(c), :], preferred_element_type=F32)
            m_scr[hh, rows, :] = m_next

    assert tq == tk, "the diagonal-block key trimming assumes square blocks"
    crosses_diagonal = j == i

    @pl.when(crosses_diagonal)
    def _():
        accumulate(True)

    @pl.when(jnp.logical_not(crosses_diagonal))
    def _():
        accumulate(False)

    @pl.when(j == ((i + 1) * tq - 1) // tk)
    def _():
        lane = lax.broadcasted_iota(jnp.int32, (tq, LANES), 1)
        outs = []
        for hh in range(2):
            acc = acc_scr[hh]
            outs.append(acc / acc[:, HEAD_DIM:HEAD_DIM + 1])
        o_ref[...] = jnp.where(lane < HEAD_DIM, outs[0], pltpu.roll(outs[1], HEAD_DIM, 1))

    @pl.when(jnp.logical_and(seq_active, grp == steps_per_seq - 1))
    def _():
        qcol = qb_ref[0][:, 0:1]
        s_self = _head_sum(qcol * kn_ref[0])
        m_prev = sm_scr[:, 0:1]
        m_fin = jnp.maximum(m_prev, s_self)
        a_past = jnp.exp(m_prev - m_fin)
        a_self = jnp.exp(s_self - m_fin)
        l_fin = a_past * sl_scr[:, 0:1] + a_self
        past = jnp.sum(sacc_scr[...], axis=1, keepdims=True)
        os_ref[0] = (_repeat_rows(a_past) * past + _repeat_rows(a_self) * vn_ref[0]) / _repeat_rows(
            l_fin)


def _flash(qa, ka, va, tq, tk, qc, page_table, q_s, k_s, v_s, logf_s, cache_k, cache_v,
           cache_logf, pp):
    t = ka.shape[1]
    nq = t // tq
    i_list, j_list = [], []
    for i in range(nq):
        for j in range(((i + 1) * tq - 1) // tk + 1):
            i_list.append(i)
            j_list.append(j)
    n_tri = len(i_list)
    it = jnp.asarray(i_list, jnp.int32)
    jt = jnp.asarray(j_list, jnp.int32)

    n_seq, n_pages = page_table.shape
    steps_per_seq = n_pages // pp
    n_steps = (N_HEADS // 2) * n_tri
    assert n_pages % pp == 0 and n_seq * steps_per_seq <= n_steps
    pt = page_table.reshape(-1)
    ck = jnp.transpose(cache_k, (0, 2, 3, 1))
    cv = jnp.transpose(cache_v, (0, 2, 3, 1))
    clf = jnp.transpose(cache_logf, (0, 2, 1))
    scale = HEAD_DIM ** -0.5
    qb = jnp.broadcast_to((q_s * scale)[:, :, None], (n_seq, D_MODEL, PAGE_SIZE))
    col = lambda a: a.reshape(n_seq, D_MODEL, 1)

    def seq_of(hp, s):
        return jnp.minimum((hp * n_tri + s) // steps_per_seq, n_seq - 1)

    def page_map(r, ndim):
        def index(hp, s, it, jt, pt):
            step = hp * n_tri + s
            grp = jnp.where(step < n_seq * steps_per_seq, step % steps_per_seq,
                            steps_per_seq - 1)
            page = pt[seq_of(hp, s) * n_pages + (n_pages - 1 - (grp * pp + r))]
            return (page,) + (0,) * ndim
        return index

    per_seq = lambda shape: pl.BlockSpec((1,) + shape,
                                         lambda hp, s, it, jt, pt: (seq_of(hp, s), 0, 0))
    kv_block = (1, N_HEADS, HEAD_DIM, PAGE_SIZE)
    in_specs = [pl.BlockSpec((2, tq, LANES), lambda hp, s, it, jt, pt: (hp, it[s], 0)),
                pl.BlockSpec((2, tk, LANES), lambda hp, s, it, jt, pt: (hp, jt[s], 0)),
                pl.BlockSpec((2, tk, LANES), lambda hp, s, it, jt, pt: (hp, jt[s], 0)),
                per_seq((D_MODEL, PAGE_SIZE)), per_seq((D_MODEL, 1)), per_seq((D_MODEL, 1)),
                per_seq((N_HEADS, 1))]
    in_specs += [pl.BlockSpec(kv_block, page_map(r, 3)) for r in range(pp)]
    in_specs += [pl.BlockSpec(kv_block, page_map(r, 3)) for r in range(pp)]
    in_specs += [pl.BlockSpec((1, N_HEADS, PAGE_SIZE), page_map(r, 2)) for r in range(pp)]
    grid_spec = pltpu.PrefetchScalarGridSpec(
        num_scalar_prefetch=3,
        grid=(N_HEADS // 2, n_tri),
        in_specs=in_specs,
        out_specs=[pl.BlockSpec((tq, LANES), lambda hp, s, it, jt, pt: (it[s], hp)),
                   per_seq((D_MODEL, 1))],
        scratch_shapes=[pltpu.VMEM((2, tq, LANES), F32), pltpu.VMEM((2, tq, LANES), F32),
                        pltpu.VMEM((N_HEADS, LANES), F32), pltpu.VMEM((N_HEADS, LANES), F32),
                        pltpu.VMEM((D_MODEL, PAGE_SIZE), F32), pltpu.VMEM((N_HEADS, LANES), F32)],
    )
    o, o_s = pl.pallas_call(
        functools.partial(_flash_kernel, tq=tq, tk=tk, qc=qc, pp=pp, n_tri=n_tri,
                          steps_per_seq=steps_per_seq, n_seq=n_seq),
        grid_spec=grid_spec,
        out_shape=[jax.ShapeDtypeStruct((t, D_MODEL), F32),
                   jax.ShapeDtypeStruct((n_seq, D_MODEL, 1), F32)],
        compiler_params=_cparams(2),
        name="flash_prompt",
    )(it, jt, pt, qa, ka, va, qb, col(k_s), col(v_s), logf_s.reshape(n_seq, N_HEADS, 1),
      *([ck] * pp), *([cv] * pp), *([clf] * pp))
    return o, o_s.reshape(n_seq, D_MODEL)


def _mix0_sample_kernel(x_ref, w_ref, c0_ref, c1_ref, c2_ref, cw_ref, cb_ref, wa_ref, ba_ref,
                        wx_ref, bx_ref, cl_ref, h0_ref, sr_ref, si_ref, bbr_ref, bbi_ref,
                        ar_ref, ai_ref, cre_ref, cim_ref, d_ref, wglu_ref, bglu_ref,
                        ya_ref, yb_ref, xa_ref, h_ref, xr_ref, xi_ref):
    proj = _dotw(x_ref[...], w_ref[...])
    xa = proj[:, :D_RNN]
    ga = proj[:, D_RNN:2 * D_RNN]
    u = proj[:, 2 * D_RNN:]
    xa_ref[...] = xa
    xc = (cb_ref[...] + c0_ref[...] * cw_ref[0:1, :] + c1_ref[...] * cw_ref[1:2, :]
          + c2_ref[...] * cw_ref[2:3, :] + xa * cw_ref[3:4, :])
    a, inp = _rglru_gates(xc, wa_ref[...], ba_ref[...], wx_ref[...], bx_ref[...], cl_ref[...])
    h = a * h0_ref[...] + inp
    h_ref[...] = h
    ya_ref[...] = h * _gelu(ga)

    xr_blocks, xi_blocks = [], []
    for j in range(N_SSM_BLOCKS):
        cols = slice(j * STATE_BLOCK, (j + 1) * STATE_BLOCK)
        uj = u[:, j * LANES:(j + 1) * LANES]
        ar = ar_ref[:, cols]
        ai = ai_ref[:, cols]
        pr = sr_ref[:, cols]
        pi = si_ref[:, cols]
        xr = _dotw(uj, bbr_ref[j]) + (ar * pr - ai * pi)
        xi = _dotw(uj, bbi_ref[j]) + (ar * pi + ai * pr)
        xr_ref[:, cols] = xr
        xi_ref[:, cols] = xi
        xr_blocks.append(xr)
        xi_blocks.append(xi)
    yb_ref[...] = _s5_readout(u, xr_blocks, xi_blocks, cre_ref, cim_ref, d_ref[...],
                              wglu_ref[...], bglu_ref[...])


def _mix0_sample(*args):
    n = args[0].shape[0]
    shapes = [(n, D_RNN), (n, D_SSM), (n, D_RNN), (n, D_RNN), (n, N_STATE), (n, N_STATE)]
    return pl.pallas_call(
        _mix0_sample_kernel,
        out_shape=[jax.ShapeDtypeStruct(s, F32) for s in shapes],
        compiler_params=pltpu.CompilerParams(vmem_limit_bytes=VMEM_LIMIT),
        name="mix0_sample",
    )(*args)


def _ffn_sample_kernel(x_ref, wup_ref, wgate_ref, b0_ref, b1_ref, cw_ref, cb_ref, wdown_ref,
                       g_ref, b_ref, o_ref, up_ref, acc_scr):
    c = pl.program_id(0)

    @pl.when(c == 0)
    def _():
        acc_scr[...] = jnp.zeros_like(acc_scr)

    x = x_ref[...]
    up = _dotw(x, wup_ref[...])
    gate = _dotw(x, wgate_ref[...])
    up_ref[...] = up
    hc = (cb_ref[...] + b0_ref[...] * cw_ref[0:1, :] + b1_ref[...] * cw_ref[1:2, :]
          + up * cw_ref[2:3, :])
    acc_scr[...] += _dotw(_gelu(hc) * gate, wdown_ref[...])

    @pl.when(c == pl.num_programs(0) - 1)
    def _():
        o_ref[...] = _layer_norm(ALPHA * x + acc_scr[...], g_ref[...], b_ref[...])


def _ffn_sample(x, wup, wgate, buf0, buf1, cw, cb, wdown, g, b):
    n = x.shape[0]
    fc = FF_CHUNK
    col = lambda rows: pl.BlockSpec((rows, fc), lambda c: (0, c))
    full = lambda shape: pl.BlockSpec(shape, lambda c: (0,) * len(shape))
    return pl.pallas_call(
        _ffn_sample_kernel,
        grid=(D_FF // fc,),
        in_specs=[full(x.shape), col(D_MODEL), col(D_MODEL), col(n), col(n), col(cw.shape[0]),
                  col(1), pl.BlockSpec((fc, D_MODEL), lambda c: (c, 0)), full(g.shape),
                  full(b.shape)],
        out_specs=[full((n, D_MODEL)), col(n)],
        out_shape=[jax.ShapeDtypeStruct((n, D_MODEL), F32), jax.ShapeDtypeStruct((n, D_FF), F32)],
        scratch_shapes=[pltpu.VMEM((n, D_MODEL), F32)],
        compiler_params=_cparams(),
        name="ffn_sample",
    )(x, wup, wgate, buf0, buf1, cw, cb, wdown, g, b)


def _qkv_sample_kernel(x_ref, w_ref, wf_ref, bf_ref, o_ref, lf_ref):
    x = x_ref[...]
    o_ref[...] = _dotw(x, w_ref[...])

    @pl.when(pl.program_id(0) == 0)
    def _():
        lf_ref[...] = _log_sigmoid(_dotw(x, wf_ref[...]) + bf_ref[...])


def _qkv_sample(x, w_in1, wf, bf):
    n = x.shape[0]
    full = lambda shape: pl.BlockSpec(shape, lambda c: (0,) * len(shape))
    return pl.pallas_call(
        _qkv_sample_kernel,
        grid=(3,),
        in_specs=[full(x.shape), pl.BlockSpec((D_MODEL, D_MODEL), lambda c: (0, c)),
                  full(wf.shape), full(bf.shape)],
        out_specs=[pl.BlockSpec((n, D_MODEL), lambda c: (0, c)), full((n, LANES))],
        out_shape=[jax.ShapeDtypeStruct((n, 3 * D_MODEL), F32),
                   jax.ShapeDtypeStruct((n, LANES), F32)],
        compiler_params=_cparams(),
        name="qkv_sample",
    )(x, w_in1, wf, bf)


def _block_diag_dense(w):
    nb, n, _ = w.shape
    eye = jnp.eye(nb, dtype=w.dtype)
    return jnp.einsum('hij,hk->hikj', w, eye).reshape(nb * n, nb * n)


def _s5_discretize(lam_re, lam_im, log_dt, b_re, b_im):
    lr = jnp.minimum(lam_re, -1e-4)
    li = lam_im
    dt = jnp.exp(log_dt)[:, None]
    mag = jnp.exp(lr * dt)
    ab_re = mag * jnp.cos(li * dt)
    ab_im = mag * jnp.sin(li * dt)
    den = lr * lr + li * li
    nr = ab_re - 1.0
    zr = (nr * lr + ab_im * li) / den
    zi = (ab_im * lr - nr * li) / den
    bb_re = zr[..., None] * b_re - zi[..., None] * b_im
    bb_im = zr[..., None] * b_im + zi[..., None] * b_re
    return ab_re, ab_im, bb_re, bb_im


def _s5_scan_tables(a_re, a_im):
    ar = a_re.reshape(1, N_STATE)
    ai = a_im.reshape(1, N_STATE)
    pows = [(ar, ai)]
    for _ in range(7):
        qr, qi = pows[-1]
        pows.append((qr * ar - qi * ai, qr * ai + qi * ar))
    pow_re = jnp.concatenate([q[0] for q in pows], axis=0)
    pow_im = jnp.concatenate([q[1] for q in pows], axis=0)
    sub = jnp.arange(8)[:, None]
    lvl_re = jnp.stack([jnp.where(sub >= d, pows[d - 1][0], 0.0) for d in (1, 2, 4)])
    lvl_im = jnp.stack([jnp.where(sub >= d, pows[d - 1][1], 0.0) for d in (1, 2, 4)])
    return dict(lvl_re=lvl_re, lvl_im=lvl_im, pow_re=pow_re, pow_im=pow_im)


def _s5_in_blocks(bb):
    eye = jnp.eye(GROUPS_PER_BLOCK, dtype=bb.dtype)
    b4 = bb.transpose(0, 2, 1).reshape(N_SSM_BLOCKS, GROUPS_PER_BLOCK, S5_GROUP, S5_STATE)
    return jnp.einsum('jgmp,gh->jgmhp', b4, eye).reshape(N_SSM_BLOCKS, LANES, STATE_BLOCK)


def _s5_out_blocks(c):
    eye = jnp.eye(GROUPS_PER_BLOCK, dtype=c.dtype)
    c4 = c.reshape(N_SSM_BLOCKS, GROUPS_PER_BLOCK, S5_GROUP, S5_STATE)
    return jnp.einsum('jgmp,gh->jgphm', c4, eye).reshape(N_SSM_BLOCKS, STATE_BLOCK, LANES)


def _aug_selectors():
    rows = jnp.arange(LANES)[:, None]
    cols = jnp.arange(N_HEADS * LANES)[None, :]
    head = cols // LANES
    c = cols % LANES - HEAD_DIM
    piece = rows // N_HEADS
    rhead = rows % N_HEADS
    same = (rhead == head) & (rows < 4 * N_HEADS)
    f_piece = same & (piece < 3)
    ones_piece = same & (piece == 3)
    selq = jnp.where(f_piece & (c == piece), 1.0, 0.0) + jnp.where(
        ones_piece & (c >= 3) & (c < AUG_COLS), 1.0, 0.0)
    selk = jnp.where(ones_piece & (c >= 0) & (c < 3), 1.0, 0.0) - jnp.where(
        f_piece & (c == piece + 3), 1.0, 0.0)
    return selq.astype(BF16), selk.astype(BF16)


def _forget_cols(w_in1, b_fgt, dtype):
    wf = w_in1[:, 3 * D_MODEL:]
    pad = jnp.zeros((D_MODEL, LANES - 3 * N_HEADS), w_in1.dtype)
    wf3 = jnp.concatenate([wf, wf, wf, pad], axis=1).astype(dtype)
    bf3 = jnp.concatenate([b_fgt, b_fgt, b_fgt, jnp.zeros((LANES - 3 * N_HEADS,), F32)])[None, :]
    return wf3, bf3


def _row(v):
    return v.reshape(1, -1)


def _pick(t, pref):
    return pref if t % pref == 0 else t


def _forward(x, xs, rg_conv, rg_h, s5_re, s5_im, cache_k, cache_v, cache_logf, ffn_conv,
             page_table, p, s5p, tiles, pages_per_step):
    t = x.shape[0]
    n = xs.shape[0]
    bf = lambda w: w.astype(BF16)
    ln = lambda a, layer, which: a[layer, which][None]

    tb = _pick(t, tiles['scan'])
    ya, tail, h_last = _rglru_prompt(
        x, bf(p['w_in0'][:, :2 * D_RNN]), p['rg_conv_w'], _row(p['rg_conv_b']),
        bf(s5p['wa_bd']), _row(p['rg_ba']), bf(s5p['wx_bd']), _row(p['rg_bx']), s5p['c_lam'], tb)
    yb, s_re, s_im = _s5_prompt(
        x, bf(p['w_in0'][:, 2 * D_RNN:]), bf(s5p['bbr']), bf(s5p['bbi']), s5p['lvl_re'],
        s5p['lvl_im'], s5p['pow_re'], s5p['pow_im'],
        bf(s5p['cre']), bf(s5p['cim']), _row(p['s5_d']), bf(p['s5_w_glu']), _row(p['s5_b_glu']), tb)
    tm = _pick(t, tiles['rows'])
    x2, ffn_tail0 = _ffn_prompt(
        [ya, yb], [bf(p['w_out0'][:D_RNN]), bf(p['w_out0'][D_RNN:])], x,
        ln(p['ln_g'], 0, 0), ln(p['ln_b'], 0, 0), bf(p['ffn_w_up'][0]), bf(p['ffn_w_gate'][0]),
        p['ffn_conv_w'][0], p['ffn_conv_b'][0][None], bf(p['ffn_w_down'][0]),
        ln(p['ln_g'], 0, 1), ln(p['ln_b'], 0, 1), tm)
    wf3, bf3 = _forget_cols(p['w_in1'], p['b_fgt'], BF16)
    selq, selk = _aug_selectors()
    kt, vt, lft, qa, ka, va = _qkv_prompt(x2, bf(p['w_in1'][:, :3 * D_MODEL]), wf3, bf3, selq, selk,
                                       _pick(t, tiles['qkv']))

    per_token = lambda a: a.reshape(N_HEADS, HEAD_DIM, t).transpose(2, 0, 1)[None]

    ya_s, yb_s, xa_s, h_new, xr, xi = _mix0_sample(
        xs, p['w_in0'], rg_conv[:, 0], rg_conv[:, 1], rg_conv[:, 2], p['rg_conv_w'],
        _row(p['rg_conv_b']), s5p['wa_bd'], _row(p['rg_ba']), s5p['wx_bd'], _row(p['rg_bx']),
        s5p['c_lam'], rg_h, s5_re.reshape(n, N_STATE), s5_im.reshape(n, N_STATE),
        s5p['bbr'], s5p['bbi'], s5p['a_re'], s5p['a_im'], s5p['cre'], s5p['cim'],
        _row(p['s5_d']), p['s5_w_glu'], _row(p['s5_b_glu']))
    xs1 = _out_ln([ya_s, yb_s], [p['w_out0'][:D_RNN], p['w_out0'][D_RNN:]], xs,
                  ln(p['ln_g'], 0, 0), ln(p['ln_b'], 0, 0), n, "out0_sample")
    xs2, up0 = _ffn_sample(xs1, p['ffn_w_up'][0], p['ffn_w_gate'][0], ffn_conv[0, :, 0],
                           ffn_conv[0, :, 1], p['ffn_conv_w'][0], p['ffn_conv_b'][0][None],
                           p['ffn_w_down'][0], ln(p['ln_g'], 0, 1), ln(p['ln_b'], 0, 1))
    wf3_s, bf3_s = _forget_cols(p['w_in1'], p['b_fgt'], F32)
    qkv_s, lf3_s = _qkv_sample(xs2, p['w_in1'], wf3_s, bf3_s)
    q_s = qkv_s[:, :D_MODEL]
    k_s = qkv_s[:, D_MODEL:2 * D_MODEL]
    v_s = qkv_s[:, 2 * D_MODEL:]
    logf_s = lf3_s[:, :N_HEADS]

    o, o_s = _flash(qa, ka, va, _pick(t, tiles['tq']), _pick(t, tiles['tk']), tiles['qc'],
                    page_table, q_s, k_s, v_s, logf_s, cache_k, cache_v, cache_logf,
                    pages_per_step)

    x4, ffn_tail1 = _ffn_prompt(
        [o], [bf(p['w_out1'])], x2, ln(p['ln_g'], 1, 0), ln(p['ln_b'], 1, 0),
        bf(p['ffn_w_up'][1]), bf(p['ffn_w_gate'][1]), p['ffn_conv_w'][1],
        p['ffn_conv_b'][1][None], bf(p['ffn_w_down'][1]), ln(p['ln_g'], 1, 1),
        ln(p['ln_b'], 1, 1), tm)
    xs3 = _out_ln([o_s], [p['w_out1']], xs2, ln(p['ln_g'], 1, 0), ln(p['ln_b'], 1, 0), n,
                  "out1_sample")
    xs4, up1 = _ffn_sample(xs3, p['ffn_w_up'][1], p['ffn_w_gate'][1], ffn_conv[1, :, 0],
                           ffn_conv[1, :, 1], p['ffn_conv_w'][1], p['ffn_conv_b'][1][None],
                           p['ffn_w_down'][1], ln(p['ln_g'], 1, 1), ln(p['ln_b'], 1, 1))

    prompt = (x4[None], tail[None, 5:8], h_last[0:1], s_re[0:1].reshape(1, S5_GROUPS, S5_STATE),
              s_im[0:1].reshape(1, S5_GROUPS, S5_STATE), per_token(kt), per_token(vt),
              lft[:N_HEADS].T[None],
              jnp.stack([ffn_tail0[None, 6:8], ffn_tail1[None, 6:8]], axis=0))
    new_rg_conv = jnp.concatenate([rg_conv[:, 1:], xa_s[:, None]], axis=1)
    new_ffn = jnp.stack([jnp.stack([ffn_conv[0, :, 1], up0], axis=1),
                         jnp.stack([ffn_conv[1, :, 1], up1], axis=1)], axis=0)
    sample = (xs4[:, None], new_rg_conv, h_new, xr.reshape(n, S5_GROUPS, S5_STATE),
              xi.reshape(n, S5_GROUPS, S5_STATE), k_s.reshape(n, 1, N_HEADS, HEAD_DIM),
              v_s.reshape(n, 1, N_HEADS, HEAD_DIM), logf_s[:, None], new_ffn)
    return prompt, sample


def _derived_params(p):
    a_re, a_im, bb_re, bb_im = _s5_discretize(p['s5_lambda_re'], p['s5_lambda_im'], p['s5_log_dt'],
                                              p['s5_b_re'], p['s5_b_im'])
    return dict(
        wa_bd=_block_diag_dense(p['rg_wa']), wx_bd=_block_diag_dense(p['rg_wx']),
        c_lam=_row(-RG_C * jax.nn.softplus(-p['rg_lambda'])),
        a_re=a_re.reshape(1, N_STATE), a_im=a_im.reshape(1, N_STATE),
        **_s5_scan_tables(a_re, a_im),
        bbr=_s5_in_blocks(bb_re), bbi=_s5_in_blocks(bb_im),
        cre=_s5_out_blocks(p['s5_c_re']), cim=_s5_out_blocks(p['s5_c_im']))


PROMPT_TILES = dict(scan=256, rows=512, qkv=256, tq=1024, tk=1024, qc=256)
PAGES_PER_STEP = 4


def kernel(x_prompt, x_sample, state_rglru_conv, state_rglru_h, state_s5_re, state_s5_im, cache_k, cache_v, cache_logf, state_ffn_conv, page_table, w_in0, rg_conv_w, rg_conv_b, rg_wa, rg_ba, rg_wx, rg_bx, rg_lambda, s5_lambda_re, s5_lambda_im, s5_log_dt, s5_b_re, s5_b_im, s5_c_re, s5_c_im, s5_d, s5_w_glu, s5_b_glu, w_out0, w_in1, b_fgt, w_out1, ffn_w_up, ffn_w_gate, ffn_conv_w, ffn_conv_b, ffn_w_down, ln_g, ln_b):
    p = dict(w_in0=w_in0, rg_conv_w=rg_conv_w, rg_conv_b=rg_conv_b, rg_wa=rg_wa, rg_ba=rg_ba,
             rg_wx=rg_wx, rg_bx=rg_bx, rg_lambda=rg_lambda, s5_lambda_re=s5_lambda_re,
             s5_lambda_im=s5_lambda_im, s5_log_dt=s5_log_dt, s5_b_re=s5_b_re, s5_b_im=s5_b_im,
             s5_c_re=s5_c_re, s5_c_im=s5_c_im, s5_d=s5_d, s5_w_glu=s5_w_glu, s5_b_glu=s5_b_glu,
             w_out0=w_out0, w_in1=w_in1, b_fgt=b_fgt, w_out1=w_out1, ffn_w_up=ffn_w_up,
             ffn_w_gate=ffn_w_gate, ffn_conv_w=ffn_conv_w, ffn_conv_b=ffn_conv_b,
             ffn_w_down=ffn_w_down, ln_g=ln_g, ln_b=ln_b)
    s5p = _derived_params(p)
    prompt, sample = _forward(x_prompt[0], x_sample[:, 0], state_rglru_conv, state_rglru_h,
                              state_s5_re, state_s5_im, cache_k, cache_v, cache_logf,
                              state_ffn_conv, page_table, p, s5p, PROMPT_TILES, PAGES_PER_STEP)
    return (prompt[0], sample[0]) + tuple(prompt[1:]) + tuple(sample[1:])
```

```python
import functools

import jax
import jax.numpy as jnp
from jax import lax
from jax.experimental import pallas as pl
from jax.experimental.pallas import tpu as pltpu

F32 = jnp.float32
BF16 = jnp.bfloat16

D_MODEL = 1024
DEPTH = 2
PAGE_SIZE = 128
D_RNN = 512
RG_BLOCKS = 8
RG_BW = D_RNN // RG_BLOCKS
RG_C = 8.0
D_SSM = 512
S5_GROUP = 16
S5_GROUPS = D_SSM // S5_GROUP
S5_STATE = 64
N_STATE = S5_GROUPS * S5_STATE
N_HEADS = 16
HEAD_DIM = D_MODEL // N_HEADS
D_FF = 2816
ALPHA = (2.0 * DEPTH) ** 0.25
LN_EPS = 1e-5

LANES = 128
GROUPS_PER_BLOCK = LANES // S5_GROUP
N_SSM_BLOCKS = D_SSM // LANES
STATE_BLOCK = GROUPS_PER_BLOCK * S5_STATE
FF_CHUNK = 256
N_FF_CHUNKS = D_FF // FF_CHUNK
NEG_BIG = -1e30
LOG2E = 1.4426950408889634
AUG_COLS = 6
VMEM_LIMIT = 56 * 1024 * 1024


def _cparams(n_axes=1, vmem=VMEM_LIMIT):
    return pltpu.CompilerParams(dimension_semantics=("arbitrary",) * n_axes,
                                vmem_limit_bytes=vmem)


def _split3(x):
    h1 = x.astype(BF16)
    r1 = x - h1.astype(F32)
    h2 = r1.astype(BF16)
    h3 = (r1 - h2.astype(F32)).astype(BF16)
    return h1, h2, h3


def _dotw(x, w):
    if w.dtype == BF16:
        return jnp.dot(x.astype(BF16), w, preferred_element_type=F32)
    xh = x.astype(BF16)
    xl = (x - xh.astype(F32)).astype(BF16)
    wh = w.astype(BF16)
    wl = (w - wh.astype(F32)).astype(BF16)
    return (jnp.dot(xh, wh, preferred_element_type=F32)
            + jnp.dot(xl, wh, preferred_element_type=F32)
            + jnp.dot(xh, wl, preferred_element_type=F32))


def _dot_exact01(m01, x):
    h1, h2, h3 = _split3(x)
    return (jnp.dot(m01, h1, preferred_element_type=F32)
            + jnp.dot(m01, h2, preferred_element_type=F32)
            + jnp.dot(m01, h3, preferred_element_type=F32))


def _gelu(x):
    return jax.nn.gelu(x)


def _sigmoid(x):
    return jax.nn.sigmoid(x)


def _log_sigmoid(x):
    return jnp.minimum(x, 0.0) - jnp.log1p(jnp.exp(-jnp.abs(x)))


def _expm1(x):
    u = jnp.exp(x)
    um1 = u - 1.0
    return jnp.where(um1 == 0.0, x, jnp.where(um1 == -1.0, -1.0, um1 * x / jnp.log(u)))


def _layer_norm(xf, g, b):
    mean = jnp.mean(xf, axis=-1, keepdims=True)
    xc = xf - mean
    var = jnp.mean(xc * xc, axis=-1, keepdims=True)
    return xc * lax.rsqrt(var + LN_EPS) * g + b


def _rglru_gates(xc, wa, ba, wx, bx, c_lam):
    r = _sigmoid(_dotw(xc, wa) + ba)
    i = _sigmoid(_dotw(xc, wx) + bx)
    log_a = c_lam * r
    a = jnp.exp(log_a)
    inp = jnp.sqrt(-_expm1(2.0 * log_a)) * (i * xc)
    return a, inp


def _scan_rows(a, b):
    n = a.shape[0]
    row = lax.broadcasted_iota(jnp.int32, a.shape, 0)
    d = 1
    while d < n:
        valid = row >= d
        b = b + jnp.where(valid, a * pltpu.roll(b, d, 0), 0.0)
        if 2 * d < n:
            a = jnp.where(valid, a * pltpu.roll(a, d, 0), a)
        d *= 2
    return b


def _lagged_inputs(u):
    sub = lax.broadcasted_iota(jnp.int32, u.shape, 0) % 8
    lags = [u]
    for k in range(1, 8):
        lags.append(jnp.where(sub >= k, pltpu.roll(u, k, 0), 0.0))
    return jnp.concatenate(lags, axis=1)


def _chain_groups(pow_re, pow_im, br, bi, pr, pi):
    n, c = br.shape
    g = n // 8
    b3r = br.reshape(g, 8, c)
    b3i = bi.reshape(g, 8, c)
    outs_r, outs_i = [], []
    for gi in range(g):
        xr = b3r[gi] + (pow_re * pr - pow_im * pi)
        xi = b3i[gi] + (pow_re * pi + pow_im * pr)
        pr, pi = xr[7:8, :], xi[7:8, :]
        outs_r.append(xr)
        outs_i.append(xi)
    return jnp.concatenate(outs_r, axis=0), jnp.concatenate(outs_i, axis=0)


def _rglru_prompt_kernel(x_ref, w_ref, cw_ref, cb_ref, wa_ref, ba_ref, wx_ref, bx_ref, cl_ref,
                         ya_ref, tail_ref, h_ref, ext_scr, h_scr, *, tb):
    step = pl.program_id(0)

    @pl.when(step == 0)
    def _():
        ext_scr[0:8, :] = jnp.zeros((8, D_RNN), F32)
        h_scr[...] = jnp.zeros_like(h_scr)

    proj = _dotw(x_ref[...], w_ref[...])
    xa = proj[:, :D_RNN]
    ga = proj[:, D_RNN:]
    ext_scr[8:8 + tb, :] = xa
    xc = cb_ref[...] + xa * cw_ref[3:4, :]
    for j in range(3):
        xc = xc + ext_scr[pl.ds(5 + j, tb), :] * cw_ref[j:j + 1, :]
    tail = ext_scr[tb:tb + 8, :]
    ext_scr[0:8, :] = tail
    tail_ref[...] = tail

    a, inp = _rglru_gates(xc, wa_ref[...], ba_ref[...], wx_ref[...], bx_ref[...], cl_ref[...])
    row = lax.broadcasted_iota(jnp.int32, a.shape, 0)
    inp = inp + jnp.where(row == 0, a * h_scr[0:1, :], 0.0)
    h = _scan_rows(a, inp)
    h_last = jnp.broadcast_to(h[tb - 1:tb, :], (8, D_RNN))
    h_scr[...] = h_last
    h_ref[...] = h_last
    ya_ref[...] = h * _gelu(ga)


def _rglru_prompt(x, w_ag, cw, cb, wa, ba, wx, bx, c_lam, tb):
    t = x.shape[0]
    full = lambda shape: pl.BlockSpec(shape, lambda i: (0,) * len(shape))
    return pl.pallas_call(
        functools.partial(_rglru_prompt_kernel, tb=tb),
        grid=(t // tb,),
        in_specs=[pl.BlockSpec((tb, D_MODEL), lambda i: (i, 0)),
                  full(w_ag.shape), full(cw.shape), full(cb.shape), full(wa.shape), full(ba.shape),
                  full(wx.shape), full(bx.shape), full(c_lam.shape)],
        out_specs=[pl.BlockSpec((tb, D_RNN), lambda i: (i, 0)),
                   pl.BlockSpec((8, D_RNN), lambda i: (0, 0)),
                   pl.BlockSpec((8, D_RNN), lambda i: (0, 0))],
        out_shape=[jax.ShapeDtypeStruct((t, D_RNN), F32),
                   jax.ShapeDtypeStruct((8, D_RNN), F32),
                   jax.ShapeDtypeStruct((8, D_RNN), F32)],
        scratch_shapes=[pltpu.VMEM((tb + 8, D_RNN), F32), pltpu.VMEM((8, D_RNN), F32)],
        compiler_params=_cparams(),
        name="rglru_prompt",
    )(x, w_ag, cw, cb, wa, ba, wx, bx, c_lam)


def _s5_readout(u, xr_blocks, xi_blocks, cre_ref, cim_ref, d, wglu, bglu):
    ys = []
    for j in range(N_SSM_BLOCKS):
        ys.append(_dotw(xr_blocks[j], cre_ref[j]) - _dotw(xi_blocks[j], cim_ref[j]))
    y = jnp.concatenate(ys, axis=1) + d * u
    g = _gelu(y)
    return g * _sigmoid(_dotw(g, wglu) + bglu)


def _s5_prompt_kernel(x_ref, w_ref, wlr_ref, wli_ref, pwr_ref, pwi_ref, cre_ref, cim_ref, d_ref,
                      wglu_ref, bglu_ref, yb_ref, sre_ref, sim_ref, sr_scr, si_scr, *, tb):
    step = pl.program_id(0)

    @pl.when(step == 0)
    def _():
        sr_scr[...] = jnp.zeros_like(sr_scr)
        si_scr[...] = jnp.zeros_like(si_scr)

    u = _dotw(x_ref[...], w_ref[...])
    xr_blocks, xi_blocks = [], []
    for j in range(N_SSM_BLOCKS):
        cols = slice(j * STATE_BLOCK, (j + 1) * STATE_BLOCK)
        lagged = _lagged_inputs(u[:, j * LANES:(j + 1) * LANES]).astype(BF16)
        br = jnp.dot(lagged, wlr_ref[j], preferred_element_type=F32)
        bi = jnp.dot(lagged, wli_ref[j], preferred_element_type=F32)
        xr, xi = _chain_groups(pwr_ref[:, cols], pwi_ref[:, cols], br, bi,
                               sr_scr[0:1, cols], si_scr[0:1, cols])
        sr_scr[:, cols] = jnp.broadcast_to(xr[tb - 1:tb, :], (8, STATE_BLOCK))
        si_scr[:, cols] = jnp.broadcast_to(xi[tb - 1:tb, :], (8, STATE_BLOCK))
        xr_blocks.append(xr)
        xi_blocks.append(xi)
    sre_ref[...] = sr_scr[...]
    sim_ref[...] = si_scr[...]
    yb_ref[...] = _s5_readout(u, xr_blocks, xi_blocks, cre_ref, cim_ref, d_ref[...],
                              wglu_ref[...], bglu_ref[...])


def _s5_prompt(x, w_b, wl_re, wl_im, pow_re, pow_im, cre, cim, d, wglu, bglu, tb):
    t = x.shape[0]
    full = lambda a: pl.BlockSpec(a.shape, lambda i: (0,) * a.ndim)
    consts = [w_b, wl_re, wl_im, pow_re, pow_im, cre, cim, d, wglu, bglu]
    return pl.pallas_call(
        functools.partial(_s5_prompt_kernel, tb=tb),
        grid=(t // tb,),
        in_specs=[pl.BlockSpec((tb, D_MODEL), lambda i: (i, 0))] + [full(c) for c in consts],
        out_specs=[pl.BlockSpec((tb, D_SSM), lambda i: (i, 0)),
                   pl.BlockSpec((8, N_STATE), lambda i: (0, 0)),
                   pl.BlockSpec((8, N_STATE), lambda i: (0, 0))],
        out_shape=[jax.ShapeDtypeStruct((t, D_SSM), F32),
                   jax.ShapeDtypeStruct((8, N_STATE), F32),
                   jax.ShapeDtypeStruct((8, N_STATE), F32)],
        scratch_shapes=[pltpu.VMEM((8, N_STATE), F32), pltpu.VMEM((8, N_STATE), F32)],
        compiler_params=_cparams(),
        name="s5_prompt",
    )(x, *consts)


def _out_ln_kernel(*refs, n_in):
    a_refs = refs[:n_in]
    w_refs = refs[n_in:2 * n_in]
    x_ref, g_ref, b_ref, o_ref = refs[2 * n_in:]
    y = ALPHA * x_ref[...]
    for a_ref, w_ref in zip(a_refs, w_refs):
        y = y + _dotw(a_ref[...], w_ref[...])
    o_ref[...] = _layer_norm(y, g_ref[...], b_ref[...])


def _out_ln(acts, weights, x, g, b, tm, name):
    m = x.shape[0]
    n_in = len(acts)
    full = lambda shape: pl.BlockSpec(shape, lambda i: (0,) * len(shape))
    in_specs = ([pl.BlockSpec((tm, a.shape[1]), lambda i: (i, 0)) for a in acts]
                + [full(w.shape) for w in weights]
                + [pl.BlockSpec((tm, D_MODEL), lambda i: (i, 0)), full(g.shape), full(b.shape)])
    return pl.pallas_call(
        functools.partial(_out_ln_kernel, n_in=n_in),
        grid=(m // tm,),
        in_specs=in_specs,
        out_specs=pl.BlockSpec((tm, D_MODEL), lambda i: (i, 0)),
        out_shape=jax.ShapeDtypeStruct((m, D_MODEL), F32),
        compiler_params=_cparams(),
        name=name,
    )(*acts, *weights, x, g, b)


def _ffn_prompt_kernel(*refs, n_in, tm):
    a_refs = refs[:n_in]
    w_refs = refs[n_in:2 * n_in]
    (x_ref, g0_ref, b0_ref, wup_ref, wgate_ref, cw_ref, cb_ref, wdown_ref, g_ref, b_ref,
     o_ref, tail_ref, h_scr, carry_scr) = refs[2 * n_in:]
    step = pl.program_id(0)

    @pl.when(step == 0)
    def _():
        carry_scr[...] = jnp.zeros_like(carry_scr)

    y = ALPHA * x_ref[...]
    for a_ref, w_ref in zip(a_refs, w_refs):
        y = y + _dotw(a_ref[...], w_ref[...])
    x = _layer_norm(y, g0_ref[...], b0_ref[...])
    xb = x.astype(BF16)
    row = lax.broadcasted_iota(jnp.int32, (tm, FF_CHUNK), 0)
    for c in range(N_FF_CHUNKS):
        cols = slice(c * FF_CHUNK, (c + 1) * FF_CHUNK)
        up = jnp.dot(xb, wup_ref[:, cols], preferred_element_type=F32)
        gate = jnp.dot(xb, wgate_ref[:, cols], preferred_element_type=F32)
        prev1 = carry_scr[7:8, cols]
        prev2 = carry_scr[6:7, cols]
        up1 = jnp.where(row >= 1, pltpu.roll(up, 1, 0), prev1)
        up2 = jnp.where(row >= 2, pltpu.roll(up, 2, 0), jnp.where(row == 1, prev1, prev2))
        hc = (cb_ref[:, cols] + up2 * cw_ref[0:1, cols] + up1 * cw_ref[1:2, cols]
              + up * cw_ref[2:3, cols])
        h_scr[:, cols] = (_gelu(hc) * gate).astype(BF16)
        carry_scr[:, cols] = up[tm - 8:tm, :]
    tail_ref[...] = carry_scr[...]
    f = jnp.dot(h_scr[...], wdown_ref[...], preferred_element_type=F32)
    o_ref[...] = _layer_norm(ALPHA * x + f, g_ref[...], b_ref[...])


def _ffn_prompt(acts, weights, x, g0, b0, wup, wgate, cw, cb, wdown, g, b, tm):
    t = x.shape[0]
    n_in = len(acts)
    const = lambda a: pl.BlockSpec(a.shape, lambda i: (0,) * a.ndim,
                                   pipeline_mode=pl.Buffered(1))
    rows = lambda a: pl.BlockSpec((tm, a.shape[1]), lambda i: (i, 0))
    consts = [g0, b0, wup, wgate, cw, cb, wdown, g, b]
    return pl.pallas_call(
        functools.partial(_ffn_prompt_kernel, n_in=n_in, tm=tm),
        grid=(t // tm,),
        in_specs=([rows(a) for a in acts] + [const(w) for w in weights] + [rows(x)]
                  + [const(c) for c in consts]),
        out_specs=[pl.BlockSpec((tm, D_MODEL), lambda i: (i, 0)),
                   pl.BlockSpec((8, D_FF), lambda i: (0, 0))],
        out_shape=[jax.ShapeDtypeStruct((t, D_MODEL), F32),
                   jax.ShapeDtypeStruct((8, D_FF), F32)],
        scratch_shapes=[pltpu.VMEM((tm, D_FF), BF16), pltpu.VMEM((8, D_FF), F32)],
        compiler_params=_cparams(),
        name="ffn_prompt",
    )(*acts, *weights, x, *consts)


def _aug_lhs(f3):
    h1, h2, h3 = (h.astype(F32) for h in _split3(f3))
    lane = lax.broadcasted_iota(jnp.int32, f3.shape, 1)
    pieces = jnp.where(lane < N_HEADS, h1, jnp.where(lane < 2 * N_HEADS, h2, jnp.where(
        lane < 3 * N_HEADS, h3, jnp.where(lane < 4 * N_HEADS, 1.0, 0.0))))
    return pieces.astype(BF16)


def _qkv_prompt_kernel(x_ref, w_ref, wf_ref, bf_ref, selq_ref, selk_ref,
                       kt_ref, vt_ref, lft_ref, qa_ref, ka_ref, va_ref, f_scr, *, tm):
    step = pl.program_id(0)

    @pl.when(step == 0)
    def _():
        f_scr[...] = jnp.zeros_like(f_scr)

    x = x_ref[...]
    qkv = _dotw(x, w_ref[...])
    k = qkv[:, D_MODEL:2 * D_MODEL]
    v = qkv[:, 2 * D_MODEL:]
    logf3 = _log_sigmoid(_dotw(x, wf_ref[...]) + bf_ref[...])
    lft_ref[...] = logf3.T

    r_i = lax.broadcasted_iota(jnp.int32, (tm, tm), 0)
    c_i = lax.broadcasted_iota(jnp.int32, (tm, tm), 1)
    tri = jnp.where(r_i >= c_i, 1.0, 0.0).astype(BF16)
    f3 = _dot_exact01(tri, logf3) + f_scr[0:1, :]
    f_scr[...] = jnp.broadcast_to(f3[tm - 1:tm, :], f_scr.shape)

    lhs = _aug_lhs(f3 * LOG2E)
    augq = jnp.dot(lhs, selq_ref[...], preferred_element_type=F32)
    augk = jnp.dot(lhs, selk_ref[...], preferred_element_type=F32)
    lane = lax.broadcasted_iota(jnp.int32, (tm, LANES), 1)
    low = lane < HEAD_DIM
    vtail = jnp.where(lane == HEAD_DIM, 1.0, 0.0)
    scale = HEAD_DIM ** -0.5 * LOG2E
    for hp in range(N_HEADS // 2):
        cols = slice(hp * LANES, (hp + 1) * LANES)
        qb = qkv[:, cols] * scale
        kb = k[:, cols]
        vb = v[:, cols]
        kt_ref[cols, :] = kb.T
        vt_ref[cols, :] = vb.T
        for odd in range(2):
            h = 2 * hp + odd
            acols = slice(h * LANES, (h + 1) * LANES)
            if odd:
                qb, kb, vb = (pltpu.roll(qb, HEAD_DIM, 1), pltpu.roll(kb, HEAD_DIM, 1),
                              pltpu.roll(vb, HEAD_DIM, 1))
            qa_ref[h] = jnp.where(low, qb, augq[:, acols]).astype(BF16)
            ka_ref[h] = jnp.where(low, kb, augk[:, acols]).astype(BF16)
            va_ref[h] = jnp.where(low, vb, vtail).astype(BF16)


def _qkv_prompt(x, w_qkv, wf3, bf3, selq, selk, tm):
    t = x.shape[0]
    full = lambda shape: pl.BlockSpec(shape, lambda i: (0,) * len(shape))
    head_major = jax.ShapeDtypeStruct((N_HEADS, t, LANES), BF16)
    head_spec = pl.BlockSpec((N_HEADS, tm, LANES), lambda i: (0, i, 0))
    return pl.pallas_call(
        functools.partial(_qkv_prompt_kernel, tm=tm),
        grid=(t // tm,),
        in_specs=[pl.BlockSpec((tm, D_MODEL), lambda i: (i, 0)),
                  full(w_qkv.shape), full(wf3.shape), full(bf3.shape), full(selq.shape),
                  full(selk.shape)],
        out_specs=[pl.BlockSpec((D_MODEL, tm), lambda i: (0, i)),
                   pl.BlockSpec((D_MODEL, tm), lambda i: (0, i)),
                   pl.BlockSpec((LANES, tm), lambda i: (0, i)),
                   head_spec, head_spec, head_spec],
        out_shape=[jax.ShapeDtypeStruct((D_MODEL, t), F32),
                   jax.ShapeDtypeStruct((D_MODEL, t), F32),
                   jax.ShapeDtypeStruct((LANES, t), F32),
                   head_major, head_major, head_major],
        scratch_shapes=[pltpu.VMEM((8, LANES), F32)],
        compiler_params=_cparams(),
        name="qkv_prompt",
    )(x, w_qkv, wf3, bf3, selq, selk)


def _repeat_rows(x):
    return jnp.broadcast_to(x[:, None, :], (N_HEADS, HEAD_DIM, x.shape[1])).reshape(
        D_MODEL, x.shape[1])


def _head_sum(x):
    return jnp.sum(x.reshape(N_HEADS, HEAD_DIM, x.shape[1]), axis=1)


def _flash_kernel(it_ref, jt_ref, pt_ref, q_ref, k_ref, v_ref, qb_ref, kn_ref, vn_ref, gt_ref,
                  *refs, tq, tk, qc, pp, n_tri, steps_per_seq, n_seq):
    k_pages = refs[:pp]
    v_pages = refs[pp:2 * pp]
    lf_pages = refs[2 * pp:3 * pp]
    o_ref, os_ref, m_scr, acc_scr, sm_scr, sl_scr, sacc_scr, scarry_scr = refs[3 * pp:]
    s_idx = pl.program_id(1)
    i = it_ref[s_idx]
    j = jt_ref[s_idx]
    t = pl.program_id(0) * n_tri + s_idx
    grp = t % steps_per_seq
    seq_active = t < n_seq * steps_per_seq
    first_group = grp == 0

    @pl.when(t == 0)
    def _():
        sm_scr[...] = jnp.zeros_like(sm_scr)
        sl_scr[...] = jnp.zeros_like(sl_scr)
        sacc_scr[...] = jnp.zeros_like(sacc_scr)
        scarry_scr[...] = jnp.zeros_like(scarry_scr)

    @pl.when(j == 0)
    def _():
        m_scr[...] = jnp.full(m_scr.shape, NEG_BIG, F32)
        acc_scr[...] = jnp.zeros_like(acc_scr)

    r_i = lax.broadcasted_iota(jnp.int32, (PAGE_SIZE, PAGE_SIZE), 0)
    c_i = lax.broadcasted_iota(jnp.int32, (PAGE_SIZE, PAGE_SIZE), 1)
    later = jnp.where(r_i > c_i, 1.0, 0.0).astype(BF16)

    def sample_pieces():
        st = {}

        def qk(r):
            def run():
                kt = k_pages[r][0].reshape(D_MODEL, PAGE_SIZE)
                st['s%d' % r] = _head_sum(kt * qb_ref[0])
            return run

        def softmax():
            gt = gt_ref[0]
            carry = jnp.where(first_group, 0.0, scarry_scr[:, 0:1])
            logits = []
            for r in range(pp):
                lf = lf_pages[r][0]
                suf = sum(jnp.dot(pc, later, preferred_element_type=F32) for pc in _split3(lf))
                logits.append(st['s%d' % r] + suf + (carry + gt))
                carry = carry + jnp.sum(lf, axis=1, keepdims=True)
            scarry_scr[...] = jnp.broadcast_to(carry, scarry_scr.shape)
            mx = logits[0]
            for lg in logits[1:]:
                mx = jnp.maximum(mx, lg)
            m_prev = jnp.where(first_group, NEG_BIG, sm_scr[:, 0:1])
            l_prev = jnp.where(first_group, 0.0, sl_scr[:, 0:1])
            m_next = jnp.maximum(m_prev, jnp.max(mx, axis=1, keepdims=True))
            alpha = jnp.exp(m_prev - m_next)
            ps = [jnp.exp(lg - m_next) for lg in logits]
            psum = ps[0]
            for pblk in ps[1:]:
                psum = psum + pblk
            sl_scr[...] = jnp.broadcast_to(
                alpha * l_prev + jnp.sum(psum, axis=1, keepdims=True), sl_scr.shape)
            sm_scr[...] = jnp.broadcast_to(m_next, sm_scr.shape)
            st['ps'] = ps
            st['alpha'] = jnp.broadcast_to(alpha, (N_HEADS, PAGE_SIZE))

        def pv(r):
            def run():
                vt = v_pages[r][0].reshape(D_MODEL, PAGE_SIZE)
                if r == 0:
                    st['acc'] = _repeat_rows(st['alpha']) * sacc_scr[...]
                st['acc'] = st['acc'] + _repeat_rows(st['ps'][r]) * vt
                if r == pp - 1:
                    sacc_scr[...] = st['acc']
            return run

        return [qk(r) for r in range(pp)] + [softmax] + [pv(r) for r in range(pp)]

    def accumulate(masked):
        chunks = [(hh, c) for hh in range(2) for c in range(tq // qc)]
        pieces = sample_pieces()

        def n_keys(c):
            return (c + 1) * qc if masked else tk

        def scores(hh, c):
            rows = slice(c * qc, (c + 1) * qc)
            nk = n_keys(c)
            s = lax.dot_general(q_ref[hh, rows, :], k_ref[hh, 0:nk, :], (((1,), (1,)), ((), ())),
                                preferred_element_type=F32)
            if masked:
                qpos = c * qc + lax.broadcasted_iota(jnp.int32, (qc, nk), 0)
                kpos = lax.broadcasted_iota(jnp.int32, (qc, nk), 1)
                s = jnp.where(kpos <= qpos, s, NEG_BIG)
            return s

        ahead = 2
        pending = [scores(*chunks[n]) for n in range(ahead)]
        per_chunk = -(-len(pieces) // len(chunks))
        for n, (hh, c) in enumerate(chunks):
            s = pending.pop(0)
            if n + ahead < len(chunks):
                pending.append(scores(*chunks[n + ahead]))
            for piece in pieces[n * per_chunk:(n + 1) * per_chunk]:
                piece()
            rows = slice(c * qc, (c + 1) * qc)
            m_prev = m_scr[hh, rows, :]
            m_next = jnp.maximum(m_prev, jnp.max(s, axis=1, keepdims=True))
            alpha = jnp.exp2(m_prev - m_next)
            p = jnp.exp2(s - m_next[:, 0:1]).astype(BF16)
            acc_scr[hh, rows, :] = alpha * acc_scr[hh, rows, :] + jnp.dot(
                p, v_ref[hh, 0:n_keys(c), :], preferred_element_type=F32)
            m_scr[hh, rows, :] = m_next

    assert tq == tk, "the diagonal-block key trimming assumes square blocks"
    crosses_diagonal = j == i

    @pl.when(crosses_diagonal)
    def _():
        accumulate(True)

    @pl.when(jnp.logical_not(crosses_diagonal))
    def _():
        accumulate(False)

    @pl.when(j == ((i + 1) * tq - 1) // tk)
    def _():
        lane = lax.broadcasted_iota(jnp.int32, (tq, LANES), 1)
        outs = []
        for hh in range(2):
            acc = acc_scr[hh]
            outs.append(acc / acc[:, HEAD_DIM:HEAD_DIM + 1])
        o_ref[...] = jnp.where(lane < HEAD_DIM, outs[0], pltpu.roll(outs[1], HEAD_DIM, 1))

    @pl.when(jnp.logical_and(seq_active, grp == steps_per_seq - 1))
    def _():
        qcol = qb_ref[0][:, 0:1]
        s_self = _head_sum(qcol * kn_ref[0])
        m_prev = sm_scr[:, 0:1]
        m_fin = jnp.maximum(m_prev, s_self)
        a_past = jnp.exp(m_prev - m_fin)
        a_self = jnp.exp(s_self - m_fin)
        l_fin = a_past * sl_scr[:, 0:1] + a_self
        past = jnp.sum(sacc_scr[...], axis=1, keepdims=True)
        os_ref[0] = (_repeat_rows(a_past) * past + _repeat_rows(a_self) * vn_ref[0]) / _repeat_rows(
            l_fin)


def _flash(qa, ka, va, tq, tk, qc, page_table, q_s, k_s, v_s, logf_s, cache_k, cache_v,
           cache_logf, pp):
    t = ka.shape[1]
    nq = t // tq
    i_list, j_list = [], []
    for i in range(nq):
        for j in range(((i + 1) * tq - 1) // tk + 1):
            i_list.append(i)
            j_list.append(j)
    n_tri = len(i_list)
    it = jnp.asarray(i_list, jnp.int32)
    jt = jnp.asarray(j_list, jnp.int32)

    n_seq, n_pages = page_table.shape
    steps_per_seq = n_pages // pp
    n_steps = (N_HEADS // 2) * n_tri
    assert n_pages % pp == 0 and n_seq * steps_per_seq <= n_steps
    pt = page_table.reshape(-1)
    ck = jnp.transpose(cache_k, (0, 2, 3, 1))
    cv = jnp.transpose(cache_v, (0, 2, 3, 1))
    clf = jnp.transpose(cache_logf, (0, 2, 1))
    scale = HEAD_DIM ** -0.5
    qb = jnp.broadcast_to((q_s * scale)[:, :, None], (n_seq, D_MODEL, PAGE_SIZE))
    col = lambda a: a.reshape(n_seq, D_MODEL, 1)

    def seq_of(hp, s):
        return jnp.minimum((hp * n_tri + s) // steps_per_seq, n_seq - 1)

    def page_map(r, ndim):
        def index(hp, s, it, jt, pt):
            step = hp * n_tri + s
            grp = jnp.where(step < n_seq * steps_per_seq, step % steps_per_seq,
                            steps_per_seq - 1)
            page = pt[seq_of(hp, s) * n_pages + (n_pages - 1 - (grp * pp + r))]
            return (page,) + (0,) * ndim
        return index

    per_seq = lambda shape: pl.BlockSpec((1,) + shape,
                                         lambda hp, s, it, jt, pt: (seq_of(hp, s), 0, 0))
    kv_block = (1, N_HEADS, HEAD_DIM, PAGE_SIZE)
    in_specs = [pl.BlockSpec((2, tq, LANES), lambda hp, s, it, jt, pt: (hp, it[s], 0)),
                pl.BlockSpec((2, tk, LANES), lambda hp, s, it, jt, pt: (hp, jt[s], 0)),
                pl.BlockSpec((2, tk, LANES), lambda hp, s, it, jt, pt: (hp, jt[s], 0)),
                per_seq((D_MODEL, PAGE_SIZE)), per_seq((D_MODEL, 1)), per_seq((D_MODEL, 1)),
                per_seq((N_HEADS, 1))]
    in_specs += [pl.BlockSpec(kv_block, page_map(r, 3)) for r in range(pp)]
    in_specs += [pl.BlockSpec(kv_block, page_map(r, 3)) for r in range(pp)]
    in_specs += [pl.BlockSpec((1, N_HEADS, PAGE_SIZE), page_map(r, 2)) for r in range(pp)]
    grid_spec = pltpu.PrefetchScalarGridSpec(
        num_scalar_prefetch=3,
        grid=(N_HEADS // 2, n_tri),
        in_specs=in_specs,
        out_specs=[pl.BlockSpec((tq, LANES), lambda hp, s, it, jt, pt: (it[s], hp)),
                   per_seq((D_MODEL, 1))],
        scratch_shapes=[pltpu.VMEM((2, tq, LANES), F32), pltpu.VMEM((2, tq, LANES), F32),
                        pltpu.VMEM((N_HEADS, LANES), F32), pltpu.VMEM((N_HEADS, LANES), F32),
                        pltpu.VMEM((D_MODEL, PAGE_SIZE), F32), pltpu.VMEM((N_HEADS, LANES), F32)],
    )
    o, o_s = pl.pallas_call(
        functools.partial(_flash_kernel, tq=tq, tk=tk, qc=qc, pp=pp, n_tri=n_tri,
                          steps_per_seq=steps_per_seq, n_seq=n_seq),
        grid_spec=grid_spec,
        out_shape=[jax.ShapeDtypeStruct((t, D_MODEL), F32),
                   jax.ShapeDtypeStruct((n_seq, D_MODEL, 1), F32)],
        compiler_params=_cparams(2),
        name="flash_prompt",
    )(it, jt, pt, qa, ka, va, qb, col(k_s), col(v_s), logf_s.reshape(n_seq, N_HEADS, 1),
      *([ck] * pp), *([cv] * pp), *([clf] * pp))
    return o, o_s.reshape(n_seq, D_MODEL)


def _mix0_sample_kernel(x_ref, w_ref, c0_ref, c1_ref, c2_ref, cw_ref, cb_ref, wa_ref, ba_ref,
                        wx_ref, bx_ref, cl_ref, h0_ref, sr_ref, si_ref, bbr_ref, bbi_ref,
                        ar_ref, ai_ref, cre_ref, cim_ref, d_ref, wglu_ref, bglu_ref,
                        ya_ref, yb_ref, xa_ref, h_ref, xr_ref, xi_ref):
    proj = _dotw(x_ref[...], w_ref[...])
    xa = proj[:, :D_RNN]
    ga = proj[:, D_RNN:2 * D_RNN]
    u = proj[:, 2 * D_RNN:]
    xa_ref[...] = xa
    xc = (cb_ref[...] + c0_ref[...] * cw_ref[0:1, :] + c1_ref[...] * cw_ref[1:2, :]
          + c2_ref[...] * cw_ref[2:3, :] + xa * cw_ref[3:4, :])
    a, inp = _rglru_gates(xc, wa_ref[...], ba_ref[...], wx_ref[...], bx_ref[...], cl_ref[...])
    h = a * h0_ref[...] + inp
    h_ref[...] = h
    ya_ref[...] = h * _gelu(ga)

    xr_blocks, xi_blocks = [], []
    for j in range(N_SSM_BLOCKS):
        cols = slice(j * STATE_BLOCK, (j + 1) * STATE_BLOCK)
        uj = u[:, j * LANES:(j + 1) * LANES]
        ar = ar_ref[:, cols]
        ai = ai_ref[:, cols]
        pr = sr_ref[:, cols]
        pi = si_ref[:, cols]
        xr = _dotw(uj, bbr_ref[j]) + (ar * pr - ai * pi)
        xi = _dotw(uj, bbi_ref[j]) + (ar * pi + ai * pr)
        xr_ref[:, cols] = xr
        xi_ref[:, cols] = xi
        xr_blocks.append(xr)
        xi_blocks.append(xi)
    yb_ref[...] = _s5_readout(u, xr_blocks, xi_blocks, cre_ref, cim_ref, d_ref[...],
                              wglu_ref[...], bglu_ref[...])


def _mix0_sample(*args):
    n = args[0].shape[0]
    shapes = [(n, D_RNN), (n, D_SSM), (n, D_RNN), (n, D_RNN), (n, N_STATE), (n, N_STATE)]
    return pl.pallas_call(
        _mix0_sample_kernel,
        out_shape=[jax.ShapeDtypeStruct(s, F32) for s in shapes],
        compiler_params=pltpu.CompilerParams(vmem_limit_bytes=VMEM_LIMIT),
        name="mix0_sample",
    )(*args)


def _ffn_sample_kernel(x_ref, wup_ref, wgate_ref, b0_ref, b1_ref, cw_ref, cb_ref, wdown_ref,
                       g_ref, b_ref, o_ref, up_ref, acc_scr):
    c = pl.program_id(0)

    @pl.when(c == 0)
    def _():
        acc_scr[...] = jnp.zeros_like(acc_scr)

    x = x_ref[...]
    up = _dotw(x, wup_ref[...])
    gate = _dotw(x, wgate_ref[...])
    up_ref[...] = up
    hc = (cb_ref[...] + b0_ref[...] * cw_ref[0:1, :] + b1_ref[...] * cw_ref[1:2, :]
          + up * cw_ref[2:3, :])
    acc_scr[...] += _dotw(_gelu(hc) * gate, wdown_ref[...])

    @pl.when(c == pl.num_programs(0) - 1)
    def _():
        o_ref[...] = _layer_norm(ALPHA * x + acc_scr[...], g_ref[...], b_ref[...])


def _ffn_sample(x, wup, wgate, buf0, buf1, cw, cb, wdown, g, b):
    n = x.shape[0]
    fc = FF_CHUNK
    col = lambda rows: pl.BlockSpec((rows, fc), lambda c: (0, c))
    full = lambda shape: pl.BlockSpec(shape, lambda c: (0,) * len(shape))
    return pl.pallas_call(
        _ffn_sample_kernel,
        grid=(D_FF // fc,),
        in_specs=[full(x.shape), col(D_MODEL), col(D_MODEL), col(n), col(n), col(cw.shape[0]),
                  col(1), pl.BlockSpec((fc, D_MODEL), lambda c: (c, 0)), full(g.shape),
                  full(b.shape)],
        out_specs=[full((n, D_MODEL)), col(n)],
        out_shape=[jax.ShapeDtypeStruct((n, D_MODEL), F32), jax.ShapeDtypeStruct((n, D_FF), F32)],
        scratch_shapes=[pltpu.VMEM((n, D_MODEL), F32)],
        compiler_params=_cparams(),
        name="ffn_sample",
    )(x, wup, wgate, buf0, buf1, cw, cb, wdown, g, b)


def _qkv_sample_kernel(x_ref, w_ref, wf_ref, bf_ref, o_ref, lf_ref):
    x = x_ref[...]
    o_ref[...] = _dotw(x, w_ref[...])

    @pl.when(pl.program_id(0) == 0)
    def _():
        lf_ref[...] = _log_sigmoid(_dotw(x, wf_ref[...]) + bf_ref[...])


def _qkv_sample(x, w_in1, wf, bf):
    n = x.shape[0]
    full = lambda shape: pl.BlockSpec(shape, lambda c: (0,) * len(shape))
    return pl.pallas_call(
        _qkv_sample_kernel,
        grid=(3,),
        in_specs=[full(x.shape), pl.BlockSpec((D_MODEL, D_MODEL), lambda c: (0, c)),
                  full(wf.shape), full(bf.shape)],
        out_specs=[pl.BlockSpec((n, D_MODEL), lambda c: (0, c)), full((n, LANES))],
        out_shape=[jax.ShapeDtypeStruct((n, 3 * D_MODEL), F32),
                   jax.ShapeDtypeStruct((n, LANES), F32)],
        compiler_params=_cparams(),
        name="qkv_sample",
    )(x, w_in1, wf, bf)


def _block_diag_dense(w):
    nb, n, _ = w.shape
    eye = jnp.eye(nb, dtype=w.dtype)
    return jnp.einsum('hij,hk->hikj', w, eye).reshape(nb * n, nb * n)


def _s5_discretize(lam_re, lam_im, log_dt, b_re, b_im):
    lr = jnp.minimum(lam_re, -1e-4)
    li = lam_im
    dt = jnp.exp(log_dt)[:, None]
    mag = jnp.exp(lr * dt)
    ab_re = mag * jnp.cos(li * dt)
    ab_im = mag * jnp.sin(li * dt)
    den = lr * lr + li * li
    nr = ab_re - 1.0
    zr = (nr * lr + ab_im * li) / den
    zi = (ab_im * lr - nr * li) / den
    bb_re = zr[..., None] * b_re - zi[..., None] * b_im
    bb_im = zr[..., None] * b_im + zi[..., None] * b_re
    return ab_re, ab_im, bb_re, bb_im


def _s5_scan_tables(a_re, a_im, bb_re, bb_im):
    pows = [(jnp.ones_like(a_re), jnp.zeros_like(a_im))]
    for _ in range(8):
        qr, qi = pows[-1]
        pows.append((qr * a_re - qi * a_im, qr * a_im + qi * a_re))
    pow_re = jnp.stack([q[0].reshape(N_STATE) for q in pows[1:]])
    pow_im = jnp.stack([q[1].reshape(N_STATE) for q in pows[1:]])
    wl_re, wl_im = [], []
    for qr, qi in pows[:8]:
        wl_re.append(_s5_in_blocks(bb_re * qr[..., None] - bb_im * qi[..., None]))
        wl_im.append(_s5_in_blocks(bb_re * qi[..., None] + bb_im * qr[..., None]))
    return dict(pow_re=pow_re, pow_im=pow_im, wl_re=jnp.concatenate(wl_re, axis=1),
                wl_im=jnp.concatenate(wl_im, axis=1))


def _s5_in_blocks(bb):
    eye = jnp.eye(GROUPS_PER_BLOCK, dtype=bb.dtype)
    b4 = bb.transpose(0, 2, 1).reshape(N_SSM_BLOCKS, GROUPS_PER_BLOCK, S5_GROUP, S5_STATE)
    return jnp.einsum('jgmp,gh->jgmhp', b4, eye).reshape(N_SSM_BLOCKS, LANES, STATE_BLOCK)


def _s5_out_blocks(c):
    eye = jnp.eye(GROUPS_PER_BLOCK, dtype=c.dtype)
    c4 = c.reshape(N_SSM_BLOCKS, GROUPS_PER_BLOCK, S5_GROUP, S5_STATE)
    return jnp.einsum('jgmp,gh->jgphm', c4, eye).reshape(N_SSM_BLOCKS, STATE_BLOCK, LANES)


def _aug_selectors():
    rows = jnp.arange(LANES)[:, None]
    cols = jnp.arange(N_HEADS * LANES)[None, :]
    head = cols // LANES
    c = cols % LANES - HEAD_DIM
    piece = rows // N_HEADS
    rhead = rows % N_HEADS
    same = (rhead == head) & (rows < 4 * N_HEADS)
    f_piece = same & (piece < 3)
    ones_piece = same & (piece == 3)
    selq = jnp.where(f_piece & (c == piece), 1.0, 0.0) + jnp.where(
        ones_piece & (c >= 3) & (c < AUG_COLS), 1.0, 0.0)
    selk = jnp.where(ones_piece & (c >= 0) & (c < 3), 1.0, 0.0) - jnp.where(
        f_piece & (c == piece + 3), 1.0, 0.0)
    return selq.astype(BF16), selk.astype(BF16)


def _forget_cols(w_in1, b_fgt, dtype):
    wf = w_in1[:, 3 * D_MODEL:]
    pad = jnp.zeros((D_MODEL, LANES - 3 * N_HEADS), w_in1.dtype)
    wf3 = jnp.concatenate([wf, wf, wf, pad], axis=1).astype(dtype)
    bf3 = jnp.concatenate([b_fgt, b_fgt, b_fgt, jnp.zeros((LANES - 3 * N_HEADS,), F32)])[None, :]
    return wf3, bf3


def _row(v):
    return v.reshape(1, -1)


def _pick(t, pref):
    return pref if t % pref == 0 else t


def _forward(x, xs, rg_conv, rg_h, s5_re, s5_im, cache_k, cache_v, cache_logf, ffn_conv,
             page_table, p, s5p, tiles, pages_per_step):
    t = x.shape[0]
    n = xs.shape[0]
    bf = lambda w: w.astype(BF16)
    ln = lambda a, layer, which: a[layer, which][None]

    tb = _pick(t, tiles['scan'])
    ya, tail, h_last = _rglru_prompt(
        x, bf(p['w_in0'][:, :2 * D_RNN]), p['rg_conv_w'], _row(p['rg_conv_b']),
        bf(s5p['wa_bd']), _row(p['rg_ba']), bf(s5p['wx_bd']), _row(p['rg_bx']), s5p['c_lam'], tb)
    yb, s_re, s_im = _s5_prompt(
        x, bf(p['w_in0'][:, 2 * D_RNN:]), bf(s5p['wl_re']), bf(s5p['wl_im']), s5p['pow_re'],
        s5p['pow_im'],
        bf(s5p['cre']), bf(s5p['cim']), _row(p['s5_d']), bf(p['s5_w_glu']), _row(p['s5_b_glu']), tb)
    tm = _pick(t, tiles['rows'])
    x2, ffn_tail0 = _ffn_prompt(
        [ya, yb], [bf(p['w_out0'][:D_RNN]), bf(p['w_out0'][D_RNN:])], x,
        ln(p['ln_g'], 0, 0), ln(p['ln_b'], 0, 0), bf(p['ffn_w_up'][0]), bf(p['ffn_w_gate'][0]),
        p['ffn_conv_w'][0], p['ffn_conv_b'][0][None], bf(p['ffn_w_down'][0]),
        ln(p['ln_g'], 0, 1), ln(p['ln_b'], 0, 1), tm)
    wf3, bf3 = _forget_cols(p['w_in1'], p['b_fgt'], BF16)
    selq, selk = _aug_selectors()
    kt, vt, lft, qa, ka, va = _qkv_prompt(x2, bf(p['w_in1'][:, :3 * D_MODEL]), wf3, bf3, selq, selk,
                                       _pick(t, tiles['qkv']))

    per_token = lambda a: a.reshape(N_HEADS, HEAD_DIM, t).transpose(2, 0, 1)[None]

    ya_s, yb_s, xa_s, h_new, xr, xi = _mix0_sample(
        xs, p['w_in0'], rg_conv[:, 0], rg_conv[:, 1], rg_conv[:, 2], p['rg_conv_w'],
        _row(p['rg_conv_b']), s5p['wa_bd'], _row(p['rg_ba']), s5p['wx_bd'], _row(p['rg_bx']),
        s5p['c_lam'], rg_h, s5_re.reshape(n, N_STATE), s5_im.reshape(n, N_STATE),
        s5p['bbr'], s5p['bbi'], s5p['a_re'], s5p['a_im'], s5p['cre'], s5p['cim'],
        _row(p['s5_d']), p['s5_w_glu'], _row(p['s5_b_glu']))
    xs1 = _out_ln([ya_s, yb_s], [p['w_out0'][:D_RNN], p['w_out0'][D_RNN:]], xs,
                  ln(p['ln_g'], 0, 0), ln(p['ln_b'], 0, 0), n, "out0_sample")
    xs2, up0 = _ffn_sample(xs1, p['ffn_w_up'][0], p['ffn_w_gate'][0], ffn_conv[0, :, 0],
                           ffn_conv[0, :, 1], p['ffn_conv_w'][0], p['ffn_conv_b'][0][None],
                           p['ffn_w_down'][0], ln(p['ln_g'], 0, 1), ln(p['ln_b'], 0, 1))
    wf3_s, bf3_s = _forget_cols(p['w_in1'], p['b_fgt'], F32)
    qkv_s, lf3_s = _qkv_sample(xs2, p['w_in1'], wf3_s, bf3_s)
    q_s = qkv_s[:, :D_MODEL]
    k_s = qkv_s[:, D_MODEL:2 * D_MODEL]
    v_s = qkv_s[:, 2 * D_MODEL:]
    logf_s = lf3_s[:, :N_HEADS]

    o, o_s = _flash(qa, ka, va, _pick(t, tiles['tq']), _pick(t, tiles['tk']), tiles['qc'],
                    page_table, q_s, k_s, v_s, logf_s, cache_k, cache_v, cache_logf,
                    pages_per_step)

    x4, ffn_tail1 = _ffn_prompt(
        [o], [bf(p['w_out1'])], x2, ln(p['ln_g'], 1, 0), ln(p['ln_b'], 1, 0),
        bf(p['ffn_w_up'][1]), bf(p['ffn_w_gate'][1]), p['ffn_conv_w'][1],
        p['ffn_conv_b'][1][None], bf(p['ffn_w_down'][1]), ln(p['ln_g'], 1, 1),
        ln(p['ln_b'], 1, 1), tm)
    xs3 = _out_ln([o_s], [p['w_out1']], xs2, ln(p['ln_g'], 1, 0), ln(p['ln_b'], 1, 0), n,
                  "out1_sample")
    xs4, up1 = _ffn_sample(xs3, p['ffn_w_up'][1], p['ffn_w_gate'][1], ffn_conv[1, :, 0],
                           ffn_conv[1, :, 1], p['ffn_conv_w'][1], p['ffn_conv_b'][1][None],
                           p['ffn_w_down'][1], ln(p['ln_g'], 1, 1), ln(p['ln_b'], 1, 1))

    prompt = (x4[None], tail[None, 5:8], h_last[0:1], s_re[0:1].reshape(1, S5_GROUPS, S5_STATE),
              s_im[0:1].reshape(1, S5_GROUPS, S5_STATE), per_token(kt), per_token(vt),
              lft[:N_HEADS].T[None],
              jnp.stack([ffn_tail0[None, 6:8], ffn_tail1[None, 6:8]], axis=0))
    new_rg_conv = jnp.concatenate([rg_conv[:, 1:], xa_s[:, None]], axis=1)
    new_ffn = jnp.stack([jnp.stack([ffn_conv[0, :, 1], up0], axis=1),
                         jnp.stack([ffn_conv[1, :, 1], up1], axis=1)], axis=0)
    sample = (xs4[:, None], new_rg_conv, h_new, xr.reshape(n, S5_GROUPS, S5_STATE),
              xi.reshape(n, S5_GROUPS, S5_STATE), k_s.reshape(n, 1, N_HEADS, HEAD_DIM),
              v_s.reshape(n, 1, N_HEADS, HEAD_DIM), logf_s[:, None], new_ffn)
    return prompt, sample


def _derived_params(p):
    a_re, a_im, bb_re, bb_im = _s5_discretize(p['s5_lambda_re'], p['s5_lambda_im'], p['s5_log_dt'],
                                              p['s5_b_re'], p['s5_b_im'])
    return dict(
        wa_bd=_block_diag_dense(p['rg_wa']), wx_bd=_block_diag_dense(p['rg_wx']),
        c_lam=_row(-RG_C * jax.nn.softplus(-p['rg_lambda'])),
        a_re=a_re.reshape(1, N_STATE), a_im=a_im.reshape(1, N_STATE),
        **_s5_scan_tables(a_re, a_im, bb_re, bb_im),
        bbr=_s5_in_blocks(bb_re), bbi=_s5_in_blocks(bb_im),
        cre=_s5_out_blocks(p['s5_c_re']), cim=_s5_out_blocks(p['s5_c_im']))


PROMPT_TILES = dict(scan=256, rows=512, qkv=256, tq=1024, tk=1024, qc=256)
PAGES_PER_STEP = 4


def kernel(x_prompt, x_sample, state_rglru_conv, state_rglru_h, state_s5_re, state_s5_im, cache_k, cache_v, cache_logf, state_ffn_conv, page_table, w_in0, rg_conv_w, rg_conv_b, rg_wa, rg_ba, rg_wx, rg_bx, rg_lambda, s5_lambda_re, s5_lambda_im, s5_log_dt, s5_b_re, s5_b_im, s5_c_re, s5_c_im, s5_d, s5_w_glu, s5_b_glu, w_out0, w_in1, b_fgt, w_out1, ffn_w_up, ffn_w_gate, ffn_conv_w, ffn_conv_b, ffn_w_down, ln_g, ln_b):
    p = dict(w_in0=w_in0, rg_conv_w=rg_conv_w, rg_conv_b=rg_conv_b, rg_wa=rg_wa, rg_ba=rg_ba,
             rg_wx=rg_wx, rg_bx=rg_bx, rg_lambda=rg_lambda, s5_lambda_re=s5_lambda_re,
             s5_lambda_im=s5_lambda_im, s5_log_dt=s5_log_dt, s5_b_re=s5_b_re, s5_b_im=s5_b_im,
             s5_c_re=s5_c_re, s5_c_im=s5_c_im, s5_d=s5_d, s5_w_glu=s5_w_glu, s5_b_glu=s5_b_glu,
             w_out0=w_out0, w_in1=w_in1, b_fgt=b_fgt, w_out1=w_out1, ffn_w_up=ffn_w_up,
             ffn_w_gate=ffn_w_gate, ffn_conv_w=ffn_conv_w, ffn_conv_b=ffn_conv_b,
             ffn_w_down=ffn_w_down, ln_g=ln_g, ln_b=ln_b)
    s5p = _derived_params(p)
    prompt, sample = _forward(x_prompt[0], x_sample[:, 0], state_rglru_conv, state_rglru_h,
                              state_s5_re, state_s5_im, cache_k, cache_v, cache_logf,
                              state_ffn_conv, page_table, p, s5p, PROMPT_TILES, PAGES_PER_STEP)
    return (prompt[0], sample[0]) + tuple(prompt[1:]) + tuple(sample[1:])
```

```python
import functools

import jax
import jax.numpy as jnp
from jax import lax
from jax.experimental import pallas as pl
from jax.experimental.pallas import tpu as pltpu

F32 = jnp.float32
BF16 = jnp.bfloat16

D_MODEL = 1024
DEPTH = 2
PAGE_SIZE = 128
D_RNN = 512
RG_BLOCKS = 8
RG_BW = D_RNN // RG_BLOCKS
RG_C = 8.0
D_SSM = 512
S5_GROUP = 16
S5_GROUPS = D_SSM // S5_GROUP
S5_STATE = 64
N_STATE = S5_GROUPS * S5_STATE
N_HEADS = 16
HEAD_DIM = D_MODEL // N_HEADS
D_FF = 2816
ALPHA = (2.0 * DEPTH) ** 0.25
LN_EPS = 1e-5

LANES = 128
GROUPS_PER_BLOCK = LANES // S5_GROUP
N_SSM_BLOCKS = D_SSM // LANES
STATE_BLOCK = GROUPS_PER_BLOCK * S5_STATE
FF_CHUNK = 256
N_FF_CHUNKS = D_FF // FF_CHUNK
NEG_BIG = -1e30
LOG2E = 1.4426950408889634
AUG_COLS = 6
VMEM_LIMIT = 56 * 1024 * 1024


def _cparams(n_axes=1, vmem=VMEM_LIMIT):
    return pltpu.CompilerParams(dimension_semantics=("arbitrary",) * n_axes,
                                vmem_limit_bytes=vmem)


def _split3(x):
    h1 = x.astype(BF16)
    r1 = x - h1.astype(F32)
    h2 = r1.astype(BF16)
    h3 = (r1 - h2.astype(F32)).astype(BF16)
    return h1, h2, h3


def _dotw(x, w):
    if w.dtype == BF16:
        return jnp.dot(x.astype(BF16), w, preferred_element_type=F32)
    xh = x.astype(BF16)
    xl = (x - xh.astype(F32)).astype(BF16)
    wh = w.astype(BF16)
    wl = (w - wh.astype(F32)).astype(BF16)
    return (jnp.dot(xh, wh, preferred_element_type=F32)
            + jnp.dot(xl, wh, preferred_element_type=F32)
            + jnp.dot(xh, wl, preferred_element_type=F32))


def _dot_exact01(m01, x):
    h1, h2, h3 = _split3(x)
    return (jnp.dot(m01, h1, preferred_element_type=F32)
            + jnp.dot(m01, h2, preferred_element_type=F32)
            + jnp.dot(m01, h3, preferred_element_type=F32))


def _gelu(x):
    return jax.nn.gelu(x)


def _sigmoid(x):
    return jax.nn.sigmoid(x)


def _log_sigmoid(x):
    return jnp.minimum(x, 0.0) - jnp.log1p(jnp.exp(-jnp.abs(x)))


def _expm1(x):
    u = jnp.exp(x)
    um1 = u - 1.0
    return jnp.where(um1 == 0.0, x, jnp.where(um1 == -1.0, -1.0, um1 * x / jnp.log(u)))


def _layer_norm(xf, g, b):
    mean = jnp.mean(xf, axis=-1, keepdims=True)
    xc = xf - mean
    var = jnp.mean(xc * xc, axis=-1, keepdims=True)
    return xc * lax.rsqrt(var + LN_EPS) * g + b


def _rglru_gates(xc, wa, ba, wx, bx, c_lam):
    r = _sigmoid(_dotw(xc, wa) + ba)
    i = _sigmoid(_dotw(xc, wx) + bx)
    log_a = c_lam * r
    a = jnp.exp(log_a)
    inp = jnp.sqrt(-_expm1(2.0 * log_a)) * (i * xc)
    return a, inp


def _scan_rows(a, b):
    n = a.shape[0]
    row = lax.broadcasted_iota(jnp.int32, a.shape, 0)
    d = 1
    while d < n:
        valid = row >= d
        b = b + jnp.where(valid, a * pltpu.roll(b, d, 0), 0.0)
        if 2 * d < n:
            a = jnp.where(valid, a * pltpu.roll(a, d, 0), a)
        d *= 2
    return b


def _lagged_inputs(u):
    sub = lax.broadcasted_iota(jnp.int32, u.shape, 0) % 8
    lags = [u]
    for k in range(1, 8):
        lags.append(jnp.where(sub >= k, pltpu.roll(u, k, 0), 0.0))
    return jnp.concatenate(lags, axis=1)


def _chain_groups(pow_re, pow_im, br, bi, pr, pi):
    n, c = br.shape
    g = n // 8
    b3r = br.reshape(g, 8, c)
    b3i = bi.reshape(g, 8, c)
    outs_r, outs_i = [], []
    for gi in range(g):
        xr = b3r[gi] + (pow_re * pr - pow_im * pi)
        xi = b3i[gi] + (pow_re * pi + pow_im * pr)
        pr, pi = xr[7:8, :], xi[7:8, :]
        outs_r.append(xr)
        outs_i.append(xi)
    return jnp.concatenate(outs_r, axis=0), jnp.concatenate(outs_i, axis=0)


def _rglru_prompt_kernel(x_ref, w_ref, cw_ref, cb_ref, wa_ref, ba_ref, wx_ref, bx_ref, cl_ref,
                         ya_ref, tail_ref, h_ref, ext_scr, h_scr, *, tb):
    step = pl.program_id(0)

    @pl.when(step == 0)
    def _():
        ext_scr[0:8, :] = jnp.zeros((8, D_RNN), F32)
        h_scr[...] = jnp.zeros_like(h_scr)

    proj = _dotw(x_ref[...], w_ref[...])
    xa = proj[:, :D_RNN]
    ga = proj[:, D_RNN:]
    ext_scr[8:8 + tb, :] = xa
    xc = cb_ref[...] + xa * cw_ref[3:4, :]
    for j in range(3):
        xc = xc + ext_scr[pl.ds(5 + j, tb), :] * cw_ref[j:j + 1, :]
    tail = ext_scr[tb:tb + 8, :]
    ext_scr[0:8, :] = tail
    tail_ref[...] = tail

    a, inp = _rglru_gates(xc, wa_ref[...], ba_ref[...], wx_ref[...], bx_ref[...], cl_ref[...])
    row = lax.broadcasted_iota(jnp.int32, a.shape, 0)
    inp = inp + jnp.where(row == 0, a * h_scr[0:1, :], 0.0)
    h = _scan_rows(a, inp)
    h_last = jnp.broadcast_to(h[tb - 1:tb, :], (8, D_RNN))
    h_scr[...] = h_last
    h_ref[...] = h_last
    ya_ref[...] = h * _gelu(ga)


def _rglru_prompt(x, w_ag, cw, cb, wa, ba, wx, bx, c_lam, tb):
    t = x.shape[0]
    full = lambda shape: pl.BlockSpec(shape, lambda i: (0,) * len(shape))
    return pl.pallas_call(
        functools.partial(_rglru_prompt_kernel, tb=tb),
        grid=(t // tb,),
        in_specs=[pl.BlockSpec((tb, D_MODEL), lambda i: (i, 0)),
                  full(w_ag.shape), full(cw.shape), full(cb.shape), full(wa.shape), full(ba.shape),
                  full(wx.shape), full(bx.shape), full(c_lam.shape)],
        out_specs=[pl.BlockSpec((tb, D_RNN), lambda i: (i, 0)),
                   pl.BlockSpec((8, D_RNN), lambda i: (0, 0)),
                   pl.BlockSpec((8, D_RNN), lambda i: (0, 0))],
        out_shape=[jax.ShapeDtypeStruct((t, D_RNN), F32),
                   jax.ShapeDtypeStruct((8, D_RNN), F32),
                   jax.ShapeDtypeStruct((8, D_RNN), F32)],
        scratch_shapes=[pltpu.VMEM((tb + 8, D_RNN), F32), pltpu.VMEM((8, D_RNN), F32)],
        compiler_params=_cparams(),
        name="rglru_prompt",
    )(x, w_ag, cw, cb, wa, ba, wx, bx, c_lam)


def _s5_readout(u, xr_blocks, xi_blocks, cre_ref, cim_ref, d, wglu, bglu):
    ys = []
    for j in range(N_SSM_BLOCKS):
        ys.append(_dotw(xr_blocks[j], cre_ref[j]) - _dotw(xi_blocks[j], cim_ref[j]))
    y = jnp.concatenate(ys, axis=1) + d * u
    g = _gelu(y)
    return g * _sigmoid(_dotw(g, wglu) + bglu)


def _s5_prompt_kernel(x_ref, w_ref, wlr_ref, wli_ref, pwr_ref, pwi_ref, cre_ref, cim_ref, d_ref,
                      wglu_ref, bglu_ref, yb_ref, sre_ref, sim_ref, sr_scr, si_scr, *, tb):
    step = pl.program_id(0)

    @pl.when(step == 0)
    def _():
        sr_scr[...] = jnp.zeros_like(sr_scr)
        si_scr[...] = jnp.zeros_like(si_scr)

    u = _dotw(x_ref[...], w_ref[...])
    xr_blocks, xi_blocks = [], []
    for j in range(N_SSM_BLOCKS):
        cols = slice(j * STATE_BLOCK, (j + 1) * STATE_BLOCK)
        lagged = _lagged_inputs(u[:, j * LANES:(j + 1) * LANES]).astype(BF16)
        br = jnp.dot(lagged, wlr_ref[j], preferred_element_type=F32)
        bi = jnp.dot(lagged, wli_ref[j], preferred_element_type=F32)
        xr, xi = _chain_groups(pwr_ref[:, cols], pwi_ref[:, cols], br, bi,
                               sr_scr[0:1, cols], si_scr[0:1, cols])
        sr_scr[:, cols] = jnp.broadcast_to(xr[tb - 1:tb, :], (8, STATE_BLOCK))
        si_scr[:, cols] = jnp.broadcast_to(xi[tb - 1:tb, :], (8, STATE_BLOCK))
        xr_blocks.append(xr)
        xi_blocks.append(xi)
    sre_ref[...] = sr_scr[...]
    sim_ref[...] = si_scr[...]
    yb_ref[...] = _s5_readout(u, xr_blocks, xi_blocks, cre_ref, cim_ref, d_ref[...],
                              wglu_ref[...], bglu_ref[...])


def _s5_prompt(x, w_b, wl_re, wl_im, pow_re, pow_im, cre, cim, d, wglu, bglu, tb):
    t = x.shape[0]
    full = lambda a: pl.BlockSpec(a.shape, lambda i: (0,) * a.ndim)
    consts = [w_b, wl_re, wl_im, pow_re, pow_im, cre, cim, d, wglu, bglu]
    return pl.pallas_call(
        functools.partial(_s5_prompt_kernel, tb=tb),
        grid=(t // tb,),
        in_specs=[pl.BlockSpec((tb, D_MODEL), lambda i: (i, 0))] + [full(c) for c in consts],
        out_specs=[pl.BlockSpec((tb, D_SSM), lambda i: (i, 0)),
                   pl.BlockSpec((8, N_STATE), lambda i: (0, 0)),
                   pl.BlockSpec((8, N_STATE), lambda i: (0, 0))],
        out_shape=[jax.ShapeDtypeStruct((t, D_SSM), F32),
                   jax.ShapeDtypeStruct((8, N_STATE), F32),
                   jax.ShapeDtypeStruct((8, N_STATE), F32)],
        scratch_shapes=[pltpu.VMEM((8, N_STATE), F32), pltpu.VMEM((8, N_STATE), F32)],
        compiler_params=_cparams(),
        name="s5_prompt",
    )(x, *consts)


def _out_ln_kernel(*refs, n_in):
    a_refs = refs[:n_in]
    w_refs = refs[n_in:2 * n_in]
    x_ref, g_ref, b_ref, o_ref = refs[2 * n_in:]
    y = ALPHA * x_ref[...]
    for a_ref, w_ref in zip(a_refs, w_refs):
        y = y + _dotw(a_ref[...], w_ref[...])
    o_ref[...] = _layer_norm(y, g_ref[...], b_ref[...])


def _out_ln(acts, weights, x, g, b, tm, name):
    m = x.shape[0]
    n_in = len(acts)
    full = lambda shape: pl.BlockSpec(shape, lambda i: (0,) * len(shape))
    in_specs = ([pl.BlockSpec((tm, a.shape[1]), lambda i: (i, 0)) for a in acts]
                + [full(w.shape) for w in weights]
                + [pl.BlockSpec((tm, D_MODEL), lambda i: (i, 0)), full(g.shape), full(b.shape)])
    return pl.pallas_call(
        functools.partial(_out_ln_kernel, n_in=n_in),
        grid=(m // tm,),
        in_specs=in_specs,
        out_specs=pl.BlockSpec((tm, D_MODEL), lambda i: (i, 0)),
        out_shape=jax.ShapeDtypeStruct((m, D_MODEL), F32),
        compiler_params=_cparams(),
        name=name,
    )(*acts, *weights, x, g, b)


def _ffn_prompt_kernel(*refs, n_in, tm):
    a_refs = refs[:n_in]
    w_refs = refs[n_in:2 * n_in]
    (x_ref, g0_ref, b0_ref, wup_ref, wgate_ref, cw_ref, cb_ref, wdown_ref, g_ref, b_ref,
     o_ref, tail_ref, h_scr, carry_scr) = refs[2 * n_in:]
    step = pl.program_id(0)

    @pl.when(step == 0)
    def _():
        carry_scr[...] = jnp.zeros_like(carry_scr)

    y = ALPHA * x_ref[...]
    for a_ref, w_ref in zip(a_refs, w_refs):
        y = y + _dotw(a_ref[...], w_ref[...])
    x = _layer_norm(y, g0_ref[...], b0_ref[...])
    xb = x.astype(BF16)
    row = lax.broadcasted_iota(jnp.int32, (tm, FF_CHUNK), 0)
    for c in range(N_FF_CHUNKS):
        cols = slice(c * FF_CHUNK, (c + 1) * FF_CHUNK)
        up = jnp.dot(xb, wup_ref[:, cols], preferred_element_type=F32)
        gate = jnp.dot(xb, wgate_ref[:, cols], preferred_element_type=F32)
        prev1 = carry_scr[7:8, cols]
        prev2 = carry_scr[6:7, cols]
        up1 = jnp.where(row >= 1, pltpu.roll(up, 1, 0), prev1)
        up2 = jnp.where(row >= 2, pltpu.roll(up, 2, 0), jnp.where(row == 1, prev1, prev2))
        hc = (cb_ref[:, cols] + up2 * cw_ref[0:1, cols] + up1 * cw_ref[1:2, cols]
              + up * cw_ref[2:3, cols])
        h_scr[:, cols] = (_gelu(hc) * gate).astype(BF16)
        carry_scr[:, cols] = up[tm - 8:tm, :]
    tail_ref[...] = carry_scr[...]
    f = jnp.dot(h_scr[...], wdown_ref[...], preferred_element_type=F32)
    o_ref[...] = _layer_norm(ALPHA * x + f, g_ref[...], b_ref[...])


def _ffn_prompt(acts, weights, x, g0, b0, wup, wgate, cw, cb, wdown, g, b, tm):
    t = x.shape[0]
    n_in = len(acts)
    const = lambda a: pl.BlockSpec(a.shape, lambda i: (0,) * a.ndim,
                                   pipeline_mode=pl.Buffered(1))
    rows = lambda a: pl.BlockSpec((tm, a.shape[1]), lambda i: (i, 0))
    consts = [g0, b0, wup, wgate, cw, cb, wdown, g, b]
    return pl.pallas_call(
        functools.partial(_ffn_prompt_kernel, n_in=n_in, tm=tm),
        grid=(t // tm,),
        in_specs=([rows(a) for a in acts] + [const(w) for w in weights] + [rows(x)]
                  + [const(c) for c in consts]),
        out_specs=[pl.BlockSpec((tm, D_MODEL), lambda i: (i, 0)),
                   pl.BlockSpec((8, D_FF), lambda i: (0, 0))],
        out_shape=[jax.ShapeDtypeStruct((t, D_MODEL), F32),
                   jax.ShapeDtypeStruct((8, D_FF), F32)],
        scratch_shapes=[pltpu.VMEM((tm, D_FF), BF16), pltpu.VMEM((8, D_FF), F32)],
        compiler_params=_cparams(),
        name="ffn_prompt",
    )(*acts, *weights, x, *consts)


def _aug_lhs(f3):
    h1, h2, h3 = (h.astype(F32) for h in _split3(f3))
    lane = lax.broadcasted_iota(jnp.int32, f3.shape, 1)
    pieces = jnp.where(lane < N_HEADS, h1, jnp.where(lane < 2 * N_HEADS, h2, jnp.where(
        lane < 3 * N_HEADS, h3, jnp.where(lane < 4 * N_HEADS, 1.0, 0.0))))
    return pieces.astype(BF16)


def _qkv_prompt_kernel(x_ref, w_ref, wf_ref, bf_ref, selq_ref, selk_ref,
                       kt_ref, vt_ref, lft_ref, qa_ref, ka_ref, va_ref, f_scr, *, tm):
    step = pl.program_id(0)

    @pl.when(step == 0)
    def _():
        f_scr[...] = jnp.zeros_like(f_scr)

    x = x_ref[...]
    qkv = _dotw(x, w_ref[...])
    k = qkv[:, D_MODEL:2 * D_MODEL]
    v = qkv[:, 2 * D_MODEL:]
    logf3 = _log_sigmoid(_dotw(x, wf_ref[...]) + bf_ref[...])
    lft_ref[...] = logf3.T

    r_i = lax.broadcasted_iota(jnp.int32, (tm, tm), 0)
    c_i = lax.broadcasted_iota(jnp.int32, (tm, tm), 1)
    tri = jnp.where(r_i >= c_i, 1.0, 0.0).astype(BF16)
    f3 = _dot_exact01(tri, logf3) + f_scr[0:1, :]
    f_scr[...] = jnp.broadcast_to(f3[tm - 1:tm, :], f_scr.shape)

    lhs = _aug_lhs(f3 * LOG2E)
    augq = jnp.dot(lhs, selq_ref[...], preferred_element_type=F32)
    augk = jnp.dot(lhs, selk_ref[...], preferred_element_type=F32)
    lane = lax.broadcasted_iota(jnp.int32, (tm, LANES), 1)
    low = lane < HEAD_DIM
    vtail = jnp.where(lane == HEAD_DIM, 1.0, 0.0)
    scale = HEAD_DIM ** -0.5 * LOG2E
    for hp in range(N_HEADS // 2):
        cols = slice(hp * LANES, (hp + 1) * LANES)
        qb = qkv[:, cols] * scale
        kb = k[:, cols]
        vb = v[:, cols]
        kt_ref[cols, :] = kb.T
        vt_ref[cols, :] = vb.T
        for odd in range(2):
            h = 2 * hp + odd
            acols = slice(h * LANES, (h + 1) * LANES)
            if odd:
                qb, kb, vb = (pltpu.roll(qb, HEAD_DIM, 1), pltpu.roll(kb, HEAD_DIM, 1),
                              pltpu.roll(vb, HEAD_DIM, 1))
            qa_ref[h] = jnp.where(low, qb, augq[:, acols]).astype(BF16)
            ka_ref[h] = jnp.where(low, kb, augk[:, acols]).astype(BF16)
            va_ref[h] = jnp.where(low, vb, vtail).astype(BF16)


def _qkv_prompt(x, w_qkv, wf3, bf3, selq, selk, tm):
    t = x.shape[0]
    full = lambda shape: pl.BlockSpec(shape, lambda i: (0,) * len(shape))
    head_major = jax.ShapeDtypeStruct((N_HEADS, t, LANES), BF16)
    head_spec = pl.BlockSpec((N_HEADS, tm, LANES), lambda i: (0, i, 0))
    return pl.pallas_call(
        functools.partial(_qkv_prompt_kernel, tm=tm),
        grid=(t // tm,),
        in_specs=[pl.BlockSpec((tm, D_MODEL), lambda i: (i, 0)),
                  full(w_qkv.shape), full(wf3.shape), full(bf3.shape), full(selq.shape),
                  full(selk.shape)],
        out_specs=[pl.BlockSpec((D_MODEL, tm), lambda i: (0, i)),
                   pl.BlockSpec((D_MODEL, tm), lambda i: (0, i)),
                   pl.BlockSpec((LANES, tm), lambda i: (0, i)),
                   head_spec, head_spec, head_spec],
        out_shape=[jax.ShapeDtypeStruct((D_MODEL, t), F32),
                   jax.ShapeDtypeStruct((D_MODEL, t), F32),
                   jax.ShapeDtypeStruct((LANES, t), F32),
                   head_major, head_major, head_major],
        scratch_shapes=[pltpu.VMEM((8, LANES), F32)],
        compiler_params=_cparams(),
        name="qkv_prompt",
    )(x, w_qkv, wf3, bf3, selq, selk)


def _repeat_rows(x):
    return jnp.broadcast_to(x[:, None, :], (N_HEADS, HEAD_DIM, x.shape[1])).reshape(
        D_MODEL, x.shape[1])


def _head_sum(x):
    return jnp.sum(x.reshape(N_HEADS, HEAD_DIM, x.shape[1]), axis=1)


def _flash_kernel(it_ref, jt_ref, pt_ref, q_ref, k_ref, v_ref, qb_ref, kn_ref, vn_ref, gt_ref,
                  *refs, tq, tk, qc, pp, n_tri, steps_per_seq, n_seq):
    k_pages = refs[:pp]
    v_pages = refs[pp:2 * pp]
    lf_pages = refs[2 * pp:3 * pp]
    o_ref, os_ref, m_scr, acc_scr, sm_scr, sl_scr, sacc_scr, scarry_scr = refs[3 * pp:]
    s_idx = pl.program_id(1)
    i = it_ref[s_idx]
    j = jt_ref[s_idx]
    t = pl.program_id(0) * n_tri + s_idx
    grp = t % steps_per_seq
    seq_active = t < n_seq * steps_per_seq
    first_group = grp == 0

    @pl.when(t == 0)
    def _():
        sm_scr[...] = jnp.zeros_like(sm_scr)
        sl_scr[...] = jnp.zeros_like(sl_scr)
        sacc_scr[...] = jnp.zeros_like(sacc_scr)
        scarry_scr[...] = jnp.zeros_like(scarry_scr)

    @pl.when(j == 0)
    def _():
        m_scr[...] = jnp.full(m_scr.shape, NEG_BIG, F32)
        acc_scr[...] = jnp.zeros_like(acc_scr)

    r_i = lax.broadcasted_iota(jnp.int32, (PAGE_SIZE, PAGE_SIZE), 0)
    c_i = lax.broadcasted_iota(jnp.int32, (PAGE_SIZE, PAGE_SIZE), 1)
    later = jnp.where(r_i > c_i, 1.0, 0.0).astype(BF16)

    def sample_pieces():
        st = {}

        def qk(r):
            def run():
                kt = k_pages[r][0].reshape(D_MODEL, PAGE_SIZE)
                st['s%d' % r] = _head_sum(kt * qb_ref[0])
            return run

        def softmax():
            gt = gt_ref[0]
            carry = jnp.where(first_group, 0.0, scarry_scr[:, 0:1])
            logits = []
            for r in range(pp):
                lf = lf_pages[r][0]
                suf = sum(jnp.dot(pc, later, preferred_element_type=F32) for pc in _split3(lf))
                logits.append(st['s%d' % r] + suf + (carry + gt))
                carry = carry + jnp.sum(lf, axis=1, keepdims=True)
            scarry_scr[...] = jnp.broadcast_to(carry, scarry_scr.shape)
            mx = logits[0]
            for lg in logits[1:]:
                mx = jnp.maximum(mx, lg)
            m_prev = jnp.where(first_group, NEG_BIG, sm_scr[:, 0:1])
            l_prev = jnp.where(first_group, 0.0, sl_scr[:, 0:1])
            m_next = jnp.maximum(m_prev, jnp.max(mx, axis=1, keepdims=True))
            alpha = jnp.exp(m_prev - m_next)
            ps = [jnp.exp(lg - m_next) for lg in logits]
            psum = ps[0]
            for pblk in ps[1:]:
                psum = psum + pblk
            sl_scr[...] = jnp.broadcast_to(
                alpha * l_prev + jnp.sum(psum, axis=1, keepdims=True), sl_scr.shape)
            sm_scr[...] = jnp.broadcast_to(m_next, sm_scr.shape)
            st['ps'] = ps
            st['alpha'] = jnp.broadcast_to(alpha, (N_HEADS, PAGE_SIZE))

        def pv(r):
            def run():
                vt = v_pages[r][0].reshape(D_MODEL, PAGE_SIZE)
                if r == 0:
                    st['acc'] = _repeat_rows(st['alpha']) * sacc_scr[...]
                st['acc'] = st['acc'] + _repeat_rows(st['ps'][r]) * vt
                if r == pp - 1:
                    sacc_scr[...] = st['acc']
            return run

        return [qk(r) for r in range(pp)] + [softmax] + [pv(r) for r in range(pp)]

    def accumulate(masked):
        chunks = [(hh, c) for hh in range(2) for c in range(tq // qc)]
        pieces = sample_pieces()

        def n_keys(c):
            return (c + 1) * qc if masked else tk

        def scores(hh, c):
            rows = slice(c * qc, (c + 1) * qc)
            nk = n_keys(c)
            s = lax.dot_general(q_ref[hh, rows, :], k_ref[hh, 0:nk, :], (((1,), (1,)), ((), ())),
                                preferred_element_type=F32)
            if masked:
                qpos = c * qc + lax.broadcasted_iota(jnp.int32, (qc, nk), 0)
                kpos = lax.broadcasted_iota(jnp.int32, (qc, nk), 1)
                s = jnp.where(kpos <= qpos, s, NEG_BIG)
            return s

        ahead = 2
        pending = [scores(*chunks[n]) for n in range(ahead)]
        per_chunk = -(-len(pieces) // len(chunks))
        for n, (hh, c) in enumerate(chunks):
            s = pending.pop(0)
            if n + ahead < len(chunks):
                pending.append(scores(*chunks[n + ahead]))
            for piece in pieces[n * per_chunk:(n + 1) * per_chunk]:
                piece()
            rows = slice(c * qc, (c + 1) * qc)
            m_prev = m_scr[hh, rows, :]
            m_next = jnp.maximum(m_prev, jnp.max(s, axis=1, keepdims=True))
            alpha = jnp.exp2(m_prev - m_next)
            p = jnp.exp2(s - m_next[:, 0:1]).astype(BF16)
            acc_scr[hh, rows, :] = alpha * acc_scr[hh, rows, :] + jnp.dot(
                p, v_ref[hh, 0:n_keys(c), :], preferred_element_type=F32)
            m_scr[hh, rows, :] = m_next

    assert tq == tk, "the diagonal-block key trimming assumes square blocks"
    crosses_diagonal = j == i

    @pl.when(crosses_diagonal)
    def _():
        accumulate(True)

    @pl.when(jnp.logical_not(crosses_diagonal))
    def _():
        accumulate(False)

    @pl.when(j == ((i + 1) * tq - 1) // tk)
    def _():
        lane = lax.broadcasted_iota(jnp.int32, (tq, LANES), 1)
        outs = []
        for hh in range(2):
            acc = acc_scr[hh]
            outs.append(acc / acc[:, HEAD_DIM:HEAD_DIM + 1])
        o_ref[...] = jnp.where(lane < HEAD_DIM, outs[0], pltpu.roll(outs[1], HEAD_DIM, 1))

    @pl.when(jnp.logical_and(seq_active, grp == steps_per_seq - 1))
    def _():
        qcol = qb_ref[0][:, 0:1]
        s_self = _head_sum(qcol * kn_ref[0])
        m_prev = sm_scr[:, 0:1]
        m_fin = jnp.maximum(m_prev, s_self)
        a_past = jnp.exp(m_prev - m_fin)
        a_self = jnp.exp(s_self - m_fin)
        l_fin = a_past * sl_scr[:, 0:1] + a_self
        past = jnp.sum(sacc_scr[...], axis=1, keepdims=True)
        os_ref[0] = (_repeat_rows(a_past) * past + _repeat_rows(a_self) * vn_ref[0]) / _repeat_rows(
            l_fin)


def _flash(qa, ka, va, tq, tk, qc, page_table, q_s, k_s, v_s, logf_s, cache_k, cache_v,
           cache_logf, pp):
    t = ka.shape[1]
    nq = t // tq
    i_list, j_list = [], []
    for i in range(nq):
        for j in range(((i + 1) * tq - 1) // tk + 1):
            i_list.append(i)
            j_list.append(j)
    n_tri = len(i_list)
    it = jnp.asarray(i_list, jnp.int32)
    jt = jnp.asarray(j_list, jnp.int32)

    n_seq, n_pages = page_table.shape
    steps_per_seq = n_pages // pp
    n_steps = (N_HEADS // 2) * n_tri
    assert n_pages % pp == 0 and n_seq * steps_per_seq <= n_steps
    pt = page_table.reshape(-1)
    ck = jnp.transpose(cache_k, (0, 2, 3, 1))
    cv = jnp.transpose(cache_v, (0, 2, 3, 1))
    clf = jnp.transpose(cache_logf, (0, 2, 1))
    scale = HEAD_DIM ** -0.5
    qb = jnp.broadcast_to((q_s * scale)[:, :, None], (n_seq, D_MODEL, PAGE_SIZE))
    col = lambda a: a.reshape(n_seq, D_MODEL, 1)

    def seq_of(hp, s):
        return jnp.minimum((hp * n_tri + s) // steps_per_seq, n_seq - 1)

    def page_map(r, ndim):
        def index(hp, s, it, jt, pt):
            step = hp * n_tri + s
            grp = jnp.where(step < n_seq * steps_per_seq, step % steps_per_seq,
                            steps_per_seq - 1)
            page = pt[seq_of(hp, s) * n_pages + (n_pages - 1 - (grp * pp + r))]
            return (page,) + (0,) * ndim
        return index

    per_seq = lambda shape: pl.BlockSpec((1,) + shape,
                                         lambda hp, s, it, jt, pt: (seq_of(hp, s), 0, 0))
    kv_block = (1, N_HEADS, HEAD_DIM, PAGE_SIZE)
    in_specs = [pl.BlockSpec((2, tq, LANES), lambda hp, s, it, jt, pt: (hp, it[s], 0)),
                pl.BlockSpec((2, tk, LANES), lambda hp, s, it, jt, pt: (hp, jt[s], 0)),
                pl.BlockSpec((2, tk, LANES), lambda hp, s, it, jt, pt: (hp, jt[s], 0)),
                per_seq((D_MODEL, PAGE_SIZE)), per_seq((D_MODEL, 1)), per_seq((D_MODEL, 1)),
                per_seq((N_HEADS, 1))]
    in_specs += [pl.BlockSpec(kv_block, page_map(r, 3)) for r in range(pp)]
    in_specs += [pl.BlockSpec(kv_block, page_map(r, 3)) for r in range(pp)]
    in_specs += [pl.BlockSpec((1, N_HEADS, PAGE_SIZE), page_map(r, 2)) for r in range(pp)]
    grid_spec = pltpu.PrefetchScalarGridSpec(
        num_scalar_prefetch=3,
        grid=(N_HEADS // 2, n_tri),
        in_specs=in_specs,
        out_specs=[pl.BlockSpec((tq, LANES), lambda hp, s, it, jt, pt: (it[s], hp)),
                   per_seq((D_MODEL, 1))],
        scratch_shapes=[pltpu.VMEM((2, tq, LANES), F32), pltpu.VMEM((2, tq, LANES), F32),
                        pltpu.VMEM((N_HEADS, LANES), F32), pltpu.VMEM((N_HEADS, LANES), F32),
                        pltpu.VMEM((D_MODEL, PAGE_SIZE), F32), pltpu.VMEM((N_HEADS, LANES), F32)],
    )
    o, o_s = pl.pallas_call(
        functools.partial(_flash_kernel, tq=tq, tk=tk, qc=qc, pp=pp, n_tri=n_tri,
                          steps_per_seq=steps_per_seq, n_seq=n_seq),
        grid_spec=grid_spec,
        out_shape=[jax.ShapeDtypeStruct((t, D_MODEL), F32),
                   jax.ShapeDtypeStruct((n_seq, D_MODEL, 1), F32)],
        compiler_params=_cparams(2),
        name="flash_prompt",
    )(it, jt, pt, qa, ka, va, qb, col(k_s), col(v_s), logf_s.reshape(n_seq, N_HEADS, 1),
      *([ck] * pp), *([cv] * pp), *([clf] * pp))
    return o, o_s.reshape(n_seq, D_MODEL)


def _mix0_sample_kernel(x_ref, w_ref, c0_ref, c1_ref, c2_ref, cw_ref, cb_ref, wa_ref, ba_ref,
                        wx_ref, bx_ref, cl_ref, h0_ref, sr_ref, si_ref, bbr_ref, bbi_ref,
                        ar_ref, ai_ref, cre_ref, cim_ref, d_ref, wglu_ref, bglu_ref,
                        ya_ref, yb_ref, xa_ref, h_ref, xr_ref, xi_ref):
    proj = _dotw(x_ref[...], w_ref[...])
    xa = proj[:, :D_RNN]
    ga = proj[:, D_RNN:2 * D_RNN]
    u = proj[:, 2 * D_RNN:]
    xa_ref[...] = xa
    xc = (cb_ref[...] + c0_ref[...] * cw_ref[0:1, :] + c1_ref[...] * cw_ref[1:2, :]
          + c2_ref[...] * cw_ref[2:3, :] + xa * cw_ref[3:4, :])
    a, inp = _rglru_gates(xc, wa_ref[...], ba_ref[...], wx_ref[...], bx_ref[...], cl_ref[...])
    h = a * h0_ref[...] + inp
    h_ref[...] = h
    ya_ref[...] = h * _gelu(ga)

    xr_blocks, xi_blocks = [], []
    for j in range(N_SSM_BLOCKS):
        cols = slice(j * STATE_BLOCK, (j + 1) * STATE_BLOCK)
        uj = u[:, j * LANES:(j + 1) * LANES]
        ar = ar_ref[:, cols]
        ai = ai_ref[:, cols]
        pr = sr_ref[:, cols]
        pi = si_ref[:, cols]
        xr = _dotw(uj, bbr_ref[j]) + (ar * pr - ai * pi)
        xi = _dotw(uj, bbi_ref[j]) + (ar * pi + ai * pr)
        xr_ref[:, cols] = xr
        xi_ref[:, cols] = xi
        xr_blocks.append(xr)
        xi_blocks.append(xi)
    yb_ref[...] = _s5_readout(u, xr_blocks, xi_blocks, cre_ref, cim_ref, d_ref[...],
                              wglu_ref[...], bglu_ref[...])


def _mix0_sample(*args):
    n = args[0].shape[0]
    shapes = [(n, D_RNN), (n, D_SSM), (n, D_RNN), (n, D_RNN), (n, N_STATE), (n, N_STATE)]
    return pl.pallas_call(
        _mix0_sample_kernel,
        out_shape=[jax.ShapeDtypeStruct(s, F32) for s in shapes],
        compiler_params=pltpu.CompilerParams(vmem_limit_bytes=VMEM_LIMIT),
        name="mix0_sample",
    )(*args)


def _ffn_sample_kernel(x_ref, wup_ref, wgate_ref, b0_ref, b1_ref, cw_ref, cb_ref, wdown_ref,
                       g_ref, b_ref, o_ref, up_ref, acc_scr):
    c = pl.program_id(0)

    @pl.when(c == 0)
    def _():
        acc_scr[...] = jnp.zeros_like(acc_scr)

    x = x_ref[...]
    up = _dotw(x, wup_ref[...])
    gate = _dotw(x, wgate_ref[...])
    up_ref[...] = up
    hc = (cb_ref[...] + b0_ref[...] * cw_ref[0:1, :] + b1_ref[...] * cw_ref[1:2, :]
          + up * cw_ref[2:3, :])
    acc_scr[...] += _dotw(_gelu(hc) * gate, wdown_ref[...])

    @pl.when(c == pl.num_programs(0) - 1)
    def _():
        o_ref[...] = _layer_norm(ALPHA * x + acc_scr[...], g_ref[...], b_ref[...])


def _ffn_sample(x, wup, wgate, buf0, buf1, cw, cb, wdown, g, b):
    n = x.shape[0]
    fc = FF_CHUNK
    col = lambda rows: pl.BlockSpec((rows, fc), lambda c: (0, c))
    full = lambda shape: pl.BlockSpec(shape, lambda c: (0,) * len(shape))
    return pl.pallas_call(
        _ffn_sample_kernel,
        grid=(D_FF // fc,),
        in_specs=[full(x.shape), col(D_MODEL), col(D_MODEL), col(n), col(n), col(cw.shape[0]),
                  col(1), pl.BlockSpec((fc, D_MODEL), lambda c: (c, 0)), full(g.shape),
                  full(b.shape)],
        out_specs=[full((n, D_MODEL)), col(n)],
        out_shape=[jax.ShapeDtypeStruct((n, D_MODEL), F32), jax.ShapeDtypeStruct((n, D_FF), F32)],
        scratch_shapes=[pltpu.VMEM((n, D_MODEL), F32)],
        compiler_params=_cparams(),
        name="ffn_sample",
    )(x, wup, wgate, buf0, buf1, cw, cb, wdown, g, b)


def _qkv_sample_kernel(x_ref, w_ref, wf_ref, bf_ref, o_ref, lf_ref):
    x = x_ref[...]
    o_ref[...] = _dotw(x, w_ref[...])

    @pl.when(pl.program_id(0) == 0)
    def _():
        lf_ref[...] = _log_sigmoid(_dotw(x, wf_ref[...]) + bf_ref[...])


def _qkv_sample(x, w_in1, wf, bf):
    n = x.shape[0]
    full = lambda shape: pl.BlockSpec(shape, lambda c: (0,) * len(shape))
    return pl.pallas_call(
        _qkv_sample_kernel,
        grid=(3,),
        in_specs=[full(x.shape), pl.BlockSpec((D_MODEL, D_MODEL), lambda c: (0, c)),
                  full(wf.shape), full(bf.shape)],
        out_specs=[pl.BlockSpec((n, D_MODEL), lambda c: (0, c)), full((n, LANES))],
        out_shape=[jax.ShapeDtypeStruct((n, 3 * D_MODEL), F32),
                   jax.ShapeDtypeStruct((n, LANES), F32)],
        compiler_params=_cparams(),
        name="qkv_sample",
    )(x, w_in1, wf, bf)


def _block_diag_dense(w):
    nb, n, _ = w.shape
    eye = jnp.eye(nb, dtype=w.dtype)
    return jnp.einsum('hij,hk->hikj', w, eye).reshape(nb * n, nb * n)


def _s5_discretize(lam_re, lam_im, log_dt, b_re, b_im):
    lr = jnp.minimum(lam_re, -1e-4)
    li = lam_im
    dt = jnp.exp(log_dt)[:, None]
    mag = jnp.exp(lr * dt)
    ab_re = mag * jnp.cos(li * dt)
    ab_im = mag * jnp.sin(li * dt)
    den = lr * lr + li * li
    nr = ab_re - 1.0
    zr = (nr * lr + ab_im * li) / den
    zi = (ab_im * lr - nr * li) / den
    bb_re = zr[..., None] * b_re - zi[..., None] * b_im
    bb_im = zr[..., None] * b_im + zi[..., None] * b_re
    return ab_re, ab_im, bb_re, bb_im


def _s5_scan_tables(a_re, a_im, bb_re, bb_im):
    pows = [(jnp.ones_like(a_re), jnp.zeros_like(a_im))]
    for _ in range(8):
        qr, qi = pows[-1]
        pows.append((qr * a_re - qi * a_im, qr * a_im + qi * a_re))
    p_re = jnp.stack([q[0] for q in pows])
    p_im = jnp.stack([q[1] for q in pows])
    pow_re = p_re[1:].reshape(8, N_STATE)
    pow_im = p_im[1:].reshape(8, N_STATE)
    lag_re = p_re[:8, :, :, None]
    lag_im = p_im[:8, :, :, None]
    return dict(pow_re=pow_re, pow_im=pow_im,
                wl_re=_s5_lag_blocks(bb_re[None] * lag_re - bb_im[None] * lag_im),
                wl_im=_s5_lag_blocks(bb_re[None] * lag_im + bb_im[None] * lag_re))


def _s5_lag_blocks(w):
    eye = jnp.eye(GROUPS_PER_BLOCK, dtype=w.dtype)
    w5 = w.transpose(0, 1, 3, 2).reshape(8, N_SSM_BLOCKS, GROUPS_PER_BLOCK, S5_GROUP, S5_STATE)
    return jnp.einsum('kjgmp,gh->jkgmhp', w5, eye).reshape(N_SSM_BLOCKS, 8 * LANES, STATE_BLOCK)


def _s5_in_blocks(bb):
    eye = jnp.eye(GROUPS_PER_BLOCK, dtype=bb.dtype)
    b4 = bb.transpose(0, 2, 1).reshape(N_SSM_BLOCKS, GROUPS_PER_BLOCK, S5_GROUP, S5_STATE)
    return jnp.einsum('jgmp,gh->jgmhp', b4, eye).reshape(N_SSM_BLOCKS, LANES, STATE_BLOCK)


def _s5_out_blocks(c):
    eye = jnp.eye(GROUPS_PER_BLOCK, dtype=c.dtype)
    c4 = c.reshape(N_SSM_BLOCKS, GROUPS_PER_BLOCK, S5_GROUP, S5_STATE)
    return jnp.einsum('jgmp,gh->jgphm', c4, eye).reshape(N_SSM_BLOCKS, STATE_BLOCK, LANES)


def _aug_selectors():
    rows = jnp.arange(LANES)[:, None]
    cols = jnp.arange(N_HEADS * LANES)[None, :]
    head = cols // LANES
    c = cols % LANES - HEAD_DIM
    piece = rows // N_HEADS
    rhead = rows % N_HEADS
    same = (rhead == head) & (rows < 4 * N_HEADS)
    f_piece = same & (piece < 3)
    ones_piece = same & (piece == 3)
    selq = jnp.where(f_piece & (c == piece), 1.0, 0.0) + jnp.where(
        ones_piece & (c >= 3) & (c < AUG_COLS), 1.0, 0.0)
    selk = jnp.where(ones_piece & (c >= 0) & (c < 3), 1.0, 0.0) - jnp.where(
        f_piece & (c == piece + 3), 1.0, 0.0)
    return selq.astype(BF16), selk.astype(BF16)


def _forget_cols(w_in1, b_fgt, dtype):
    wf = w_in1[:, 3 * D_MODEL:]
    pad = jnp.zeros((D_MODEL, LANES - 3 * N_HEADS), w_in1.dtype)
    wf3 = jnp.concatenate([wf, wf, wf, pad], axis=1).astype(dtype)
    bf3 = jnp.concatenate([b_fgt, b_fgt, b_fgt, jnp.zeros((LANES - 3 * N_HEADS,), F32)])[None, :]
    return wf3, bf3


def _row(v):
    return v.reshape(1, -1)


def _pick(t, pref):
    return pref if t % pref == 0 else t


def _forward(x, xs, rg_conv, rg_h, s5_re, s5_im, cache_k, cache_v, cache_logf, ffn_conv,
             page_table, p, s5p, tiles, pages_per_step):
    t = x.shape[0]
    n = xs.shape[0]
    bf = lambda w: w.astype(BF16)
    ln = lambda a, layer, which: a[layer, which][None]

    tb = _pick(t, tiles['scan'])
    ya, tail, h_last = _rglru_prompt(
        x, bf(p['w_in0'][:, :2 * D_RNN]), p['rg_conv_w'], _row(p['rg_conv_b']),
        bf(s5p['wa_bd']), _row(p['rg_ba']), bf(s5p['wx_bd']), _row(p['rg_bx']), s5p['c_lam'], tb)
    yb, s_re, s_im = _s5_prompt(
        x, bf(p['w_in0'][:, 2 * D_RNN:]), bf(s5p['wl_re']), bf(s5p['wl_im']), s5p['pow_re'],
        s5p['pow_im'],
        bf(s5p['cre']), bf(s5p['cim']), _row(p['s5_d']), bf(p['s5_w_glu']), _row(p['s5_b_glu']), tb)
    tm = _pick(t, tiles['rows'])
    x2, ffn_tail0 = _ffn_prompt(
        [ya, yb], [bf(p['w_out0'][:D_RNN]), bf(p['w_out0'][D_RNN:])], x,
        ln(p['ln_g'], 0, 0), ln(p['ln_b'], 0, 0), bf(p['ffn_w_up'][0]), bf(p['ffn_w_gate'][0]),
        p['ffn_conv_w'][0], p['ffn_conv_b'][0][None], bf(p['ffn_w_down'][0]),
        ln(p['ln_g'], 0, 1), ln(p['ln_b'], 0, 1), tm)
    wf3, bf3 = _forget_cols(p['w_in1'], p['b_fgt'], BF16)
    selq, selk = _aug_selectors()
    kt, vt, lft, qa, ka, va = _qkv_prompt(x2, bf(p['w_in1'][:, :3 * D_MODEL]), wf3, bf3, selq, selk,
                                       _pick(t, tiles['qkv']))

    per_token = lambda a: a.reshape(N_HEADS, HEAD_DIM, t).transpose(2, 0, 1)[None]

    ya_s, yb_s, xa_s, h_new, xr, xi = _mix0_sample(
        xs, p['w_in0'], rg_conv[:, 0], rg_conv[:, 1], rg_conv[:, 2], p['rg_conv_w'],
        _row(p['rg_conv_b']), s5p['wa_bd'], _row(p['rg_ba']), s5p['wx_bd'], _row(p['rg_bx']),
        s5p['c_lam'], rg_h, s5_re.reshape(n, N_STATE), s5_im.reshape(n, N_STATE),
        s5p['bbr'], s5p['bbi'], s5p['a_re'], s5p['a_im'], s5p['cre'], s5p['cim'],
        _row(p['s5_d']), p['s5_w_glu'], _row(p['s5_b_glu']))
    xs1 = _out_ln([ya_s, yb_s], [p['w_out0'][:D_RNN], p['w_out0'][D_RNN:]], xs,
                  ln(p['ln_g'], 0, 0), ln(p['ln_b'], 0, 0), n, "out0_sample")
    xs2, up0 = _ffn_sample(xs1, p['ffn_w_up'][0], p['ffn_w_gate'][0], ffn_conv[0, :, 0],
                           ffn_conv[0, :, 1], p['ffn_conv_w'][0], p['ffn_conv_b'][0][None],
                           p['ffn_w_down'][0], ln(p['ln_g'], 0, 1), ln(p['ln_b'], 0, 1))
    wf3_s, bf3_s = _forget_cols(p['w_in1'], p['b_fgt'], F32)
    qkv_s, lf3_s = _qkv_sample(xs2, p['w_in1'], wf3_s, bf3_s)
    q_s = qkv_s[:, :D_MODEL]
    k_s = qkv_s[:, D_MODEL:2 * D_MODEL]
    v_s = qkv_s[:, 2 * D_MODEL:]
    logf_s = lf3_s[:, :N_HEADS]

    o, o_s = _flash(qa, ka, va, _pick(t, tiles['tq']), _pick(t, tiles['tk']), tiles['qc'],
                    page_table, q_s, k_s, v_s, logf_s, cache_k, cache_v, cache_logf,
                    pages_per_step)

    x4, ffn_tail1 = _ffn_prompt(
        [o], [bf(p['w_out1'])], x2, ln(p['ln_g'], 1, 0), ln(p['ln_b'], 1, 0),
        bf(p['ffn_w_up'][1]), bf(p['ffn_w_gate'][1]), p['ffn_conv_w'][1],
        p['ffn_conv_b'][1][None], bf(p['ffn_w_down'][1]), ln(p['ln_g'], 1, 1),
        ln(p['ln_b'], 1, 1), tm)
    xs3 = _out_ln([o_s], [p['w_out1']], xs2, ln(p['ln_g'], 1, 0), ln(p['ln_b'], 1, 0), n,
                  "out1_sample")
    xs4, up1 = _ffn_sample(xs3, p['ffn_w_up'][1], p['ffn_w_gate'][1], ffn_conv[1, :, 0],
                           ffn_conv[1, :, 1], p['ffn_conv_w'][1], p['ffn_conv_b'][1][None],
                           p['ffn_w_down'][1], ln(p['ln_g'], 1, 1), ln(p['ln_b'], 1, 1))

    prompt = (x4[None], tail[None, 5:8], h_last[0:1], s_re[0:1].reshape(1, S5_GROUPS, S5_STATE),
              s_im[0:1].reshape(1, S5_GROUPS, S5_STATE), per_token(kt), per_token(vt),
              lft[:N_HEADS].T[None],
              jnp.stack([ffn_tail0[None, 6:8], ffn_tail1[None, 6:8]], axis=0))
    new_rg_conv = jnp.concatenate([rg_conv[:, 1:], xa_s[:, None]], axis=1)
    new_ffn = jnp.stack([jnp.stack([ffn_conv[0, :, 1], up0], axis=1),
                         jnp.stack([ffn_conv[1, :, 1], up1], axis=1)], axis=0)
    sample = (xs4[:, None], new_rg_conv, h_new, xr.reshape(n, S5_GROUPS, S5_STATE),
              xi.reshape(n, S5_GROUPS, S5_STATE), k_s.reshape(n, 1, N_HEADS, HEAD_DIM),
              v_s.reshape(n, 1, N_HEADS, HEAD_DIM), logf_s[:, None], new_ffn)
    return prompt, sample


def _derived_params(p):
    a_re, a_im, bb_re, bb_im = _s5_discretize(p['s5_lambda_re'], p['s5_lambda_im'], p['s5_log_dt'],
                                              p['s5_b_re'], p['s5_b_im'])
    return dict(
        wa_bd=_block_diag_dense(p['rg_wa']), wx_bd=_block_diag_dense(p['rg_wx']),
        c_lam=_row(-RG_C * jax.nn.softplus(-p['rg_lambda'])),
        a_re=a_re.reshape(1, N_STATE), a_im=a_im.reshape(1, N_STATE),
        **_s5_scan_tables(a_re, a_im, bb_re, bb_im),
        bbr=_s5_in_blocks(bb_re), bbi=_s5_in_blocks(bb_im),
        cre=_s5_out_blocks(p['s5_c_re']), cim=_s5_out_blocks(p['s5_c_im']))


PROMPT_TILES = dict(scan=256, rows=512, qkv=256, tq=1024, tk=1024, qc=256)
PAGES_PER_STEP = 4


def kernel(x_prompt, x_sample, state_rglru_conv, state_rglru_h, state_s5_re, state_s5_im, cache_k, cache_v, cache_logf, state_ffn_conv, page_table, w_in0, rg_conv_w, rg_conv_b, rg_wa, rg_ba, rg_wx, rg_bx, rg_lambda, s5_lambda_re, s5_lambda_im, s5_log_dt, s5_b_re, s5_b_im, s5_c_re, s5_c_im, s5_d, s5_w_glu, s5_b_glu, w_out0, w_in1, b_fgt, w_out1, ffn_w_up, ffn_w_gate, ffn_conv_w, ffn_conv_b, ffn_w_down, ln_g, ln_b):
    p = dict(w_in0=w_in0, rg_conv_w=rg_conv_w, rg_conv_b=rg_conv_b, rg_wa=rg_wa, rg_ba=rg_ba,
             rg_wx=rg_wx, rg_bx=rg_bx, rg_lambda=rg_lambda, s5_lambda_re=s5_lambda_re,
             s5_lambda_im=s5_lambda_im, s5_log_dt=s5_log_dt, s5_b_re=s5_b_re, s5_b_im=s5_b_im,
             s5_c_re=s5_c_re, s5_c_im=s5_c_im, s5_d=s5_d, s5_w_glu=s5_w_glu, s5_b_glu=s5_b_glu,
             w_out0=w_out0, w_in1=w_in1, b_fgt=b_fgt, w_out1=w_out1, ffn_w_up=ffn_w_up,
             ffn_w_gate=ffn_w_gate, ffn_conv_w=ffn_conv_w, ffn_conv_b=ffn_conv_b,
             ffn_w_down=ffn_w_down, ln_g=ln_g, ln_b=ln_b)
    s5p = _derived_params(p)
    prompt, sample = _forward(x_prompt[0], x_sample[:, 0], state_rglru_conv, state_rglru_h,
                              state_s5_re, state_s5_im, cache_k, cache_v, cache_logf,
                              state_ffn_conv, page_table, p, s5p, PROMPT_TILES, PAGES_PER_STEP)
    return (prompt[0], sample[0]) + tuple(prompt[1:]) + tuple(sample[1:])
```
